```python
import math
import jax, jax.numpy as jnp
from jax import lax
import numpy as np

D_MODEL = 1024
BATCH = 16
SEQ = 4096
DEPTH = 1
DEC_BATCH = 8
DEC_SEQ = 8192
PAST_LEN = 128

HEAD_DIM = 64
N_HEADS_A = 8
N_HEADS_B = 8
WIDTH_A = N_HEADS_A * HEAD_DIM
WIDTH_B = N_HEADS_B * HEAD_DIM
MIX_WIDTH = WIDTH_A + WIDTH_B
DILATED_PATTERNS = ((128, 1), (512, 4), (2048, 16))
GRID_W = 64
NA_ROWS_MAX = 8
NA_COLS = 16
N_MEM = 256
N_HEADS_X = 4
HEAD_DIM_X = D_MODEL // N_HEADS_X
D_FF = 4 * D_MODEL
ROPE_THETA = 10000.0
LN_EPS = 1e-5
ALPHA = (2 * DEPTH) ** 0.25
BETA = (8 * DEPTH) ** -0.25
NEG_INF = -1e30

kernel_name = 'hybrid_dilated_neighbourhood_encoder'


def layer_norm(x, g, b):
    xf = x.astype(jnp.float32)
    mu = jnp.mean(xf, axis=-1, keepdims=True)
    var = jnp.mean(jnp.square(xf - mu), axis=-1, keepdims=True)
    y = (xf - mu) * lax.rsqrt(var + LN_EPS)
    return (y * g.astype(jnp.float32) + b.astype(jnp.float32)).astype(x.dtype)


def rms_norm(x, g):
    xf = x.astype(jnp.float32)
    y = xf * lax.rsqrt(jnp.mean(jnp.square(xf), axis=-1, keepdims=True) + LN_EPS)
    return y * g.astype(jnp.float32)


def rotary(x):
    T = x.shape[1]
    half = HEAD_DIM // 2
    inv = ROPE_THETA ** (-jnp.arange(half, dtype=jnp.float32) / half)
    ang = jnp.arange(T, dtype=jnp.float32)[:, None] * inv[None, :]
    cos = jnp.cos(ang)[None, :, None, :]
    sin = jnp.sin(ang)[None, :, None, :]
    x1, x2 = x[..., :half], x[..., half:]
    return jnp.concatenate([x1 * cos - x2 * sin, x2 * cos + x1 * sin], axis=-1)


def dilated_window_attention(q, k, v, window, dilation):
    B, T, H, hd = q.shape
    n_side = (window // 2) // dilation
    L = T // dilation
    nb = -(-L // n_side)
    Lp = nb * n_side

    def to_sub(a):
        return a.reshape(B, L, dilation, H, hd).transpose(0, 2, 1, 3, 4)

    qs, ks, vs = to_sub(q), to_sub(k), to_sub(v)
    qs = jnp.pad(qs, ((0, 0), (0, 0), (0, Lp - L), (0, 0), (0, 0)))
    kpad = ((0, 0), (0, 0), (n_side, Lp - L + n_side), (0, 0), (0, 0))
    ks = jnp.pad(ks, kpad)
    vs = jnp.pad(vs, kpad)
    qb = qs.reshape(B, dilation, nb, n_side, H, hd)
    kidx = n_side * jnp.arange(nb)[:, None] + jnp.arange(3 * n_side)[None, :]
    kb = ks[:, :, kidx]
    vb = vs[:, :, kidx]
    qi = n_side * jnp.arange(nb)[:, None] + jnp.arange(n_side)[None, :]
    kj = kidx - n_side
    valid = ((kj[:, None, :] >= 0) & (kj[:, None, :] < L)
             & (jnp.abs(kj[:, None, :] - qi[:, :, None]) <= n_side))
    s = jnp.einsum('bdnqhc,bdnkhc->bdnhqk', qb, kb) * (hd ** -0.5)
    s = jnp.where(valid[None, None, :, None], s, NEG_INF)
    m = jnp.max(s, axis=-1, keepdims=True)
    p = jnp.exp(s - m)
    den = jnp.sum(p, axis=-1)
    o = jnp.einsum('bdnhqk,bdnkhc->bdnqhc', p, vb) / den.transpose(0, 1, 2, 4, 3)[..., None]
    lse = m[..., 0] + jnp.log(den)
    o = o.reshape(B, dilation, Lp, H, hd)[:, :, :L].transpose(0, 2, 1, 3, 4).reshape(B, T, H, hd)
    lse = lse.transpose(0, 1, 2, 4, 3).reshape(B, dilation, Lp, H)[:, :, :L]
    lse = lse.transpose(0, 2, 1, 3).reshape(B, T, H)
    return o, lse


def dilated_mixture_attention(q, k, v):
    outs, lses = [], []
    for window, dilation in DILATED_PATTERNS:
        o, lse = dilated_window_attention(q, k, v, window, dilation)
        outs.append(o)
        lses.append(lse)
    wts = jax.nn.softmax(jnp.stack(lses, axis=0), axis=0)
    return jnp.einsum('pbth,pbthc->bthc', wts, jnp.stack(outs, axis=0))


def neighbourhood_attention(q, k, v, rpb):
    B, T, H, hd = q.shape
    rows = T // GRID_W
    kh = min(NA_ROWS_MAX, rows)
    r = jnp.arange(rows)
    rs = jnp.clip(r - kh // 2, 0, rows - kh)
    row_idx = rs[:, None] + jnp.arange(kh)[None, :]
    c = jnp.arange(GRID_W)
    cs = jnp.clip(c - NA_COLS // 2, 0, GRID_W - NA_COLS)
    qg = q.reshape(B, rows, GRID_W, H, hd)
    kg = k.reshape(B, rows, GRID_W, H, hd)[:, row_idx]
    vg = v.reshape(B, rows, GRID_W, H, hd)[:, row_idx]
    s = jnp.einsum('brqhc,brawhc->brhqaw', qg, kg) * (hd ** -0.5)
    dr = row_idx - r[:, None]
    dc = c[None, :] - c[:, None]
    ridx = (dr + NA_ROWS_MAX - 1)[:, :, None, None]
    cidx = (jnp.clip(dc, -(NA_COLS - 1), NA_COLS - 1) + NA_COLS - 1)[None, None]
    bias = rpb.astype(jnp.float32)[:, ridx, cidx]
    s = s + bias.transpose(1, 0, 3, 2, 4)[None]
    col_ok = (c[None, :] >= cs[:, None]) & (c[None, :] < cs[:, None] + NA_COLS)
    s = jnp.where(col_ok[:, None, :], s, NEG_INF)
    p = jax.nn.softmax(s.reshape(s.shape[:4] + (kh * GRID_W,)), axis=-1).reshape(s.shape)
    o = jnp.einsum('brhqaw,brawhc->brqhc', p, vg)
    return o.reshape(B, T, H, hd)


def token_mixer(x, w_in, rpb, g_mix_a, g_mix_b, w_out):
    B, T, _ = x.shape
    proj = (x @ w_in).astype(jnp.float32)
    splits = [WIDTH_A, 2 * WIDTH_A, 3 * WIDTH_A, 3 * WIDTH_A + WIDTH_B, 3 * WIDTH_A + 2 * WIDTH_B]
    qa, ka, va, qb, kb, vb = jnp.split(proj, splits, axis=-1)
    qa = rotary(qa.reshape(B, T, N_HEADS_A, HEAD_DIM))
    ka = rotary(ka.reshape(B, T, N_HEADS_A, HEAD_DIM))
    va = va.reshape(B, T, N_HEADS_A, HEAD_DIM)
    oa = dilated_mixture_attention(qa, ka, va).reshape(B, T, WIDTH_A)
    ob = neighbourhood_attention(qb.reshape(B, T, N_HEADS_B, HEAD_DIM),
                                 kb.reshape(B, T, N_HEADS_B, HEAD_DIM),
                                 vb.reshape(B, T, N_HEADS_B, HEAD_DIM), rpb).reshape(B, T, WIDTH_B)
    y = jnp.concatenate([rms_norm(oa, g_mix_a), rms_norm(ob, g_mix_b)], axis=-1).astype(x.dtype)
    return y @ w_out


def memory_cross_attention(x, mem, w_xq, w_xkv, w_xo):
    B, T, D = x.shape
    M = mem.shape[1]
    q = (x @ w_xq).astype(jnp.float32).reshape(B, T, N_HEADS_X, HEAD_DIM_X)
    kv = (mem @ w_xkv).astype(jnp.float32).reshape(B, M, 2, N_HEADS_X, HEAD_DIM_X)
    k, v = kv[:, :, 0], kv[:, :, 1]
    s = jnp.einsum('bthc,bmhc->bhtm', q, k) * (HEAD_DIM_X ** -0.5)
    p = jax.nn.softmax(s, axis=-1)
    o = jnp.einsum('bhtm,bmhc->bthc', p, v).reshape(B, T, D).astype(x.dtype)
    return o @ w_xo


def squared_relu_mlp(x, w_up, w_down):
    return jnp.square(jax.nn.relu(x @ w_up)) @ w_down


def encoder_trunk(x, mem, ln_in_g, ln_in_b, w_in, rpb, g_mix_a, g_mix_b, w_out, ln1_g, ln1_b,
                  w_xq, w_xkv, w_xo, ln2_g, ln2_b, w_up, w_down, ln3_g, ln3_b):
    x = layer_norm(x, ln_in_g, ln_in_b)
    for l in range(DEPTH):
        x = layer_norm(ALPHA * x + token_mixer(x, w_in[l], rpb[l], g_mix_a[l], g_mix_b[l], w_out[l]),
                       ln1_g[l], ln1_b[l])
        x = layer_norm(ALPHA * x + memory_cross_attention(x, mem, w_xq[l], w_xkv[l], w_xo[l]),
                       ln2_g[l], ln2_b[l])
        x = layer_norm(ALPHA * x + squared_relu_mlp(x, w_up[l], w_down[l]), ln3_g[l], ln3_b[l])
    return x


def setup_inputs(seed: int = 0) -> dict:
    key = jax.random.key(seed)
    ks = jax.random.split(key, 24)
    D = D_MODEL

    def nrm(k, shape, scale):
        return jax.random.normal(k, shape, jnp.float32) * scale

    return {
        'x_prompt': nrm(ks[0], (BATCH, SEQ, D), 1.0),
        'x_sample': nrm(ks[1], (DEC_BATCH, DEC_SEQ, D), 1.0),
        'mem_prompt': nrm(ks[2], (BATCH, N_MEM, D), 1.0),
        'mem_sample': nrm(ks[3], (DEC_BATCH, N_MEM, D), 1.0),
        'ln_in_g': 1.0 + nrm(ks[4], (D,), 0.05),
        'ln_in_b': nrm(ks[5], (D,), 0.02),
        'w_in': nrm(ks[6], (DEPTH, D, 3 * MIX_WIDTH), D ** -0.5),
        'rpb': nrm(ks[7], (DEPTH, N_HEADS_B, 2 * NA_ROWS_MAX - 1, 2 * NA_COLS - 1), 0.5),
        'g_mix_a': 1.0 + nrm(ks[8], (DEPTH, WIDTH_A), 0.05),
        'g_mix_b': 1.0 + nrm(ks[9], (DEPTH, WIDTH_B), 0.05),
        'w_out': nrm(ks[10], (DEPTH, MIX_WIDTH, D), BETA * MIX_WIDTH ** -0.5),
        'ln1_g': 1.0 + nrm(ks[11], (DEPTH, D), 0.05),
        'ln1_b': nrm(ks[12], (DEPTH, D), 0.02),
        'w_xq': nrm(ks[13], (DEPTH, D, D), D ** -0.5),
        'w_xkv': nrm(ks[14], (DEPTH, D, 2 * D), D ** -0.5),
        'w_xo': nrm(ks[15], (DEPTH, D, D), BETA * D ** -0.5),
        'ln2_g': 1.0 + nrm(ks[16], (DEPTH, D), 0.05),
        'ln2_b': nrm(ks[17], (DEPTH, D), 0.02),
        'w_up': nrm(ks[18], (DEPTH, D, D_FF), D ** -0.5),
        'w_down': nrm(ks[19], (DEPTH, D_FF, D), BETA * D_FF ** -0.5),
        'ln3_g': 1.0 + nrm(ks[20], (DEPTH, D), 0.05),
        'ln3_b': nrm(ks[21], (DEPTH, D), 0.02),
    }


def reference(x_prompt, x_sample, mem_prompt, mem_sample, ln_in_g, ln_in_b, w_in, rpb, g_mix_a,
              g_mix_b, w_out, ln1_g, ln1_b, w_xq, w_xkv, w_xo, ln2_g, ln2_b, w_up, w_down,
              ln3_g, ln3_b):
    y_prompt = encoder_trunk(x_prompt, mem_prompt, ln_in_g, ln_in_b, w_in, rpb, g_mix_a, g_mix_b,
                             w_out, ln1_g, ln1_b, w_xq, w_xkv, w_xo, ln2_g, ln2_b, w_up, w_down,
                             ln3_g, ln3_b)
    y_sample = encoder_trunk(x_sample, mem_sample, ln_in_g, ln_in_b, w_in, rpb, g_mix_a, g_mix_b,
                             w_out, ln1_g, ln1_b, w_xq, w_xkv, w_xo, ln2_g, ln2_b, w_up, w_down,
                             ln3_g, ln3_b)
    return (y_prompt, y_sample)
```

```python
import functools

import jax
import jax.numpy as jnp
from jax import lax
from jax.experimental import pallas as pl
from jax.experimental.pallas import tpu as pltpu

F32 = jnp.float32
BF16 = jnp.bfloat16

D_MODEL = 1024
HEAD_DIM = 64
WIDTH_A = 512
WIDTH_B = 512
N_PAIRS = WIDTH_A // 128
GRID_W = 64
NA_ROWS = 8
NA_COLS = 16
N_HEADS_X = 4
HEAD_DIM_X = 256
D_FF = 4096
ROPE_THETA = 10000.0
LN_EPS = 1e-5
ALPHA = 2.0 ** 0.25
NEG_INF = -1e30
N_SIDE = 64
MAX_DIL = 16

VMEM_LIMIT = 56 * 1024 * 1024

NT_DIMS = (((1,), (1,)), ((), ()))


def _ln(x, g, b):
    mu = jnp.mean(x, axis=-1, keepdims=True)
    xc = x - mu
    var = jnp.mean(xc * xc, axis=-1, keepdims=True)
    return xc * lax.rsqrt(var + LN_EPS) * g + b


def _rms(x, g):
    return x * lax.rsqrt(jnp.mean(x * x, axis=-1, keepdims=True) + LN_EPS) * g


def _params(n_axes):
    return pltpu.CompilerParams(dimension_semantics=("arbitrary",) * n_axes,
                                vmem_limit_bytes=VMEM_LIMIT)


def _const_spec(shape):
    nd = len(shape)
    return pl.BlockSpec(shape, lambda *_: (0,) * nd, pipeline_mode=pl.Buffered(1))


def _mem_kv_kernel(mem_ref, w_ref, o_ref):
    o_ref[0] = jnp.dot(mem_ref[0].astype(BF16), w_ref[...],
                       preferred_element_type=F32).astype(BF16)


def _mem_kv(mem, w_xkv):
    B, M, _ = mem.shape
    return pl.pallas_call(
        _mem_kv_kernel,
        grid=(B,),
        in_specs=[pl.BlockSpec((1, M, D_MODEL), lambda b: (b, 0, 0)),
                  _const_spec((D_MODEL, 2 * D_MODEL))],
        out_specs=pl.BlockSpec((1, M, 2 * D_MODEL), lambda b: (b, 0, 0)),
        out_shape=jax.ShapeDtypeStruct((B, M, 2 * D_MODEL), BF16),
        compiler_params=_params(1),
        name="mem_kv",
    )(mem, w_xkv)


def _qkv_kernel(x_ref, g_ref, b_ref, w_ref, cos_ref, sin_ref, nat_ref, perm_ref, scr_ref, *, tm):
    xb = _ln(x_ref[0], g_ref[...], b_ref[...]).astype(BF16)
    cos = jnp.concatenate([cos_ref[...]] * 4, axis=1)
    sin = jnp.concatenate([sin_ref[...]] * 4, axis=1)
    lane = lax.broadcasted_iota(jnp.int32, (tm, WIDTH_A), 1)
    first_half = (lane % HEAD_DIM) < HEAD_DIM // 2
    for part in range(6):
        r = jnp.dot(xb, w_ref[:, part * 512:(part + 1) * 512], preferred_element_type=F32)
        if part in (0, 1):
            swapped = jnp.where(first_half, pltpu.roll(r, WIDTH_A - 32, 1), pltpu.roll(r, 32, 1))
            r = r * cos + swapped * sin
        if part in (0, 3):
            r = r * (HEAD_DIM ** -0.5)
        nat_ref[0, :, part * 512:(part + 1) * 512] = r.astype(BF16)
        if part < 3:
            for s in range(4):
                scr_ref[part * 4 + s] = r[:, s * 128:(s + 1) * 128]
    for c in range(MAX_DIL):
        for slab in range(12):
            perm_ref[0, c, :, slab * 128:(slab + 1) * 128] = (
                scr_ref[slab, pl.ds(c, tm // MAX_DIL, stride=MAX_DIL), :].astype(BF16))


def _qkv(x, ln_g, ln_b, w_in, cos, sin, tm):
    B, T, _ = x.shape
    L16 = T // MAX_DIL
    return pl.pallas_call(
        functools.partial(_qkv_kernel, tm=tm),
        grid=(B, T // tm),
        in_specs=[pl.BlockSpec((1, tm, D_MODEL), lambda b, i: (b, i, 0)),
                  _const_spec((1, D_MODEL)), _const_spec((1, D_MODEL)),
                  _const_spec((D_MODEL, 3 * D_MODEL)),
                  pl.BlockSpec((tm, 128), lambda b, i: (i, 0)),
                  pl.BlockSpec((tm, 128), lambda b, i: (i, 0))],
        out_specs=[pl.BlockSpec((1, tm, 3 * D_MODEL), lambda b, i: (b, i, 0)),
                   pl.BlockSpec((1, MAX_DIL, tm // MAX_DIL, 3 * WIDTH_A), lambda b, i: (b, 0, i, 0))],
        out_shape=[jax.ShapeDtypeStruct((B, T, 3 * D_MODEL), BF16),
                   jax.ShapeDtypeStruct((B, MAX_DIL, L16, 3 * WIDTH_A), BF16)],
        scratch_shapes=[pltpu.VMEM((12, tm, 128), F32)],
        compiler_params=_params(2),
        name="qkv",
    )(x, ln_g, ln_b, w_in, cos, sin)


def _pair_tile(q2, k2, v2, bias_lo, bias_hi, mask, lo_q, lo_o):
    zero = jnp.zeros_like(q2)
    parts = []
    for q1, bias in ((jnp.where(lo_q, q2, zero), bias_lo), (jnp.where(lo_q, zero, q2), bias_hi)):
        s = lax.dot_general(q1, k2, NT_DIMS, preferred_element_type=F32)
        if bias is not None:
            s = s + bias
        if mask is not None:
            s = jnp.where(mask, s, NEG_INF)
        m = jnp.max(s, axis=1, keepdims=True)
        p = jnp.exp(s - m)
        l = jnp.sum(p, axis=1, keepdims=True)
        pv = jnp.dot(p.astype(BF16), v2, preferred_element_type=F32)
        parts.append((m, l, pv))
    (m0, l0, a0), (m1, l1, a1) = parts
    return jnp.where(lo_o, m0, m1), jnp.where(lo_o, l0, l1), jnp.where(lo_o, a0, a1)


def _merge(m_r, l_r, a_r, m_t, l_t, a_t):
    m_n = jnp.maximum(m_r, m_t)
    e_r = jnp.exp(m_r - m_n)
    e_t = jnp.exp(m_t - m_n)
    return m_n, l_r * e_r + l_t * e_t, a_r * e_r + a_t * e_t


def _dilated_kernel(qn_ref, kn_ref, vn_ref, qp_ref, kp_ref, vp_ref, o_ref,
                    m_ref, l_ref, a_ref, tmp_ref, *, T):
    BQ, BK = 128, 256
    L16 = T // MAX_DIL
    row = lax.broadcasted_iota(jnp.int32, (BQ, BK), 0)
    col = lax.broadcasted_iota(jnp.int32, (BQ, BK), 1)
    band = row - col
    band4 = 4 * ((row % 32) - (col % 64)) + (row // 32 - col // 64)
    lo_q = lax.broadcasted_iota(jnp.int32, (BQ, 128), 1) < HEAD_DIM
    lo_o = lo_q

    nib = L16 // BQ

    def body16(idx, carry):
        c = idx // nib
        ib = idx % nib
        kst = jnp.clip(ib * BQ - N_SIDE, 0, L16 - BK)
        q0 = pl.multiple_of(c * L16 + ib * BQ, BQ)
        k0 = pl.multiple_of(c * L16 + kst, N_SIDE)
        mask = jnp.abs(band + (ib * BQ - kst)) <= N_SIDE
        m, l, a = _pair_tile(qp_ref[0, pl.ds(q0, BQ), :], kp_ref[0, pl.ds(k0, BK), :],
                             vp_ref[0, pl.ds(k0, BK), :], None, None, mask, lo_q, lo_o)
        m_ref[pl.ds(q0, BQ), :] = m
        l_ref[pl.ds(q0, BQ), :] = l
        a_ref[pl.ds(q0, BQ), :] = a
        return carry

    lax.fori_loop(0, MAX_DIL * nib, body16, 0)

    nb4 = L16 // 32

    def body4(idx, carry):
        r4 = idx // nb4
        ib = idx % nb4
        i0 = ib * 32
        kst = jnp.clip(i0 - 16, 0, L16 - 64)
        qrows = [pl.multiple_of((r4 + 4 * u) * L16 + i0, 32) for u in range(4)]
        krows = [pl.multiple_of((r4 + 4 * u) * L16 + kst, 16) for u in range(4)]
        q2 = jnp.concatenate([qp_ref[0, pl.ds(r, 32), :] for r in qrows], axis=0)
        k2 = jnp.concatenate([kp_ref[0, pl.ds(r, 64), :] for r in krows], axis=0)
        v2 = jnp.concatenate([vp_ref[0, pl.ds(r, 64), :] for r in krows], axis=0)
        mask = jnp.abs(band4 + 4 * (i0 - kst)) <= N_SIDE
        m_t, l_t, a_t = _pair_tile(q2, k2, v2, None, None, mask, lo_q, lo_o)
        m_r = jnp.concatenate([m_ref[pl.ds(r, 32), :] for r in qrows], axis=0)
        l_r = jnp.concatenate([l_ref[pl.ds(r, 32), :] for r in qrows], axis=0)
        a_r = jnp.concatenate([a_ref[pl.ds(r, 32), :] for r in qrows], axis=0)
        m_n, l_n, a_n = _merge(m_r, l_r, a_r, m_t, l_t, a_t)
        for u, r in enumerate(qrows):
            m_ref[pl.ds(r, 32), :] = m_n[u * 32:(u + 1) * 32]
            l_ref[pl.ds(r, 32), :] = l_n[u * 32:(u + 1) * 32]
            a_ref[pl.ds(r, 32), :] = a_n[u * 32:(u + 1) * 32]
        return carry

    lax.fori_loop(0, 4 * nb4, body4, 0)

    def body1(tb, carry):
        t0 = pl.multiple_of(tb * BQ, BQ)
        kst = pl.multiple_of(jnp.clip(t0 - N_SIDE, 0, T - BK), N_SIDE)
        mask = jnp.abs(band + (t0 - kst)) <= N_SIDE
        m_t, l_t, a_t = _pair_tile(qn_ref[0, pl.ds(t0, BQ), :], kn_ref[0, pl.ds(kst, BK), :],
                                   vn_ref[0, pl.ds(kst, BK), :], None, None, mask, lo_q, lo_o)
        i0 = tb * (BQ // MAX_DIL)
        for c in range(MAX_DIL):
            src = pl.multiple_of(c * L16 + i0, BQ // MAX_DIL)
            dst = pl.ds(c, BQ // MAX_DIL, stride=MAX_DIL)
            tmp_ref[0, dst, :] = m_ref[pl.ds(src, BQ // MAX_DIL), :]
            tmp_ref[1, dst, :] = l_ref[pl.ds(src, BQ // MAX_DIL), :]
            tmp_ref[2, dst, :] = a_ref[pl.ds(src, BQ // MAX_DIL), :]
        _, l_n, a_n = _merge(tmp_ref[0], tmp_ref[1], tmp_ref[2], m_t, l_t, a_t)
        o_ref[0, pl.ds(t0, BQ), :] = (a_n / l_n).astype(BF16)
        return carry

    lax.fori_loop(0, T // BQ, body1, 0)


def _dilated(qkv_nat, qkv_perm):
    B, T, _ = qkv_nat.shape
    assert T % (MAX_DIL * 128) == 0 and T // MAX_DIL >= 256
    seq = lambda col0: pl.BlockSpec((1, T, 128), lambda b, j: (b, 0, col0 + j))
    return pl.pallas_call(
        functools.partial(_dilated_kernel, T=T),
        grid=(B, N_PAIRS),
        in_specs=[seq(0), seq(N_PAIRS), seq(2 * N_PAIRS), seq(0), seq(N_PAIRS), seq(2 * N_PAIRS)],
        out_specs=pl.BlockSpec((1, T, 128), lambda b, j: (b, 0, j)),
        out_shape=jax.ShapeDtypeStruct((B, T, WIDTH_A), BF16),
        scratch_shapes=[pltpu.VMEM((T, 128), F32), pltpu.VMEM((T, 128), F32),
                        pltpu.VMEM((T, 128), F32), pltpu.VMEM((3, 128, 128), F32)],
        compiler_params=_params(2),
        name="dilated",
    )(qkv_nat, qkv_nat, qkv_nat, qkv_perm, qkv_perm, qkv_perm)


def _nbr_kernel(q_ref, k_ref, v_ref, bias_ref, o_ref, *, T):
    rows = T // GRID_W
    BK = NA_ROWS * GRID_W
    lo_q = lax.broadcasted_iota(jnp.int32, (GRID_W, 128), 1) < HEAD_DIM

    def body(r, carry):
        rs = jnp.clip(r - NA_ROWS // 2, 0, rows - NA_ROWS)
        var = rs - r + (NA_ROWS - 1)
        q0 = pl.multiple_of(r * GRID_W, GRID_W)
        k0 = pl.multiple_of(rs * GRID_W, GRID_W)
        _, l, a = _pair_tile(q_ref[0, pl.ds(q0, GRID_W), :], k_ref[0, pl.ds(k0, BK), :],
                             v_ref[0, pl.ds(k0, BK), :], bias_ref[0, var], bias_ref[1, var],
                             None, lo_q, lo_q)
        o_ref[0, pl.ds(q0, GRID_W), :] = (a / l).astype(BF16)
        return carry

    lax.fori_loop(0, rows, body, 0)


def _nbr(qkv_nat, bias):
    B, T, _ = qkv_nat.shape
    assert T % GRID_W == 0 and T // GRID_W >= NA_ROWS
    BK = NA_ROWS * GRID_W
    seq = lambda col0: pl.BlockSpec((1, T, 128), lambda b, j: (b, 0, col0 + j))
    return pl.pallas_call(
        functools.partial(_nbr_kernel, T=T),
        grid=(B, N_PAIRS),
        in_specs=[seq(3 * N_PAIRS), seq(4 * N_PAIRS), seq(5 * N_PAIRS),
                  pl.BlockSpec((2, NA_ROWS, GRID_W, BK), lambda b, j: (j, 0, 0, 0))],
        out_specs=pl.BlockSpec((1, T, 128), lambda b, j: (b, 0, j)),
        out_shape=jax.ShapeDtypeStruct((B, T, WIDTH_B), BF16),
        compiler_params=_params(2),
        name="nbr",
    )(qkv_nat, qkv_nat, qkv_nat, bias)


def _nbr_bias(rpb):
    qc = jnp.arange(GRID_W)[:, None]
    kc = jnp.arange(GRID_W)[None, :]
    cidx = jnp.clip(kc - qc, -(NA_COLS - 1), NA_COLS - 1) + NA_COLS - 1
    cs = jnp.clip(qc - NA_COLS // 2, 0, GRID_W - NA_COLS)
    col_ok = (kc >= cs) & (kc < cs + NA_COLS)
    ridx = jnp.arange(NA_ROWS)[:, None] + jnp.arange(NA_ROWS)[None, :]
    t = rpb.astype(F32)[:, ridx][:, :, :, cidx]
    t = jnp.where(col_ok[None, None, None], t, NEG_INF)
    t = t.transpose(0, 1, 3, 2, 4)
    return t.reshape(rpb.shape[0], NA_ROWS, GRID_W, NA_ROWS * GRID_W)


def _mix_out_kernel(oa_ref, ob_ref, x_ref, ga_ref, gb_ref, g0_ref, b0_ref, w_ref, g1_ref, b1_ref,
                    o_ref):
    ya = _rms(oa_ref[0].astype(F32), ga_ref[...])
    yb = _rms(ob_ref[0].astype(F32), gb_ref[...])
    y = jnp.concatenate([ya, yb], axis=1).astype(BF16)
    z = jnp.dot(y, w_ref[...], preferred_element_type=F32)
    x0 = _ln(x_ref[0], g0_ref[...], b0_ref[...])
    o_ref[0] = _ln(ALPHA * x0 + z, g1_ref[...], b1_ref[...])


def _mix_out(oa, ob, x, g_a, g_b, ln_in_g, ln_in_b, w_out, ln1_g, ln1_b, tm):
    B, T, _ = x.shape
    vec = lambda n: _const_spec((1, n))
    return pl.pallas_call(
        _mix_out_kernel,
        grid=(B, T // tm),
        in_specs=[pl.BlockSpec((1, tm, WIDTH_A), lambda b, i: (b, i, 0)),
                  pl.BlockSpec((1, tm, WIDTH_B), lambda b, i: (b, i, 0)),
                  pl.BlockSpec((1, tm, D_MODEL), lambda b, i: (b, i, 0)),
                  vec(WIDTH_A), vec(WIDTH_B), vec(D_MODEL), vec(D_MODEL),
                  _const_spec((D_MODEL, D_MODEL)), vec(D_MODEL), vec(D_MODEL)],
        out_specs=pl.BlockSpec((1, tm, D_MODEL), lambda b, i: (b, i, 0)),
        out_shape=jax.ShapeDtypeStruct((B, T, D_MODEL), F32),
        compiler_params=_params(2),
        name="mix_out",
    )(oa, ob, x, g_a, g_b, ln_in_g, ln_in_b, w_out, ln1_g, ln1_b)


def _cross_kernel(x_ref, kv_ref, wq_ref, wo_ref, g_ref, b_ref, o_ref):
    x = x_ref[0]
    q = jnp.dot(x.astype(BF16), wq_ref[...], preferred_element_type=F32)
    q = (q * (HEAD_DIM_X ** -0.5)).astype(BF16)
    outs = []
    for h in range(N_HEADS_X):
        k = kv_ref[0, :, h * HEAD_DIM_X:(h + 1) * HEAD_DIM_X]
        v = kv_ref[0, :, D_MODEL + h * HEAD_DIM_X:D_MODEL + (h + 1) * HEAD_DIM_X]
        s = lax.dot_general(q[:, h * HEAD_DIM_X:(h + 1) * HEAD_DIM_X], k, NT_DIMS,
                            preferred_element_type=F32)
        p = jnp.exp(s - jnp.max(s, axis=1, keepdims=True))
        l = jnp.sum(p, axis=1, keepdims=True)
        outs.append(jnp.dot(p.astype(BF16), v, preferred_element_type=F32) / l)
    o = jnp.concatenate(outs, axis=1).astype(BF16)
    z = jnp.dot(o, wo_ref[...], preferred_element_type=F32)
    o_ref[0] = _ln(ALPHA * x + z, g_ref[...], b_ref[...])


def _cross(x, kv, w_xq, w_xo, ln_g, ln_b, tm):
    B, T, _ = x.shape
    M = kv.shape[1]
    return pl.pallas_call(
        _cross_kernel,
        grid=(B, T // tm),
        in_specs=[pl.BlockSpec((1, tm, D_MODEL), lambda b, i: (b, i, 0)),
                  pl.BlockSpec((1, M, 2 * D_MODEL), lambda b, i: (b, 0, 0)),
                  _const_spec((D_MODEL, D_MODEL)), _const_spec((D_MODEL, D_MODEL)),
                  _const_spec((1, D_MODEL)), _const_spec((1, D_MODEL))],
        out_specs=pl.BlockSpec((1, tm, D_MODEL), lambda b, i: (b, i, 0)),
        out_shape=jax.ShapeDtypeStruct((B, T, D_MODEL), F32),
        compiler_params=_params(2),
        name="cross",
    )(x, kv, w_xq, w_xo, ln_g, ln_b)


def _mlp_kernel(x_ref, wu_ref, wd_ref, g_ref, b_ref, o_ref):
    x = x_ref[0]
    h = jnp.dot(x.astype(BF16), wu_ref[...], preferred_element_type=F32)
    h = jnp.maximum(h, 0.0)
    h = (h * h).astype(BF16)
    z = jnp.dot(h, wd_ref[...], preferred_element_type=F32)
    o_ref[0] = _ln(ALPHA * x + z, g_ref[...], b_ref[...])


def _mlp(x, w_up, w_down, ln_g, ln_b, tm):
    B, T, _ = x.shape
    return pl.pallas_call(
        _mlp_kernel,
        grid=(B, T // tm),
        in_specs=[pl.BlockSpec((1, tm, D_MODEL), lambda b, i: (b, i, 0)),
                  _const_spec((D_MODEL, D_FF)), _const_spec((D_FF, D_MODEL)),
                  _const_spec((1, D_MODEL)), _const_spec((1, D_MODEL))],
        out_specs=pl.BlockSpec((1, tm, D_MODEL), lambda b, i: (b, i, 0)),
        out_shape=jax.ShapeDtypeStruct((B, T, D_MODEL), F32),
        compiler_params=_params(2),
        name="mlp",
    )(x, w_up, w_down, ln_g, ln_b)


def _rope_tables(T):
    half = HEAD_DIM // 2
    inv = ROPE_THETA ** (-jnp.arange(half, dtype=F32) / half)
    ang = jnp.arange(T, dtype=F32)[:, None] * inv[None, :]
    cos, sin = jnp.cos(ang), jnp.sin(ang)
    return (jnp.concatenate([cos, cos, cos, cos], axis=1),
            jnp.concatenate([-sin, sin, -sin, sin], axis=1))


def _trunk(x, mem, p, tm=256):
    B, T, _ = x.shape
    cos, sin = _rope_tables(T)
    kv = _mem_kv(mem, p["w_xkv"])
    qkv_nat, qkv_perm = _qkv(x, p["ln_in_g"], p["ln_in_b"], p["w_in"], cos, sin, tm)
    oa = _dilated(qkv_nat, qkv_perm.reshape(B, T, 3 * WIDTH_A))
    ob = _nbr(qkv_nat, p["bias"])
    x1 = _mix_out(oa, ob, x, p["g_mix_a"], p["g_mix_b"], p["ln_in_g"], p["ln_in_b"], p["w_out"],
                  p["ln1_g"], p["ln1_b"], tm)
    x2 = _cross(x1, kv, p["w_xq"], p["w_xo"], p["ln2_g"], p["ln2_b"], tm)
    return _mlp(x2, p["w_up"], p["w_down"], p["ln3_g"], p["ln3_b"], tm)


def kernel(x_prompt, x_sample, mem_prompt, mem_sample, ln_in_g, ln_in_b, w_in, rpb, g_mix_a, g_mix_b,
           w_out, ln1_g, ln1_b, w_xq, w_xkv, w_xo, ln2_g, ln2_b, w_up, w_down, ln3_g, ln3_b):
    assert w_in.shape[0] == 1, "single-layer trunk"
    row = lambda v: v.reshape(1, -1).astype(F32)
    p = dict(
        ln_in_g=row(ln_in_g), ln_in_b=row(ln_in_b),
        w_in=w_in[0].astype(BF16), bias=_nbr_bias(rpb[0]),
        g_mix_a=row(g_mix_a[0]), g_mix_b=row(g_mix_b[0]), w_out=w_out[0].astype(BF16),
        ln1_g=row(ln1_g[0]), ln1_b=row(ln1_b[0]),
        w_xq=w_xq[0].astype(BF16), w_xkv=w_xkv[0].astype(BF16), w_xo=w_xo[0].astype(BF16),
        ln2_g=row(ln2_g[0]), ln2_b=row(ln2_b[0]),
        w_up=w_up[0].astype(BF16), w_down=w_down[0].astype(BF16),
        ln3_g=row(ln3_g[0]), ln3_b=row(ln3_b[0]),
    )
    return _trunk(x_prompt, mem_prompt, p), _trunk(x_sample, mem_sample, p)
```

```python
import functools

import jax
import jax.numpy as jnp
from jax import lax
from jax.experimental import pallas as pl
from jax.experimental.pallas import tpu as pltpu

F32 = jnp.float32
BF16 = jnp.bfloat16

D_MODEL = 1024
HEAD_DIM = 64
WIDTH_A = 512
WIDTH_B = 512
N_PAIRS = WIDTH_A // 128
GRID_W = 64
NA_ROWS = 8
NA_COLS = 16
N_HEADS_X = 4
HEAD_DIM_X = 256
D_FF = 4096
ROPE_THETA = 10000.0
LN_EPS = 1e-5
ALPHA = 2.0 ** 0.25
NEG_INF = -1e30
N_SIDE = 64
MAX_DIL = 16

VMEM_LIMIT = 56 * 1024 * 1024

NT_DIMS = (((1,), (1,)), ((), ()))


def _ln(x, g, b):
    mu = jnp.mean(x, axis=-1, keepdims=True)
    xc = x - mu
    var = jnp.mean(xc * xc, axis=-1, keepdims=True)
    return xc * lax.rsqrt(var + LN_EPS) * g + b


def _rms(x, g):
    return x * lax.rsqrt(jnp.mean(x * x, axis=-1, keepdims=True) + LN_EPS) * g


def _params(n_axes):
    return pltpu.CompilerParams(dimension_semantics=("arbitrary",) * n_axes,
                                vmem_limit_bytes=VMEM_LIMIT)


def _const_spec(shape):
    nd = len(shape)
    return pl.BlockSpec(shape, lambda *_: (0,) * nd, pipeline_mode=pl.Buffered(1))


def _mem_kv_kernel(mem_ref, w_ref, o_ref):
    o_ref[0] = jnp.dot(mem_ref[0].astype(BF16), w_ref[...],
                       preferred_element_type=F32).astype(BF16)


def _mem_kv(mem, w_xkv):
    B, M, _ = mem.shape
    return pl.pallas_call(
        _mem_kv_kernel,
        grid=(B,),
        in_specs=[pl.BlockSpec((1, M, D_MODEL), lambda b: (b, 0, 0)),
                  _const_spec((D_MODEL, 2 * D_MODEL))],
        out_specs=pl.BlockSpec((1, M, 2 * D_MODEL), lambda b: (b, 0, 0)),
        out_shape=jax.ShapeDtypeStruct((B, M, 2 * D_MODEL), BF16),
        compiler_params=_params(1),
        name="mem_kv",
    )(mem, w_xkv)


def _qkv_kernel(x_ref, g_ref, b_ref, w_ref, cos_ref, sin_ref, nat_ref, perm_ref, scr_ref, *, tm):
    xb = _ln(x_ref[0], g_ref[...], b_ref[...]).astype(BF16)
    cos = jnp.concatenate([cos_ref[...]] * 4, axis=1)
    sin = jnp.concatenate([sin_ref[...]] * 4, axis=1)
    lane = lax.broadcasted_iota(jnp.int32, (tm, WIDTH_A), 1)
    first_half = (lane % HEAD_DIM) < HEAD_DIM // 2
    for part in range(6):
        r = jnp.dot(xb, w_ref[:, part * 512:(part + 1) * 512], preferred_element_type=F32)
        if part in (0, 1):
            swapped = jnp.where(first_half, pltpu.roll(r, WIDTH_A - 32, 1), pltpu.roll(r, 32, 1))
            r = r * cos + swapped * sin
        if part in (0, 3):
            r = r * (HEAD_DIM ** -0.5)
        nat_ref[0, :, part * 512:(part + 1) * 512] = r.astype(BF16)
        if part < 3:
            for s in range(4):
                scr_ref[part * 4 + s] = r[:, s * 128:(s + 1) * 128]
    for c in range(MAX_DIL):
        for slab in range(12):
            perm_ref[0, c, :, slab * 128:(slab + 1) * 128] = (
                scr_ref[slab, pl.ds(c, tm // MAX_DIL, stride=MAX_DIL), :].astype(BF16))


def _qkv(x, ln_g, ln_b, w_in, cos, sin, tm):
    B, T, _ = x.shape
    L16 = T // MAX_DIL
    return pl.pallas_call(
        functools.partial(_qkv_kernel, tm=tm),
        grid=(B, T // tm),
        in_specs=[pl.BlockSpec((1, tm, D_MODEL), lambda b, i: (b, i, 0)),
                  _const_spec((1, D_MODEL)), _const_spec((1, D_MODEL)),
                  _const_spec((D_MODEL, 3 * D_MODEL)),
                  pl.BlockSpec((tm, 128), lambda b, i: (i, 0)),
                  pl.BlockSpec((tm, 128), lambda b, i: (i, 0))],
        out_specs=[pl.BlockSpec((1, tm, 3 * D_MODEL), lambda b, i: (b, i, 0)),
                   pl.BlockSpec((1, MAX_DIL, tm // MAX_DIL, 3 * WIDTH_A), lambda b, i: (b, 0, i, 0))],
        out_shape=[jax.ShapeDtypeStruct((B, T, 3 * D_MODEL), BF16),
                   jax.ShapeDtypeStruct((B, MAX_DIL, L16, 3 * WIDTH_A), BF16)],
        scratch_shapes=[pltpu.VMEM((12, tm, 128), F32)],
        compiler_params=_params(2),
        name="qkv",
    )(x, ln_g, ln_b, w_in, cos, sin)


UNROLL = 4


def _pair_tile(q2, k2, v2, bias_lo, bias_hi, lo):
    zero = jnp.zeros_like(q2)
    parts = []
    for q1, bias in ((jnp.where(lo, q2, zero), bias_lo), (jnp.where(lo, zero, q2), bias_hi)):
        s = lax.dot_general(q1, k2, NT_DIMS, preferred_element_type=F32) + bias
        m = jnp.max(s, axis=1, keepdims=True)
        p = jnp.exp(s - m)
        l = jnp.sum(p, axis=1, keepdims=True)
        pv = jnp.dot(p.astype(BF16), v2, preferred_element_type=F32)
        parts.append((m, l, pv))
    (m0, l0, a0), (m1, l1, a1) = parts
    return jnp.where(lo, m0, m1), jnp.where(lo, l0, l1), jnp.where(lo, a0, a1)


def _merge(m_r, l_r, a_r, m_t, l_t, a_t):
    m_n = jnp.maximum(m_r, m_t)
    e_r = jnp.exp(m_r - m_n)
    e_t = jnp.exp(m_t - m_n)
    return m_n, l_r * e_r + l_t * e_t, a_r * e_r + a_t * e_t


def _dilated_kernel(qn_ref, kn_ref, vn_ref, qp_ref, kp_ref, vp_ref, o_ref,
                    m_ref, l_ref, a_ref, tmp_ref, mask_ref, *, T):
    BQ, BK, U = 128, 256, UNROLL
    L16 = T // MAX_DIL
    row = lax.broadcasted_iota(jnp.int32, (BQ, BK), 0)
    col = lax.broadcasted_iota(jnp.int32, (BQ, BK), 1)
    band = row - col
    band4 = 4 * ((row % 32) - (col % 64)) + (row // 32 - col // 64)
    for n in range(3):
        mask_ref[n] = jnp.where(jnp.abs(band + N_SIDE * n) <= N_SIDE, 0.0, NEG_INF)
        mask_ref[3 + n] = jnp.where(jnp.abs(band4 + N_SIDE * n) <= N_SIDE, 0.0, NEG_INF)
    lo = lax.broadcasted_iota(jnp.int32, (BQ, 128), 1) < HEAD_DIM

    nib = L16 // BQ

    def body16(it, carry):
        done = []
        for u in range(U):
            g = it * U + u
            c = lax.div(g, nib)
            ib = lax.rem(g, nib)
            kst = jnp.clip(ib * BQ - N_SIDE, 0, L16 - BK)
            q0 = pl.multiple_of(c * L16 + ib * BQ, BQ)
            k0 = pl.multiple_of(c * L16 + kst, N_SIDE)
            mask = mask_ref[lax.div(ib * BQ - kst, N_SIDE)]
            done.append((q0, _pair_tile(qp_ref[0, pl.ds(q0, BQ), :], kp_ref[0, pl.ds(k0, BK), :],
                                        vp_ref[0, pl.ds(k0, BK), :], mask, mask, lo)))
        for q0, (m, l, a) in done:
            m_ref[pl.ds(q0, BQ), :] = m
            l_ref[pl.ds(q0, BQ), :] = l
            a_ref[pl.ds(q0, BQ), :] = a
        return carry

    lax.fori_loop(0, MAX_DIL * nib // U, body16, 0)

    nb4 = L16 // 32

    def body4(it, carry):
        done = []
        for u in range(U):
            g = it * U + u
            r4 = lax.div(g, nb4)
            i0 = lax.rem(g, nb4) * 32
            kst = jnp.clip(i0 - 16, 0, L16 - 64)
            qrows = [pl.multiple_of((r4 + 4 * w) * L16 + i0, 32) for w in range(4)]
            krows = [pl.multiple_of((r4 + 4 * w) * L16 + kst, 16) for w in range(4)]
            q2 = jnp.concatenate([qp_ref[0, pl.ds(r, 32), :] for r in qrows], axis=0)
            k2 = jnp.concatenate([kp_ref[0, pl.ds(r, 64), :] for r in krows], axis=0)
            v2 = jnp.concatenate([vp_ref[0, pl.ds(r, 64), :] for r in krows], axis=0)
            mask = mask_ref[3 + lax.div(i0 - kst, 16)]
            m_r = jnp.concatenate([m_ref[pl.ds(r, 32), :] for r in qrows], axis=0)
            l_r = jnp.concatenate([l_ref[pl.ds(r, 32), :] for r in qrows], axis=0)
            a_r = jnp.concatenate([a_ref[pl.ds(r, 32), :] for r in qrows], axis=0)
            done.append((qrows, _merge(m_r, l_r, a_r, *_pair_tile(q2, k2, v2, mask, mask, lo))))
        for qrows, (m_n, l_n, a_n) in done:
            for w, r in enumerate(qrows):
                m_ref[pl.ds(r, 32), :] = m_n[w * 32:(w + 1) * 32]
                l_ref[pl.ds(r, 32), :] = l_n[w * 32:(w + 1) * 32]
                a_ref[pl.ds(r, 32), :] = a_n[w * 32:(w + 1) * 32]
        return carry

    lax.fori_loop(0, 4 * nb4 // U, body4, 0)

    n_i = U * BQ // MAX_DIL

    def body1(it, carry):
        i0 = it * n_i
        for c in range(MAX_DIL):
            src = pl.ds(pl.multiple_of(c * L16 + i0, n_i), n_i)
            dst = pl.ds(c, n_i, stride=MAX_DIL)
            tmp_ref[0, dst, :] = m_ref[src, :]
            tmp_ref[1, dst, :] = l_ref[src, :]
            tmp_ref[2, dst, :] = a_ref[src, :]
        for u in range(U):
            t0 = pl.multiple_of((it * U + u) * BQ, BQ)
            kst = pl.multiple_of(jnp.clip(t0 - N_SIDE, 0, T - BK), N_SIDE)
            mask = mask_ref[lax.div(t0 - kst, N_SIDE)]
            tile = _pair_tile(qn_ref[0, pl.ds(t0, BQ), :], kn_ref[0, pl.ds(kst, BK), :],
                              vn_ref[0, pl.ds(kst, BK), :], mask, mask, lo)
            rows = slice(u * BQ, (u + 1) * BQ)
            _, l_n, a_n = _merge(tmp_ref[0, rows, :], tmp_ref[1, rows, :], tmp_ref[2, rows, :], *tile)
            o_ref[0, pl.ds(t0, BQ), :] = (a_n / l_n).astype(BF16)
        return carry

    lax.fori_loop(0, T // (BQ * U), body1, 0)


def _dilated(qkv_nat, qkv_perm):
    B, T, _ = qkv_nat.shape
    assert T % (MAX_DIL * 128) == 0 and T % (128 * UNROLL) == 0 and T // MAX_DIL >= 256
    seq = lambda col0: pl.BlockSpec((1, T, 128), lambda b, j: (b, 0, col0 + j))
    return pl.pallas_call(
        functools.partial(_dilated_kernel, T=T),
        grid=(B, N_PAIRS),
        in_specs=[seq(0), seq(N_PAIRS), seq(2 * N_PAIRS), seq(0), seq(N_PAIRS), seq(2 * N_PAIRS)],
        out_specs=pl.BlockSpec((1, T, 128), lambda b, j: (b, 0, j)),
        out_shape=jax.ShapeDtypeStruct((B, T, WIDTH_A), BF16),
        scratch_shapes=[pltpu.VMEM((T, 128), F32), pltpu.VMEM((T, 128), F32),
                        pltpu.VMEM((T, 128), F32), pltpu.VMEM((3, UNROLL * 128, 128), F32),
                        pltpu.VMEM((6, 128, 256), F32)],
        compiler_params=_params(2),
        name="dilated",
    )(qkv_nat, qkv_nat, qkv_nat, qkv_perm, qkv_perm, qkv_perm)


def _nbr_kernel(q_ref, k_ref, v_ref, bias_ref, o_ref, *, T):
    rows = T // GRID_W
    BK = NA_ROWS * GRID_W
    lo = lax.broadcasted_iota(jnp.int32, (GRID_W, 128), 1) < HEAD_DIM

    def body(it, carry):
        for u in range(UNROLL):
            r = it * UNROLL + u
            rs = jnp.clip(r - NA_ROWS // 2, 0, rows - NA_ROWS)
            var = rs - r + (NA_ROWS - 1)
            q0 = pl.multiple_of(r * GRID_W, GRID_W)
            k0 = pl.multiple_of(rs * GRID_W, GRID_W)
            _, l, a = _pair_tile(q_ref[0, pl.ds(q0, GRID_W), :], k_ref[0, pl.ds(k0, BK), :],
                                 v_ref[0, pl.ds(k0, BK), :], bias_ref[0, var], bias_ref[1, var], lo)
            o_ref[0, pl.ds(q0, GRID_W), :] = (a / l).astype(BF16)
        return carry

    lax.fori_loop(0, rows // UNROLL, body, 0)


def _nbr(qkv_nat, bias):
    B, T, _ = qkv_nat.shape
    assert T % (GRID_W * UNROLL) == 0 and T // GRID_W >= NA_ROWS
    BK = NA_ROWS * GRID_W
    seq = lambda col0: pl.BlockSpec((1, T, 128), lambda b, j: (b, 0, col0 + j))
    return pl.pallas_call(
        functools.partial(_nbr_kernel, T=T),
        grid=(B, N_PAIRS),
        in_specs=[seq(3 * N_PAIRS), seq(4 * N_PAIRS), seq(5 * N_PAIRS),
                  pl.BlockSpec((2, NA_ROWS, GRID_W, BK), lambda b, j: (j, 0, 0, 0))],
        out_specs=pl.BlockSpec((1, T, 128), lambda b, j: (b, 0, j)),
        out_shape=jax.ShapeDtypeStruct((B, T, WIDTH_B), BF16),
        compiler_params=_params(2),
        name="nbr",
    )(qkv_nat, qkv_nat, qkv_nat, bias)


def _nbr_bias(rpb):
    qc = jnp.arange(GRID_W)[:, None]
    kc = jnp.arange(GRID_W)[None, :]
    cidx = jnp.clip(kc - qc, -(NA_COLS - 1), NA_COLS - 1) + NA_COLS - 1
    cs = jnp.clip(qc - NA_COLS // 2, 0, GRID_W - NA_COLS)
    col_ok = (kc >= cs) & (kc < cs + NA_COLS)
    ridx = jnp.arange(NA_ROWS)[:, None] + jnp.arange(NA_ROWS)[None, :]
    t = rpb.astype(F32)[:, ridx][:, :, :, cidx]
    t = jnp.where(col_ok[None, None, None], t, NEG_INF)
    t = t.transpose(0, 1, 3, 2, 4)
    return t.reshape(rpb.shape[0], NA_ROWS, GRID_W, NA_ROWS * GRID_W)


def _mix_out_kernel(oa_ref, ob_ref, x_ref, ga_ref, gb_ref, g0_ref, b0_ref, w_ref, g1_ref, b1_ref,
                    o_ref):
    ya = _rms(oa_ref[0].astype(F32), ga_ref[...])
    yb = _rms(ob_ref[0].astype(F32), gb_ref[...])
    y = jnp.concatenate([ya, yb], axis=1).astype(BF16)
    z = jnp.dot(y, w_ref[...], preferred_element_type=F32)
    x0 = _ln(x_ref[0], g0_ref[...], b0_ref[...])
    o_ref[0] = _ln(ALPHA * x0 + z, g1_ref[...], b1_ref[...])


def _mix_out(oa, ob, x, g_a, g_b, ln_in_g, ln_in_b, w_out, ln1_g, ln1_b, tm):
    B, T, _ = x.shape
    vec = lambda n: _const_spec((1, n))
    return pl.pallas_call(
        _mix_out_kernel,
        grid=(B, T // tm),
        in_specs=[pl.BlockSpec((1, tm, WIDTH_A), lambda b, i: (b, i, 0)),
                  pl.BlockSpec((1, tm, WIDTH_B), lambda b, i: (b, i, 0)),
                  pl.BlockSpec((1, tm, D_MODEL), lambda b, i: (b, i, 0)),
                  vec(WIDTH_A), vec(WIDTH_B), vec(D_MODEL), vec(D_MODEL),
                  _const_spec((D_MODEL, D_MODEL)), vec(D_MODEL), vec(D_MODEL)],
        out_specs=pl.BlockSpec((1, tm, D_MODEL), lambda b, i: (b, i, 0)),
        out_shape=jax.ShapeDtypeStruct((B, T, D_MODEL), F32),
        compiler_params=_params(2),
        name="mix_out",
    )(oa, ob, x, g_a, g_b, ln_in_g, ln_in_b, w_out, ln1_g, ln1_b)


def _cross_kernel(x_ref, kv_ref, wq_ref, wo_ref, g_ref, b_ref, o_ref):
    x = x_ref[0]
    q = jnp.dot(x.astype(BF16), wq_ref[...], preferred_element_type=F32)
    q = (q * (HEAD_DIM_X ** -0.5)).astype(BF16)
    outs = []
    for h in range(N_HEADS_X):
        k = kv_ref[0, :, h * HEAD_DIM_X:(h + 1) * HEAD_DIM_X]
        v = kv_ref[0, :, D_MODEL + h * HEAD_DIM_X:D_MODEL + (h + 1) * HEAD_DIM_X]
        s = lax.dot_general(q[:, h * HEAD_DIM_X:(h + 1) * HEAD_DIM_X], k, NT_DIMS,
                            preferred_element_type=F32)
        p = jnp.exp(s - jnp.max(s, axis=1, keepdims=True))
        l = jnp.sum(p, axis=1, keepdims=True)
        outs.append(jnp.dot(p.astype(BF16), v, preferred_element_type=F32) / l)
    o = jnp.concatenate(outs, axis=1).astype(BF16)
    z = jnp.dot(o, wo_ref[...], preferred_element_type=F32)
    o_ref[0] = _ln(ALPHA * x + z, g_ref[...], b_ref[...])


def _cross(x, kv, w_xq, w_xo, ln_g, ln_b, tm):
    B, T, _ = x.shape
    M = kv.shape[1]
    return pl.pallas_call(
        _cross_kernel,
        grid=(B, T // tm),
        in_specs=[pl.BlockSpec((1, tm, D_MODEL), lambda b, i: (b, i, 0)),
                  pl.BlockSpec((1, M, 2 * D_MODEL), lambda b, i: (b, 0, 0)),
                  _const_spec((D_MODEL, D_MODEL)), _const_spec((D_MODEL, D_MODEL)),
                  _const_spec((1, D_MODEL)), _const_spec((1, D_MODEL))],
        out_specs=pl.BlockSpec((1, tm, D_MODEL), lambda b, i: (b, i, 0)),
        out_shape=jax.ShapeDtypeStruct((B, T, D_MODEL), F32),
        compiler_params=_params(2),
        name="cross",
    )(x, kv, w_xq, w_xo, ln_g, ln_b)


def _mlp_kernel(x_ref, wu_ref, wd_ref, g_ref, b_ref, o_ref):
    x = x_ref[0]
    h = jnp.dot(x.astype(BF16), wu_ref[...], preferred_element_type=F32)
    h = jnp.maximum(h, 0.0)
    h = (h * h).astype(BF16)
    z = jnp.dot(h, wd_ref[...], preferred_element_type=F32)
    o_ref[0] = _ln(ALPHA * x + z, g_ref[...], b_ref[...])


def _mlp(x, w_up, w_down, ln_g, ln_b, tm):
    B, T, _ = x.shape
    return pl.pallas_call(
        _mlp_kernel,
        grid=(B, T // tm),
        in_specs=[pl.BlockSpec((1, tm, D_MODEL), lambda b, i: (b, i, 0)),
                  _const_spec((D_MODEL, D_FF)), _const_spec((D_FF, D_MODEL)),
                  _const_spec((1, D_MODEL)), _const_spec((1, D_MODEL))],
        out_specs=pl.BlockSpec((1, tm, D_MODEL), lambda b, i: (b, i, 0)),
        out_shape=jax.ShapeDtypeStruct((B, T, D_MODEL), F32),
        compiler_params=_params(2),
        name="mlp",
    )(x, w_up, w_down, ln_g, ln_b)


def _rope_tables(T):
    half = HEAD_DIM // 2
    inv = ROPE_THETA ** (-jnp.arange(half, dtype=F32) / half)
    ang = jnp.arange(T, dtype=F32)[:, None] * inv[None, :]
    cos, sin = jnp.cos(ang), jnp.sin(ang)
    return (jnp.concatenate([cos, cos, cos, cos], axis=1),
            jnp.concatenate([-sin, sin, -sin, sin], axis=1))


def _trunk(x, mem, p, tm=256):
    B, T, _ = x.shape
    cos, sin = _rope_tables(T)
    kv = _mem_kv(mem, p["w_xkv"])
    qkv_nat, qkv_perm = _qkv(x, p["ln_in_g"], p["ln_in_b"], p["w_in"], cos, sin, tm)
    oa = _dilated(qkv_nat, qkv_perm.reshape(B, T, 3 * WIDTH_A))
    ob = _nbr(qkv_nat, p["bias"])
    x1 = _mix_out(oa, ob, x, p["g_mix_a"], p["g_mix_b"], p["ln_in_g"], p["ln_in_b"], p["w_out"],
                  p["ln1_g"], p["ln1_b"], tm)
    x2 = _cross(x1, kv, p["w_xq"], p["w_xo"], p["ln2_g"], p["ln2_b"], tm)
    return _mlp(x2, p["w_up"], p["w_down"], p["ln3_g"], p["ln3_b"], tm)


def kernel(x_prompt, x_sample, mem_prompt, mem_sample, ln_in_g, ln_in_b, w_in, rpb, g_mix_a, g_mix_b,
           w_out, ln1_g, ln1_b, w_xq, w_xkv, w_xo, ln2_g, ln2_b, w_up, w_down, ln3_g, ln3_b):
    assert w_in.shape[0] == 1, "single-layer trunk"
    row = lambda v: v.reshape(1, -1).astype(F32)
    p = dict(
        ln_in_g=row(ln_in_g), ln_in_b=row(ln_in_b),
        w_in=w_in[0].astype(BF16), bias=_nbr_bias(rpb[0]),
        g_mix_a=row(g_mix_a[0]), g_mix_b=row(g_mix_b[0]), w_out=w_out[0].astype(BF16),
        ln1_g=row(ln1_g[0]), ln1_b=row(ln1_b[0]),
        w_xq=w_xq[0].astype(BF16), w_xkv=w_xkv[0].astype(BF16), w_xo=w_xo[0].astype(BF16),
        ln2_g=row(ln2_g[0]), ln2_b=row(ln2_b[0]),
        w_up=w_up[0].astype(BF16), w_down=w_down[0].astype(BF16),
        ln3_g=row(ln3_g[0]), ln3_b=row(ln3_b[0]),
    )
    return _trunk(x_prompt, mem_prompt, p), _trunk(x_sample, mem_sample, p)
```

```python
import functools

import jax
import jax.numpy as jnp
from jax import lax
from jax.experimental import pallas as pl
from jax.experimental.pallas import tpu as pltpu

F32 = jnp.float32
BF16 = jnp.bfloat16

D_MODEL = 1024
HEAD_DIM = 64
WIDTH_A = 512
WIDTH_B = 512
N_PAIRS = WIDTH_A // 128
GRID_W = 64
NA_ROWS = 8
NA_COLS = 16
N_HEADS_X = 4
HEAD_DIM_X = 256
D_FF = 4096
ROPE_THETA = 10000.0
LN_EPS = 1e-5
ALPHA = 2.0 ** 0.25
NEG_INF = -1e30
N_SIDE = 64
MAX_DIL = 16

VMEM_LIMIT = 56 * 1024 * 1024

NT_DIMS = (((1,), (1,)), ((), ()))


def _ln(x, g, b):
    mu = jnp.mean(x, axis=-1, keepdims=True)
    xc = x - mu
    var = jnp.mean(xc * xc, axis=-1, keepdims=True)
    return xc * lax.rsqrt(var + LN_EPS) * g + b


def _rms(x, g):
    return x * lax.rsqrt(jnp.mean(x * x, axis=-1, keepdims=True) + LN_EPS) * g


def _params(n_axes):
    return pltpu.CompilerParams(dimension_semantics=("arbitrary",) * n_axes,
                                vmem_limit_bytes=VMEM_LIMIT)


def _const_spec(shape):
    nd = len(shape)
    return pl.BlockSpec(shape, lambda *_: (0,) * nd, pipeline_mode=pl.Buffered(1))


def _mem_kv_kernel(mem_ref, w_ref, o_ref):
    o_ref[0] = jnp.dot(mem_ref[0].astype(BF16), w_ref[...],
                       preferred_element_type=F32).astype(BF16)


def _mem_kv(mem, w_xkv):
    B, M, _ = mem.shape
    return pl.pallas_call(
        _mem_kv_kernel,
        grid=(B,),
        in_specs=[pl.BlockSpec((1, M, D_MODEL), lambda b: (b, 0, 0)),
                  _const_spec((D_MODEL, 2 * D_MODEL))],
        out_specs=pl.BlockSpec((1, M, 2 * D_MODEL), lambda b: (b, 0, 0)),
        out_shape=jax.ShapeDtypeStruct((B, M, 2 * D_MODEL), BF16),
        compiler_params=_params(1),
        name="mem_kv",
    )(mem, w_xkv)


def _qkv_kernel(x_ref, g_ref, b_ref, w_ref, cos_ref, sin_ref, nat_ref, perm_ref, scr_ref, *, tm):
    xb = _ln(x_ref[0], g_ref[...], b_ref[...]).astype(BF16)
    cos = jnp.concatenate([cos_ref[...]] * 4, axis=1)
    sin = jnp.concatenate([sin_ref[...]] * 4, axis=1)
    lane = lax.broadcasted_iota(jnp.int32, (tm, WIDTH_A), 1)
    first_half = (lane % HEAD_DIM) < HEAD_DIM // 2
    for part in range(6):
        r = jnp.dot(xb, w_ref[:, part * 512:(part + 1) * 512], preferred_element_type=F32)
        if part in (0, 1):
            swapped = jnp.where(first_half, pltpu.roll(r, WIDTH_A - 32, 1), pltpu.roll(r, 32, 1))
            r = r * cos + swapped * sin
        if part in (0, 3):
            r = r * (HEAD_DIM ** -0.5)
        nat_ref[0, :, part * 512:(part + 1) * 512] = r.astype(BF16)
        if part < 3:
            for s in range(4):
                scr_ref[part * 4 + s] = r[:, s * 128:(s + 1) * 128]
    for c in range(MAX_DIL):
        for slab in range(12):
            perm_ref[0, c, :, slab * 128:(slab + 1) * 128] = (
                scr_ref[slab, pl.ds(c, tm // MAX_DIL, stride=MAX_DIL), :].astype(BF16))


def _qkv(x, ln_g, ln_b, w_in, cos, sin, tm):
    B, T, _ = x.shape
    L16 = T // MAX_DIL
    return pl.pallas_call(
        functools.partial(_qkv_kernel, tm=tm),
        grid=(B, T // tm),
        in_specs=[pl.BlockSpec((1, tm, D_MODEL), lambda b, i: (b, i, 0)),
                  _const_spec((1, D_MODEL)), _const_spec((1, D_MODEL)),
                  _const_spec((D_MODEL, 3 * D_MODEL)),
                  pl.BlockSpec((tm, 128), lambda b, i: (i, 0)),
                  pl.BlockSpec((tm, 128), lambda b, i: (i, 0))],
        out_specs=[pl.BlockSpec((1, tm, 3 * D_MODEL), lambda b, i: (b, i, 0)),
                   pl.BlockSpec((1, MAX_DIL, tm // MAX_DIL, 3 * WIDTH_A), lambda b, i: (b, 0, i, 0))],
        out_shape=[jax.ShapeDtypeStruct((B, T, 3 * D_MODEL), BF16),
                   jax.ShapeDtypeStruct((B, MAX_DIL, L16, 3 * WIDTH_A), BF16)],
        scratch_shapes=[pltpu.VMEM((12, tm, 128), F32)],
        compiler_params=_params(2),
        name="qkv",
    )(x, ln_g, ln_b, w_in, cos, sin)


GROUP = 2


def _attn_pipeline(n_groups, load_qk, load_v, finalize, lo, s_scr, p_scr, ml_scr):
    def stage_a(g, slot):
        for t in range(GROUP):
            q2, k2, b_lo, b_hi = load_qk(g, t)
            zero = jnp.zeros_like(q2)
            s_scr[slot, t, 0] = lax.dot_general(jnp.where(lo, q2, zero), k2, NT_DIMS,
                                                preferred_element_type=F32) + b_lo
            s_scr[slot, t, 1] = lax.dot_general(jnp.where(lo, zero, q2), k2, NT_DIMS,
                                                preferred_element_type=F32) + b_hi

    def stage_b(slot):
        for t in range(GROUP):
            ms, ls = [], []
            for h in range(2):
                s = s_scr[slot, t, h]
                m = jnp.max(s, axis=1, keepdims=True)
                p = jnp.exp(s - m)
                p_scr[slot, t, h] = p.astype(BF16)
                ms.append(m)
                ls.append(jnp.sum(p, axis=1, keepdims=True))
            ml_scr[slot, t, 0] = jnp.where(lo, ms[0], ms[1])
            ml_scr[slot, t, 1] = jnp.where(lo, ls[0], ls[1])

    def stage_c(g, slot):
        tiles = []
        for t in range(GROUP):
            v2 = load_v(g, t)
            a = jnp.where(lo, jnp.dot(p_scr[slot, t, 0], v2, preferred_element_type=F32),
                          jnp.dot(p_scr[slot, t, 1], v2, preferred_element_type=F32))
            tiles.append((ml_scr[slot, t, 0], ml_scr[slot, t, 1], a))
        finalize(g, tiles)

    assert n_groups >= 2
    stage_a(0, 0)
    stage_b(0)
    stage_a(1, 1)

    def body(i, carry):
        sa = lax.rem(i, 2)
        stage_c(i, sa)
        stage_b(1 - sa)
        stage_a(i + 2, sa)
        return carry

    lax.fori_loop(0, n_groups - 2, body, 0)
    stage_c(n_groups - 2, n_groups % 2)
    stage_b(1 - n_groups % 2)
    stage_c(n_groups - 1, 1 - n_groups % 2)


def _pipeline_scratch(bq, bk):
    return [pltpu.VMEM((2, GROUP, 2, bq, bk), F32), pltpu.VMEM((2, GROUP, 2, bq, bk), BF16),
            pltpu.VMEM((2, GROUP, 2, bq, 128), F32)]


def _merge(m_r, l_r, a_r, m_t, l_t, a_t):
    m_n = jnp.maximum(m_r, m_t)
    e_r = jnp.exp(m_r - m_n)
    e_t = jnp.exp(m_t - m_n)
    return m_n, l_r * e_r + l_t * e_t, a_r * e_r + a_t * e_t


def _dilated_kernel(qn_ref, kn_ref, vn_ref, qp_ref, kp_ref, vp_ref, o_ref,
                    m_ref, l_ref, a_ref, tmp_ref, mask_ref, s_scr, p_scr, ml_scr, *, T):
    BQ, BK = 128, 256
    L16 = T // MAX_DIL
    row = lax.broadcasted_iota(jnp.int32, (BQ, BK), 0)
    col = lax.broadcasted_iota(jnp.int32, (BQ, BK), 1)
    band = row - col
    band4 = 4 * ((row % 32) - (col % 64)) + (row // 32 - col // 64)
    for n in range(3):
        mask_ref[n] = jnp.where(jnp.abs(band + N_SIDE * n) <= N_SIDE, 0.0, NEG_INF)
        mask_ref[3 + n] = jnp.where(jnp.abs(band4 + N_SIDE * n) <= N_SIDE, 0.0, NEG_INF)
    lo = lax.broadcasted_iota(jnp.int32, (BQ, 128), 1) < HEAD_DIM
    n_groups = T // (BQ * GROUP)
    run = functools.partial(_attn_pipeline, n_groups, lo=lo, s_scr=s_scr, p_scr=p_scr, ml_scr=ml_scr)

    nib = L16 // BQ

    def rows16(g, t):
        tile = jnp.asarray(g * GROUP + t, jnp.int32)
        c = lax.div(tile, nib)
        ib = lax.rem(tile, nib)
        kst = jnp.clip(ib * BQ - N_SIDE, 0, L16 - BK)
        return (pl.multiple_of(c * L16 + ib * BQ, BQ), pl.multiple_of(c * L16 + kst, N_SIDE),
                lax.div(ib * BQ - kst, N_SIDE))

    def load_qk16(g, t):
        q0, k0, case = rows16(g, t)
        mask = mask_ref[case]
        return qp_ref[0, pl.ds(q0, BQ), :], kp_ref[0, pl.ds(k0, BK), :], mask, mask

    def load_v16(g, t):
        return vp_ref[0, pl.ds(rows16(g, t)[1], BK), :]

    def finalize16(g, tiles):
        for t, (m, l, a) in enumerate(tiles):
            q0 = rows16(g, t)[0]
            m_ref[pl.ds(q0, BQ), :] = m
            l_ref[pl.ds(q0, BQ), :] = l
            a_ref[pl.ds(q0, BQ), :] = a

    run(load_qk16, load_v16, finalize16)

    nb4 = L16 // 32

    def rows4(g, t):
        tile = jnp.asarray(g * GROUP + t, jnp.int32)
        r4 = lax.div(tile, nb4)
        i0 = lax.rem(tile, nb4) * 32
        kst = jnp.clip(i0 - 16, 0, L16 - 64)
        return ([pl.multiple_of((r4 + 4 * w) * L16 + i0, 32) for w in range(4)],
                [pl.multiple_of((r4 + 4 * w) * L16 + kst, 16) for w in range(4)],
                3 + lax.div(i0 - kst, 16))

    def load_qk4(g, t):
        qrows, krows, case = rows4(g, t)
        mask = mask_ref[case]
        return (jnp.concatenate([qp_ref[0, pl.ds(r, 32), :] for r in qrows], axis=0),
                jnp.concatenate([kp_ref[0, pl.ds(r, 64), :] for r in krows], axis=0), mask, mask)

    def load_v4(g, t):
        return jnp.concatenate([vp_ref[0, pl.ds(r, 64), :] for r in rows4(g, t)[1]], axis=0)

    def finalize4(g, tiles):
        merged = []
        for t, tile in enumerate(tiles):
            qrows = rows4(g, t)[0]
            state = [jnp.concatenate([ref[pl.ds(r, 32), :] for r in qrows], axis=0)
                     for ref in (m_ref, l_ref, a_ref)]
            merged.append((qrows, _merge(*state, *tile)))
        for qrows, new in merged:
            for ref, val in zip((m_ref, l_ref, a_ref), new):
                for w, r in enumerate(qrows):
                    ref[pl.ds(r, 32), :] = val[w * 32:(w + 1) * 32]

    run(load_qk4, load_v4, finalize4)

    def rows1(g, t):
        t0 = pl.multiple_of(jnp.asarray(g * GROUP + t, jnp.int32) * BQ, BQ)
        kst = pl.multiple_of(jnp.clip(t0 - N_SIDE, 0, T - BK), N_SIDE)
        return t0, kst, lax.div(t0 - kst, N_SIDE)

    def load_qk1(g, t):
        t0, kst, case = rows1(g, t)
        mask = mask_ref[case]
        return qn_ref[0, pl.ds(t0, BQ), :], kn_ref[0, pl.ds(kst, BK), :], mask, mask

    def load_v1(g, t):
        return vn_ref[0, pl.ds(rows1(g, t)[1], BK), :]

    def finalize1(g, tiles):
        n_i = BQ // MAX_DIL
        for t, tile in enumerate(tiles):
            t0 = rows1(g, t)[0]
            i0 = lax.div(t0, MAX_DIL)
            for c in range(MAX_DIL):
                src = pl.ds(pl.multiple_of(c * L16 + i0, n_i), n_i)
                dst = pl.ds(c, n_i, stride=MAX_DIL)
                for n, ref in enumerate((m_ref, l_ref, a_ref)):
                    tmp_ref[t, n, dst, :] = ref[src, :]
            _, l_n, a_n = _merge(tmp_ref[t, 0], tmp_ref[t, 1], tmp_ref[t, 2], *tile)
            o_ref[0, pl.ds(t0, BQ), :] = (a_n / l_n).astype(BF16)

    run(load_qk1, load_v1, finalize1)


def _dilated(qkv_nat, qkv_perm):
    B, T, _ = qkv_nat.shape
    assert T % (MAX_DIL * 128) == 0 and T % (128 * GROUP) == 0 and T // MAX_DIL >= 256
    seq = lambda col0: pl.BlockSpec((1, T, 128), lambda b, j: (b, 0, col0 + j))
    return pl.pallas_call(
        functools.partial(_dilated_kernel, T=T),
        grid=(B, N_PAIRS),
        in_specs=[seq(0), seq(N_PAIRS), seq(2 * N_PAIRS), seq(0), seq(N_PAIRS), seq(2 * N_PAIRS)],
        out_specs=pl.BlockSpec((1, T, 128), lambda b, j: (b, 0, j)),
        out_shape=jax.ShapeDtypeStruct((B, T, WIDTH_A), BF16),
        scratch_shapes=[pltpu.VMEM((T, 128), F32), pltpu.VMEM((T, 128), F32),
                        pltpu.VMEM((T, 128), F32), pltpu.VMEM((GROUP, 3, 128, 128), F32),
                        pltpu.VMEM((6, 128, 256), F32)] + _pipeline_scratch(128, 256),
        compiler_params=_params(2),
        name="dilated",
    )(qkv_nat, qkv_nat, qkv_nat, qkv_perm, qkv_perm, qkv_perm)


def _nbr_kernel(q_ref, k_ref, v_ref, bias_ref, o_ref, s_scr, p_scr, ml_scr, *, T):
    rows = T // GRID_W
    BK = NA_ROWS * GRID_W
    lo = lax.broadcasted_iota(jnp.int32, (GRID_W, 128), 1) < HEAD_DIM

    def where(g, t):
        r = jnp.asarray(g * GROUP + t, jnp.int32)
        rs = jnp.clip(r - NA_ROWS // 2, 0, rows - NA_ROWS)
        return (pl.multiple_of(r * GRID_W, GRID_W), pl.multiple_of(rs * GRID_W, GRID_W),
                rs - r + (NA_ROWS - 1))

    def load_qk(g, t):
        q0, k0, var = where(g, t)
        return (q_ref[0, pl.ds(q0, GRID_W), :], k_ref[0, pl.ds(k0, BK), :],
                bias_ref[0, var], bias_ref[1, var])

    def load_v(g, t):
        return v_ref[0, pl.ds(where(g, t)[1], BK), :]

    def finalize(g, tiles):
        for t, (_, l, a) in enumerate(tiles):
            o_ref[0, pl.ds(where(g, t)[0], GRID_W), :] = (a / l).astype(BF16)

    _attn_pipeline(rows // GROUP, load_qk, load_v, finalize, lo, s_scr, p_scr, ml_scr)


def _nbr(qkv_nat, bias):
    B, T, _ = qkv_nat.shape
    assert T % (GRID_W * GROUP) == 0 and T // GRID_W >= NA_ROWS
    BK = NA_ROWS * GRID_W
    seq = lambda col0: pl.BlockSpec((1, T, 128), lambda b, j: (b, 0, col0 + j))
    return pl.pallas_call(
        functools.partial(_nbr_kernel, T=T),
        grid=(B, N_PAIRS),
        in_specs=[seq(3 * N_PAIRS), seq(4 * N_PAIRS), seq(5 * N_PAIRS),
                  pl.BlockSpec((2, NA_ROWS, GRID_W, BK), lambda b, j: (j, 0, 0, 0))],
        out_specs=pl.BlockSpec((1, T, 128), lambda b, j: (b, 0, j)),
        out_shape=jax.ShapeDtypeStruct((B, T, WIDTH_B), BF16),
        scratch_shapes=_pipeline_scratch(GRID_W, BK),
        compiler_params=_params(2),
        name="nbr",
    )(qkv_nat, qkv_nat, qkv_nat, bias)


def _nbr_bias(rpb):
    qc = jnp.arange(GRID_W)[:, None]
    kc = jnp.arange(GRID_W)[None, :]
    cidx = jnp.clip(kc - qc, -(NA_COLS - 1), NA_COLS - 1) + NA_COLS - 1
    cs = jnp.clip(qc - NA_COLS // 2, 0, GRID_W - NA_COLS)
    col_ok = (kc >= cs) & (kc < cs + NA_COLS)
    ridx = jnp.arange(NA_ROWS)[:, None] + jnp.arange(NA_ROWS)[None, :]
    t = rpb.astype(F32)[:, ridx][:, :, :, cidx]
    t = jnp.where(col_ok[None, None, None], t, NEG_INF)
    t = t.transpose(0, 1, 3, 2, 4)
    return t.reshape(rpb.shape[0], NA_ROWS, GRID_W, NA_ROWS * GRID_W)


def _mix_out_kernel(oa_ref, ob_ref, x_ref, ga_ref, gb_ref, g0_ref, b0_ref, w_ref, g1_ref, b1_ref,
                    o_ref):
    ya = _rms(oa_ref[0].astype(F32), ga_ref[...])
    yb = _rms(ob_ref[0].astype(F32), gb_ref[...])
    y = jnp.concatenate([ya, yb], axis=1).astype(BF16)
    z = jnp.dot(y, w_ref[...], preferred_element_type=F32)
    x0 = _ln(x_ref[0], g0_ref[...], b0_ref[...])
    o_ref[0] = _ln(ALPHA * x0 + z, g1_ref[...], b1_ref[...])


def _mix_out(oa, ob, x, g_a, g_b, ln_in_g, ln_in_b, w_out, ln1_g, ln1_b, tm):
    B, T, _ = x.shape
    vec = lambda n: _const_spec((1, n))
    return pl.pallas_call(
        _mix_out_kernel,
        grid=(B, T // tm),
        in_specs=[pl.BlockSpec((1, tm, WIDTH_A), lambda b, i: (b, i, 0)),
                  pl.BlockSpec((1, tm, WIDTH_B), lambda b, i: (b, i, 0)),
                  pl.BlockSpec((1, tm, D_MODEL), lambda b, i: (b, i, 0)),
                  vec(WIDTH_A), vec(WIDTH_B), vec(D_MODEL), vec(D_MODEL),
                  _const_spec((D_MODEL, D_MODEL)), vec(D_MODEL), vec(D_MODEL)],
        out_specs=pl.BlockSpec((1, tm, D_MODEL), lambda b, i: (b, i, 0)),
        out_shape=jax.ShapeDtypeStruct((B, T, D_MODEL), F32),
        compiler_params=_params(2),
        name="mix_out",
    )(oa, ob, x, g_a, g_b, ln_in_g, ln_in_b, w_out, ln1_g, ln1_b)


def _cross_kernel(x_ref, kv_ref, wq_ref, wo_ref, g_ref, b_ref, o_ref):
    x = x_ref[0]
    q = jnp.dot(x.astype(BF16), wq_ref[...], preferred_element_type=F32)
    q = (q * (HEAD_DIM_X ** -0.5)).astype(BF16)
    outs = []
    for h in range(N_HEADS_X):
        k = kv_ref[0, :, h * HEAD_DIM_X:(h + 1) * HEAD_DIM_X]
        v = kv_ref[0, :, D_MODEL + h * HEAD_DIM_X:D_MODEL + (h + 1) * HEAD_DIM_X]
        s = lax.dot_general(q[:, h * HEAD_DIM_X:(h + 1) * HEAD_DIM_X], k, NT_DIMS,
                            preferred_element_type=F32)
        p = jnp.exp(s - jnp.max(s, axis=1, keepdims=True))
        l = jnp.sum(p, axis=1, keepdims=True)
        outs.append(jnp.dot(p.astype(BF16), v, preferred_element_type=F32) / l)
    o = jnp.concatenate(outs, axis=1).astype(BF16)
    z = jnp.dot(o, wo_ref[...], preferred_element_type=F32)
    o_ref[0] = _ln(ALPHA * x + z, g_ref[...], b_ref[...])


def _cross(x, kv, w_xq, w_xo, ln_g, ln_b, tm):
    B, T, _ = x.shape
    M = kv.shape[1]
    return pl.pallas_call(
        _cross_kernel,
        grid=(B, T // tm),
        in_specs=[pl.BlockSpec((1, tm, D_MODEL), lambda b, i: (b, i, 0)),
                  pl.BlockSpec((1, M, 2 * D_MODEL), lambda b, i: (b, 0, 0)),
                  _const_spec((D_MODEL, D_MODEL)), _const_spec((D_MODEL, D_MODEL)),
                  _const_spec((1, D_MODEL)), _const_spec((1, D_MODEL))],
        out_specs=pl.BlockSpec((1, tm, D_MODEL), lambda b, i: (b, i, 0)),
        out_shape=jax.ShapeDtypeStruct((B, T, D_MODEL), F32),
        compiler_params=_params(2),
        name="cross",
    )(x, kv, w_xq, w_xo, ln_g, ln_b)


def _mlp_kernel(x_ref, wu_ref, wd_ref, g_ref, b_ref, o_ref):
    x = x_ref[0]
    h = jnp.dot(x.astype(BF16), wu_ref[...], preferred_element_type=F32)
    h = jnp.maximum(h, 0.0)
    h = (h * h).astype(BF16)
    z = jnp.dot(h, wd_ref[...], preferred_element_type=F32)
    o_ref[0] = _ln(ALPHA * x + z, g_ref[...], b_ref[...])


def _mlp(x, w_up, w_down, ln_g, ln_b, tm):
    B, T, _ = x.shape
    return pl.pallas_call(
        _mlp_kernel,
        grid=(B, T // tm),
        in_specs=[pl.BlockSpec((1, tm, D_MODEL), lambda b, i: (b, i, 0)),
                  _const_spec((D_MODEL, D_FF)), _const_spec((D_FF, D_MODEL)),
                  _const_spec((1, D_MODEL)), _const_spec((1, D_MODEL))],
        out_specs=pl.BlockSpec((1, tm, D_MODEL), lambda b, i: (b, i, 0)),
        out_shape=jax.ShapeDtypeStruct((B, T, D_MODEL), F32),
        compiler_params=_params(2),
        name="mlp",
    )(x, w_up, w_down, ln_g, ln_b)


def _rope_tables(T):
    half = HEAD_DIM // 2
    inv = ROPE_THETA ** (-jnp.arange(half, dtype=F32) / half)
    ang = jnp.arange(T, dtype=F32)[:, None] * inv[None, :]
    cos, sin = jnp.cos(ang), jnp.sin(ang)
    return (jnp.concatenate([cos, cos, cos, cos], axis=1),
            jnp.concatenate([-sin, sin, -sin, sin], axis=1))


def _trunk(x, mem, p, tm=256):
    B, T, _ = x.shape
    cos, sin = _rope_tables(T)
    kv = _mem_kv(mem, p["w_xkv"])
    qkv_nat, qkv_perm = _qkv(x, p["ln_in_g"], p["ln_in_b"], p["w_in"], cos, sin, tm)
    oa = _dilated(qkv_nat, qkv_perm.reshape(B, T, 3 * WIDTH_A))
    ob = _nbr(qkv_nat, p["bias"])
    x1 = _mix_out(oa, ob, x, p["g_mix_a"], p["g_mix_b"], p["ln_in_g"], p["ln_in_b"], p["w_out"],
                  p["ln1_g"], p["ln1_b"], tm)
    x2 = _cross(x1, kv, p["w_xq"], p["w_xo"], p["ln2_g"], p["ln2_b"], tm)
    return _mlp(x2, p["w_up"], p["w_down"], p["ln3_g"], p["ln3_b"], tm)


def kernel(x_prompt, x_sample, mem_prompt, mem_sample, ln_in_g, ln_in_b, w_in, rpb, g_mix_a, g_mix_b,
           w_out, ln1_g, ln1_b, w_xq, w_xkv, w_xo, ln2_g, ln2_b, w_up, w_down, ln3_g, ln3_b):
    assert w_in.shape[0] == 1, "single-layer trunk"
    row = lambda v: v.reshape(1, -1).astype(F32)
    p = dict(
        ln_in_g=row(ln_in_g), ln_in_b=row(ln_in_b),
        w_in=w_in[0].astype(BF16), bias=_nbr_bias(rpb[0]),
        g_mix_a=row(g_mix_a[0]), g_mix_b=row(g_mix_b[0]), w_out=w_out[0].astype(BF16),
        ln1_g=row(ln1_g[0]), ln1_b=row(ln1_b[0]),
        w_xq=w_xq[0].astype(BF16), w_xkv=w_xkv[0].astype(BF16), w_xo=w_xo[0].astype(BF16),
        ln2_g=row(ln2_g[0]), ln2_b=row(ln2_b[0]),
        w_up=w_up[0].astype(BF16), w_down=w_down[0].astype(BF16),
        ln3_g=row(ln3_g[0]), ln3_b=row(ln3_b[0]),
    )
    return _trunk(x_prompt, mem_prompt, p), _trunk(x_sample, mem_sample, p)
```

```python
import functools

import jax
import jax.numpy as jnp
from jax import lax
from jax.experimental import pallas as pl
from jax.experimental.pallas import tpu as pltpu

F32 = jnp.float32
BF16 = jnp.bfloat16

D_MODEL = 1024
HEAD_DIM = 64
WIDTH_A = 512
WIDTH_B = 512
N_PAIRS = WIDTH_A // 128
GRID_W = 64
NA_ROWS = 8
NA_COLS = 16
N_HEADS_X = 4
HEAD_DIM_X = 256
D_FF = 4096
ROPE_THETA = 10000.0
LN_EPS = 1e-5
ALPHA = 2.0 ** 0.25
NEG_INF = -1e30
N_SIDE = 64
MAX_DIL = 16

VMEM_LIMIT = 56 * 1024 * 1024

NT_DIMS = (((1,), (1,)), ((), ()))


def _ln(x, g, b):
    mu = jnp.mean(x, axis=-1, keepdims=True)
    xc = x - mu
    var = jnp.mean(xc * xc, axis=-1, keepdims=True)
    return xc * lax.rsqrt(var + LN_EPS) * g + b


def _rms(x, g):
    return x * lax.rsqrt(jnp.mean(x * x, axis=-1, keepdims=True) + LN_EPS) * g


def _params(n_axes):
    return pltpu.CompilerParams(dimension_semantics=("arbitrary",) * n_axes,
                                vmem_limit_bytes=VMEM_LIMIT)


def _const_spec(shape):
    nd = len(shape)
    return pl.BlockSpec(shape, lambda *_: (0,) * nd, pipeline_mode=pl.Buffered(1))


def _mem_kv_kernel(mem_ref, w_ref, o_ref):
    o_ref[0] = jnp.dot(mem_ref[0].astype(BF16), w_ref[...],
                       preferred_element_type=F32).astype(BF16)


def _mem_kv(mem, w_xkv):
    B, M, _ = mem.shape
    return pl.pallas_call(
        _mem_kv_kernel,
        grid=(B,),
        in_specs=[pl.BlockSpec((1, M, D_MODEL), lambda b: (b, 0, 0)),
                  _const_spec((D_MODEL, 2 * D_MODEL))],
        out_specs=pl.BlockSpec((1, M, 2 * D_MODEL), lambda b: (b, 0, 0)),
        out_shape=jax.ShapeDtypeStruct((B, M, 2 * D_MODEL), BF16),
        compiler_params=_params(1),
        name="mem_kv",
    )(mem, w_xkv)


def _qkv_kernel(x_ref, g_ref, b_ref, w_ref, cos_ref, sin_ref, nat_ref, perm_ref, scr_ref, *, tm):
    xb = _ln(x_ref[0], g_ref[...], b_ref[...]).astype(BF16)
    cos = jnp.concatenate([cos_ref[...]] * 4, axis=1)
    sin = jnp.concatenate([sin_ref[...]] * 4, axis=1)
    lane = lax.broadcasted_iota(jnp.int32, (tm, WIDTH_A), 1)
    first_half = (lane % HEAD_DIM) < HEAD_DIM // 2
    for part in range(6):
        r = jnp.dot(xb, w_ref[:, part * 512:(part + 1) * 512], preferred_element_type=F32)
        if part in (0, 1):
            swapped = jnp.where(first_half, pltpu.roll(r, WIDTH_A - 32, 1), pltpu.roll(r, 32, 1))
            r = r * cos + swapped * sin
        if part in (0, 3):
            r = r * (HEAD_DIM ** -0.5)
        nat_ref[0, :, part * 512:(part + 1) * 512] = r.astype(BF16)
        if part < 3:
            for s in range(4):
                scr_ref[part * 4 + s] = r[:, s * 128:(s + 1) * 128]
    for c in range(MAX_DIL):
        for slab in range(12):
            perm_ref[0, c, :, slab * 128:(slab + 1) * 128] = (
                scr_ref[slab, pl.ds(c, tm // MAX_DIL, stride=MAX_DIL), :].astype(BF16))


def _qkv(x, ln_g, ln_b, w_in, cos, sin, tm):
    B, T, _ = x.shape
    L16 = T // MAX_DIL
    return pl.pallas_call(
        functools.partial(_qkv_kernel, tm=tm),
        grid=(B, T // tm),
        in_specs=[pl.BlockSpec((1, tm, D_MODEL), lambda b, i: (b, i, 0)),
                  _const_spec((1, D_MODEL)), _const_spec((1, D_MODEL)),
                  _const_spec((D_MODEL, 3 * D_MODEL)),
                  pl.BlockSpec((tm, 128), lambda b, i: (i, 0)),
                  pl.BlockSpec((tm, 128), lambda b, i: (i, 0))],
        out_specs=[pl.BlockSpec((1, tm, 3 * D_MODEL), lambda b, i: (b, i, 0)),
                   pl.BlockSpec((1, MAX_DIL, tm // MAX_DIL, 3 * WIDTH_A), lambda b, i: (b, 0, i, 0))],
        out_shape=[jax.ShapeDtypeStruct((B, T, 3 * D_MODEL), BF16),
                   jax.ShapeDtypeStruct((B, MAX_DIL, L16, 3 * WIDTH_A), BF16)],
        scratch_shapes=[pltpu.VMEM((12, tm, 128), F32)],
        compiler_params=_params(2),
        name="qkv",
    )(x, ln_g, ln_b, w_in, cos, sin)


GROUP = 4


def _attn_pipeline(segments, lo, s_scr, p_scr, m_scr):
    bq, bk = s_scr.shape[-2:]

    def stage_a(seg, g, slot):
        for t in range(GROUP):
            q2, k2, b_lo, b_hi = seg[1](g, t)
            zero = jnp.zeros_like(q2)
            for h, (q1, bias) in enumerate(((jnp.where(lo, q2, zero), b_lo),
                                            (jnp.where(lo, zero, q2), b_hi))):
                s = lax.dot_general(q1, k2, NT_DIMS, preferred_element_type=F32) + bias
                s_scr[slot, t, h] = s
                m_scr[slot, t, h] = jnp.broadcast_to(jnp.max(s, axis=1, keepdims=True), (bq, 128))

    def stage_b(slot):
        for t in range(GROUP):
            for h in range(2):
                m = jnp.concatenate([m_scr[slot, t, h]] * (bk // 128), axis=1)
                p_scr[slot, t, h] = jnp.exp(s_scr[slot, t, h] - m).astype(BF16)

    def stage_c(seg, g, slot):
        tiles = []
        for t in range(GROUP):
            v2 = seg[2](g, t)
            v_ones = jnp.concatenate([v2, jnp.ones_like(v2)], axis=1)
            pv0 = jnp.dot(p_scr[slot, t, 0], v_ones, preferred_element_type=F32)
            pv1 = jnp.dot(p_scr[slot, t, 1], v_ones, preferred_element_type=F32)
            tiles.append((jnp.where(lo, m_scr[slot, t, 0], m_scr[slot, t, 1]),
                          jnp.where(lo, pv0[:, 128:], pv1[:, 128:]),
                          jnp.where(lo, pv0[:, :128], pv1[:, :128])))
        seg[3](g, tiles)

    stage_a(segments[0], 0, 0)
    stage_b(0)
    stage_a(segments[0], 1, 1)
    for k, seg in enumerate(segments):
        n = seg[0]
        assert n >= 2 and n % 2 == 0

        def body(j, carry, seg=seg):
            for slot in range(2):
                stage_c(seg, 2 * j + slot, slot)
                stage_b(1 - slot)
                stage_a(seg, 2 * j + slot + 2, slot)
            return carry

        lax.fori_loop(0, (n - 2) // 2, body, 0)
        nxt = segments[k + 1] if k + 1 < len(segments) else None
        stage_c(seg, n - 2, 0)
        stage_b(1)
        if nxt:
            stage_a(nxt, 0, 0)
        stage_c(seg, n - 1, 1)
        if nxt:
            stage_b(0)
            stage_a(nxt, 1, 1)


def _pipeline_scratch(bq, bk):
    return [pltpu.VMEM((2, GROUP, 2, bq, bk), F32), pltpu.VMEM((2, GROUP, 2, bq, bk), BF16),
            pltpu.VMEM((2, GROUP, 2, bq, 128), F32)]


def _merge(m_r, l_r, a_r, m_t, l_t, a_t):
    m_n = jnp.maximum(m_r, m_t)
    e_r = jnp.exp(m_r - m_n)
    e_t = jnp.exp(m_t - m_n)
    return m_n, l_r * e_r + l_t * e_t, a_r * e_r + a_t * e_t


def _dilated_kernel(qn_ref, kn_ref, vn_ref, qp_ref, kp_ref, vp_ref, o_ref,
                    m_ref, l_ref, a_ref, tmp_ref, mask_ref, s_scr, p_scr, m_scr, *, T):
    BQ, BK = 128, 256
    L16 = T // MAX_DIL
    row = lax.broadcasted_iota(jnp.int32, (BQ, BK), 0)
    col = lax.broadcasted_iota(jnp.int32, (BQ, BK), 1)
    band = row - col
    band4 = 4 * ((row % 32) - (col % 64)) + (row // 32 - col // 64)
    for n in range(3):
        mask_ref[n] = jnp.where(jnp.abs(band + N_SIDE * n) <= N_SIDE, 0.0, NEG_INF)
        mask_ref[3 + n] = jnp.where(jnp.abs(band4 + N_SIDE * n) <= N_SIDE, 0.0, NEG_INF)
    lo = lax.broadcasted_iota(jnp.int32, (BQ, 128), 1) < HEAD_DIM
    n_groups = T // (BQ * GROUP)

    nib = L16 // BQ

    def rows16(g, t):
        tile = jnp.asarray(g * GROUP + t, jnp.int32)
        c = lax.div(tile, nib)
        ib = lax.rem(tile, nib)
        kst = jnp.clip(ib * BQ - N_SIDE, 0, L16 - BK)
        return (pl.multiple_of(c * L16 + ib * BQ, BQ), pl.multiple_of(c * L16 + kst, N_SIDE),
                lax.div(ib * BQ - kst, N_SIDE))

    def load_qk16(g, t):
        q0, k0, case = rows16(g, t)
        mask = mask_ref[case]
        return qp_ref[0, pl.ds(q0, BQ), :], kp_ref[0, pl.ds(k0, BK), :], mask, mask

    def load_v16(g, t):
        return vp_ref[0, pl.ds(rows16(g, t)[1], BK), :]

    def finalize16(g, tiles):
        for t, (m, l, a) in enumerate(tiles):
            q0 = rows16(g, t)[0]
            m_ref[pl.ds(q0, BQ), :] = m
            l_ref[pl.ds(q0, BQ), :] = l
            a_ref[pl.ds(q0, BQ), :] = a

    nb4 = L16 // 32

    def rows4(g, t):
        tile = jnp.asarray(g * GROUP + t, jnp.int32)
        r4 = lax.div(tile, nb4)
        i0 = lax.rem(tile, nb4) * 32
        kst = jnp.clip(i0 - 16, 0, L16 - 64)
        return ([pl.multiple_of((r4 + 4 * w) * L16 + i0, 32) for w in range(4)],
                [pl.multiple_of((r4 + 4 * w) * L16 + kst, 16) for w in range(4)],
                3 + lax.div(i0 - kst, 16))

    def load_qk4(g, t):
        qrows, krows, case = rows4(g, t)
        mask = mask_ref[case]
        return (jnp.concatenate([qp_ref[0, pl.ds(r, 32), :] for r in qrows], axis=0),
                jnp.concatenate([kp_ref[0, pl.ds(r, 64), :] for r in krows], axis=0), mask, mask)

    def load_v4(g, t):
        return jnp.concatenate([vp_ref[0, pl.ds(r, 64), :] for r in rows4(g, t)[1]], axis=0)

    def finalize4(g, tiles):
        merged = []
        for t, tile in enumerate(tiles):
            qrows = rows4(g, t)[0]
            state = [jnp.concatenate([ref[pl.ds(r, 32), :] for r in qrows], axis=0)
                     for ref in (m_ref, l_ref, a_ref)]
            merged.append((qrows, _merge(*state, *tile)))
        for qrows, new in merged:
            for ref, val in zip((m_ref, l_ref, a_ref), new):
                for w, r in enumerate(qrows):
                    ref[pl.ds(r, 32), :] = val[w * 32:(w + 1) * 32]

    def rows1(g, t):
        t0 = pl.multiple_of(jnp.asarray(g * GROUP + t, jnp.int32) * BQ, BQ)
        kst = pl.multiple_of(jnp.clip(t0 - N_SIDE, 0, T - BK), N_SIDE)
        return t0, kst, lax.div(t0 - kst, N_SIDE)

    def load_qk1(g, t):
        t0, kst, case = rows1(g, t)
        mask = mask_ref[case]
        return qn_ref[0, pl.ds(t0, BQ), :], kn_ref[0, pl.ds(kst, BK), :], mask, mask

    def load_v1(g, t):
        return vn_ref[0, pl.ds(rows1(g, t)[1], BK), :]

    def finalize1(g, tiles):
        n_i = BQ // MAX_DIL
        for t, tile in enumerate(tiles):
            t0 = rows1(g, t)[0]
            i0 = lax.div(t0, MAX_DIL)
            for c in range(MAX_DIL):
                src = pl.ds(pl.multiple_of(c * L16 + i0, n_i), n_i)
                dst = pl.ds(c, n_i, stride=MAX_DIL)
                for n, ref in enumerate((m_ref, l_ref, a_ref)):
                    tmp_ref[t, n, dst, :] = ref[src, :]
            _, l_n, a_n = _merge(tmp_ref[t, 0], tmp_ref[t, 1], tmp_ref[t, 2], *tile)
            o_ref[0, pl.ds(t0, BQ), :] = (a_n / l_n).astype(BF16)

    _attn_pipeline([(n_groups, load_qk16, load_v16, finalize16),
                    (n_groups, load_qk4, load_v4, finalize4),
                    (n_groups, load_qk1, load_v1, finalize1)], lo, s_scr, p_scr, m_scr)


def _dilated(qkv_nat, qkv_perm):
    B, T, _ = qkv_nat.shape
    assert T % (MAX_DIL * 128) == 0 and T % (128 * GROUP) == 0 and T // MAX_DIL >= 256
    seq = lambda col0: pl.BlockSpec((1, T, 128), lambda b, j: (b, 0, col0 + j))
    return pl.pallas_call(
        functools.partial(_dilated_kernel, T=T),
        grid=(B, N_PAIRS),
        in_specs=[seq(0), seq(N_PAIRS), seq(2 * N_PAIRS), seq(0), seq(N_PAIRS), seq(2 * N_PAIRS)],
        out_specs=pl.BlockSpec((1, T, 128), lambda b, j: (b, 0, j)),
        out_shape=jax.ShapeDtypeStruct((B, T, WIDTH_A), BF16),
        scratch_shapes=[pltpu.VMEM((T, 128), F32), pltpu.VMEM((T, 128), F32),
                        pltpu.VMEM((T, 128), F32), pltpu.VMEM((GROUP, 3, 128, 128), F32),
                        pltpu.VMEM((6, 128, 256), F32)] + _pipeline_scratch(128, 256),
        compiler_params=_params(2),
        name="dilated",
    )(qkv_nat, qkv_nat, qkv_nat, qkv_perm, qkv_perm, qkv_perm)


def _nbr_kernel(q_ref, k_ref, v_ref, bias_ref, o_ref, s_scr, p_scr, m_scr, *, T):
    rows = T // GRID_W
    BK = NA_ROWS * GRID_W
    lo = lax.broadcasted_iota(jnp.int32, (GRID_W, 128), 1) < HEAD_DIM

    def where(g, t):
        r = jnp.asarray(g * GROUP + t, jnp.int32)
        rs = jnp.clip(r - NA_ROWS // 2, 0, rows - NA_ROWS)
        return (pl.multiple_of(r * GRID_W, GRID_W), pl.multiple_of(rs * GRID_W, GRID_W),
                rs - r + (NA_ROWS - 1))

    def load_qk(g, t):
        q0, k0, var = where(g, t)
        return (q_ref[0, pl.ds(q0, GRID_W), :], k_ref[0, pl.ds(k0, BK), :],
                bias_ref[0, var], bias_ref[1, var])

    def load_v(g, t):
        return v_ref[0, pl.ds(where(g, t)[1], BK), :]

    def finalize(g, tiles):
        for t, (_, l, a) in enumerate(tiles):
            o_ref[0, pl.ds(where(g, t)[0], GRID_W), :] = (a / l).astype(BF16)

    _attn_pipeline([(rows // GROUP, load_qk, load_v, finalize)], lo, s_scr, p_scr, m_scr)


def _nbr(qkv_nat, bias):
    B, T, _ = qkv_nat.shape
    assert T % (GRID_W * GROUP) == 0 and T // GRID_W >= NA_ROWS
    BK = NA_ROWS * GRID_W
    seq = lambda col0: pl.BlockSpec((1, T, 128), lambda b, j: (b, 0, col0 + j))
    return pl.pallas_call(
        functools.partial(_nbr_kernel, T=T),
        grid=(B, N_PAIRS),
        in_specs=[seq(3 * N_PAIRS), seq(4 * N_PAIRS), seq(5 * N_PAIRS),
                  pl.BlockSpec((2, NA_ROWS, GRID_W, BK), lambda b, j: (j, 0, 0, 0))],
        out_specs=pl.BlockSpec((1, T, 128), lambda b, j: (b, 0, j)),
        out_shape=jax.ShapeDtypeStruct((B, T, WIDTH_B), BF16),
        scratch_shapes=_pipeline_scratch(GRID_W, BK),
        compiler_params=_params(2),
        name="nbr",
    )(qkv_nat, qkv_nat, qkv_nat, bias)


def _nbr_bias(rpb):
    qc = jnp.arange(GRID_W)[:, None]
    kc = jnp.arange(GRID_W)[None, :]
    cidx = jnp.clip(kc - qc, -(NA_COLS - 1), NA_COLS - 1) + NA_COLS - 1
    cs = jnp.clip(qc - NA_COLS // 2, 0, GRID_W - NA_COLS)
    col_ok = (kc >= cs) & (kc < cs + NA_COLS)
    ridx = jnp.arange(NA_ROWS)[:, None] + jnp.arange(NA_ROWS)[None, :]
    t = rpb.astype(F32)[:, ridx][:, :, :, cidx]
    t = jnp.where(col_ok[None, None, None], t, NEG_INF)
    t = t.transpose(0, 1, 3, 2, 4)
    return t.reshape(rpb.shape[0], NA_ROWS, GRID_W, NA_ROWS * GRID_W)


def _mix_out_kernel(oa_ref, ob_ref, x_ref, ga_ref, gb_ref, g0_ref, b0_ref, w_ref, g1_ref, b1_ref,
                    o_ref):
    ya = _rms(oa_ref[0].astype(F32), ga_ref[...])
    yb = _rms(ob_ref[0].astype(F32), gb_ref[...])
    y = jnp.concatenate([ya, yb], axis=1).astype(BF16)
    z = jnp.dot(y, w_ref[...], preferred_element_type=F32)
    x0 = _ln(x_ref[0], g0_ref[...], b0_ref[...])
    o_ref[0] = _ln(ALPHA * x0 + z, g1_ref[...], b1_ref[...])


def _mix_out(oa, ob, x, g_a, g_b, ln_in_g, ln_in_b, w_out, ln1_g, ln1_b, tm):
    B, T, _ = x.shape
    vec = lambda n: _const_spec((1, n))
    return pl.pallas_call(
        _mix_out_kernel,
        grid=(B, T // tm),
        in_specs=[pl.BlockSpec((1, tm, WIDTH_A), lambda b, i: (b, i, 0)),
                  pl.BlockSpec((1, tm, WIDTH_B), lambda b, i: (b, i, 0)),
                  pl.BlockSpec((1, tm, D_MODEL), lambda b, i: (b, i, 0)),
                  vec(WIDTH_A), vec(WIDTH_B), vec(D_MODEL), vec(D_MODEL),
                  _const_spec((D_MODEL, D_MODEL)), vec(D_MODEL), vec(D_MODEL)],
        out_specs=pl.BlockSpec((1, tm, D_MODEL), lambda b, i: (b, i, 0)),
        out_shape=jax.ShapeDtypeStruct((B, T, D_MODEL), F32),
        compiler_params=_params(2),
        name="mix_out",
    )(oa, ob, x, g_a, g_b, ln_in_g, ln_in_b, w_out, ln1_g, ln1_b)


def _cross_kernel(x_ref, kv_ref, wq_ref, wo_ref, g_ref, b_ref, o_ref):
    x = x_ref[0]
    q = jnp.dot(x.astype(BF16), wq_ref[...], preferred_element_type=F32)
    q = (q * (HEAD_DIM_X ** -0.5)).astype(BF16)
    outs = []
    for h in range(N_HEADS_X):
        k = kv_ref[0, :, h * HEAD_DIM_X:(h + 1) * HEAD_DIM_X]
        v = kv_ref[0, :, D_MODEL + h * HEAD_DIM_X:D_MODEL + (h + 1) * HEAD_DIM_X]
        s = lax.dot_general(q[:, h * HEAD_DIM_X:(h + 1) * HEAD_DIM_X], k, NT_DIMS,
                            preferred_element_type=F32)
        p = jnp.exp(s - jnp.max(s, axis=1, keepdims=True))
        l = jnp.sum(p, axis=1, keepdims=True)
        outs.append(jnp.dot(p.astype(BF16), v, preferred_element_type=F32) / l)
    o = jnp.concatenate(outs, axis=1).astype(BF16)
    z = jnp.dot(o, wo_ref[...], preferred_element_type=F32)
    o_ref[0] = _ln(ALPHA * x + z, g_ref[...], b_ref[...])


def _cross(x, kv, w_xq, w_xo, ln_g, ln_b, tm):
    B, T, _ = x.shape
    M = kv.shape[1]
    return pl.pallas_call(
        _cross_kernel,
        grid=(B, T // tm),
        in_specs=[pl.BlockSpec((1, tm, D_MODEL), lambda b, i: (b, i, 0)),
                  pl.BlockSpec((1, M, 2 * D_MODEL), lambda b, i: (b, 0, 0)),
                  _const_spec((D_MODEL, D_MODEL)), _const_spec((D_MODEL, D_MODEL)),
                  _const_spec((1, D_MODEL)), _const_spec((1, D_MODEL))],
        out_specs=pl.BlockSpec((1, tm, D_MODEL), lambda b, i: (b, i, 0)),
        out_shape=jax.ShapeDtypeStruct((B, T, D_MODEL), F32),
        compiler_params=_params(2),
        name="cross",
    )(x, kv, w_xq, w_xo, ln_g, ln_b)


def _mlp_kernel(x_ref, wu_ref, wd_ref, g_ref, b_ref, o_ref):
    x = x_ref[0]
    h = jnp.dot(x.astype(BF16), wu_ref[...], preferred_element_type=F32)
    h = jnp.maximum(h, 0.0)
    h = (h * h).astype(BF16)
    z = jnp.dot(h, wd_ref[...], preferred_element_type=F32)
    o_ref[0] = _ln(ALPHA * x + z, g_ref[...], b_ref[...])


def _mlp(x, w_up, w_down, ln_g, ln_b, tm):
    B, T, _ = x.shape
    return pl.pallas_call(
        _mlp_kernel,
        grid=(B, T // tm),
        in_specs=[pl.BlockSpec((1, tm, D_MODEL), lambda b, i: (b, i, 0)),
                  _const_spec((D_MODEL, D_FF)), _const_spec((D_FF, D_MODEL)),
                  _const_spec((1, D_MODEL)), _const_spec((1, D_MODEL))],
        out_specs=pl.BlockSpec((1, tm, D_MODEL), lambda b, i: (b, i, 0)),
        out_shape=jax.ShapeDtypeStruct((B, T, D_MODEL), F32),
        compiler_params=_params(2),
        name="mlp",
    )(x, w_up, w_down, ln_g, ln_b)


def _rope_tables(T):
    half = HEAD_DIM // 2
    inv = ROPE_THETA ** (-jnp.arange(half, dtype=F32) / half)
    ang = jnp.arange(T, dtype=F32)[:, None] * inv[None, :]
    cos, sin = jnp.cos(ang), jnp.sin(ang)
    return (jnp.concatenate([cos, cos, cos, cos], axis=1),
            jnp.concatenate([-sin, sin, -sin, sin], axis=1))


def _trunk(x, mem, p, tm=256):
    B, T, _ = x.shape
    cos, sin = _rope_tables(T)
    kv = _mem_kv(mem, p["w_xkv"])
    qkv_nat, qkv_perm = _qkv(x, p["ln_in_g"], p["ln_in_b"], p["w_in"], cos, sin, tm)
    oa = _dilated(qkv_nat, qkv_perm.reshape(B, T, 3 * WIDTH_A))
    ob = _nbr(qkv_nat, p["bias"])
    x1 = _mix_out(oa, ob, x, p["g_mix_a"], p["g_mix_b"], p["ln_in_g"], p["ln_in_b"], p["w_out"],
                  p["ln1_g"], p["ln1_b"], tm)
    x2 = _cross(x1, kv, p["w_xq"], p["w_xo"], p["ln2_g"], p["ln2_b"], tm)
    return _mlp(x2, p["w_up"], p["w_down"], p["ln3_g"], p["ln3_b"], tm)


def kernel(x_prompt, x_sample, mem_prompt, mem_sample, ln_in_g, ln_in_b, w_in, rpb, g_mix_a, g_mix_b,
           w_out, ln1_g, ln1_b, w_xq, w_xkv, w_xo, ln2_g, ln2_b, w_up, w_down, ln3_g, ln3_b):
    assert w_in.shape[0] == 1, "single-layer trunk"
    row = lambda v: v.reshape(1, -1).astype(F32)
    p = dict(
        ln_in_g=row(ln_in_g), ln_in_b=row(ln_in_b),
        w_in=w_in[0].astype(BF16), bias=_nbr_bias(rpb[0]),
        g_mix_a=row(g_mix_a[0]), g_mix_b=row(g_mix_b[0]), w_out=w_out[0].astype(BF16),
        ln1_g=row(ln1_g[0]), ln1_b=row(ln1_b[0]),
        w_xq=w_xq[0].astype(BF16), w_xkv=w_xkv[0].astype(BF16), w_xo=w_xo[0].astype(BF16),
        ln2_g=row(ln2_g[0]), ln2_b=row(ln2_b[0]),
        w_up=w_up[0].astype(BF16), w_down=w_down[0].astype(BF16),
        ln3_g=row(ln3_g[0]), ln3_b=row(ln3_b[0]),
    )
    return _trunk(x_prompt, mem_prompt, p), _trunk(x_sample, mem_sample, p)
```

```python
import functools

import jax
import jax.numpy as jnp
from jax import lax
from jax.experimental import pallas as pl
from jax.experimental.pallas import tpu as pltpu

F32 = jnp.float32
BF16 = jnp.bfloat16

D_MODEL = 1024
HEAD_DIM = 64
WIDTH_A = 512
WIDTH_B = 512
N_PAIRS = WIDTH_A // 128
GRID_W = 64
NA_ROWS = 8
NA_COLS = 16
N_HEADS_X = 4
HEAD_DIM_X = 256
D_FF = 4096
ROPE_THETA = 10000.0
LN_EPS = 1e-5
ALPHA = 2.0 ** 0.25
NEG_INF = -1e30
N_SIDE = 64
MAX_DIL = 16

VMEM_LIMIT = 56 * 1024 * 1024

NT_DIMS = (((1,), (1,)), ((), ()))


def _ln(x, g, b):
    mu = jnp.mean(x, axis=-1, keepdims=True)
    xc = x - mu
    var = jnp.mean(xc * xc, axis=-1, keepdims=True)
    return xc * lax.rsqrt(var + LN_EPS) * g + b


def _rms(x, g):
    return x * lax.rsqrt(jnp.mean(x * x, axis=-1, keepdims=True) + LN_EPS) * g


def _params(n_axes):
    return pltpu.CompilerParams(dimension_semantics=("arbitrary",) * n_axes,
                                vmem_limit_bytes=VMEM_LIMIT)


def _const_spec(shape):
    nd = len(shape)
    return pl.BlockSpec(shape, lambda *_: (0,) * nd, pipeline_mode=pl.Buffered(1))


def _mem_kv_kernel(mem_ref, w_ref, o_ref):
    o_ref[0] = jnp.dot(mem_ref[0].astype(BF16), w_ref[...],
                       preferred_element_type=F32).astype(BF16)


def _mem_kv(mem, w_xkv):
    B, M, _ = mem.shape
    return pl.pallas_call(
        _mem_kv_kernel,
        grid=(B,),
        in_specs=[pl.BlockSpec((1, M, D_MODEL), lambda b: (b, 0, 0)),
                  _const_spec((D_MODEL, 2 * D_MODEL))],
        out_specs=pl.BlockSpec((1, M, 2 * D_MODEL), lambda b: (b, 0, 0)),
        out_shape=jax.ShapeDtypeStruct((B, M, 2 * D_MODEL), BF16),
        compiler_params=_params(1),
        name="mem_kv",
    )(mem, w_xkv)


def _qkv_kernel(x_ref, g_ref, b_ref, w_ref, cos_ref, sin_ref, nat_ref, perm_ref, scr_ref, *, tm):
    xb = _ln(x_ref[0], g_ref[...], b_ref[...]).astype(BF16)
    cos = jnp.concatenate([cos_ref[...]] * 4, axis=1)
    sin = jnp.concatenate([sin_ref[...]] * 4, axis=1)
    lane = lax.broadcasted_iota(jnp.int32, (tm, WIDTH_A), 1)
    first_half = (lane % HEAD_DIM) < HEAD_DIM // 2
    for part in range(6):
        r = jnp.dot(xb, w_ref[:, part * 512:(part + 1) * 512], preferred_element_type=F32)
        if part in (0, 1):
            swapped = jnp.where(first_half, pltpu.roll(r, WIDTH_A - 32, 1), pltpu.roll(r, 32, 1))
            r = r * cos + swapped * sin
        if part in (0, 3):
            r = r * (HEAD_DIM ** -0.5)
        nat_ref[0, :, part * 512:(part + 1) * 512] = r.astype(BF16)
        if part < 3:
            for s in range(4):
                scr_ref[part * 4 + s] = r[:, s * 128:(s + 1) * 128]
    for c in range(MAX_DIL):
        for slab in range(12):
            perm_ref[0, c, :, slab * 128:(slab + 1) * 128] = (
                scr_ref[slab, pl.ds(c, tm // MAX_DIL, stride=MAX_DIL), :].astype(BF16))


def _qkv(x, ln_g, ln_b, w_in, cos, sin, tm):
    B, T, _ = x.shape
    L16 = T // MAX_DIL
    return pl.pallas_call(
        functools.partial(_qkv_kernel, tm=tm),
        grid=(B, T // tm),
        in_specs=[pl.BlockSpec((1, tm, D_MODEL), lambda b, i: (b, i, 0)),
                  _const_spec((1, D_MODEL)), _const_spec((1, D_MODEL)),
                  _const_spec((D_MODEL, 3 * D_MODEL)),
                  pl.BlockSpec((tm, 128), lambda b, i: (i, 0)),
                  pl.BlockSpec((tm, 128), lambda b, i: (i, 0))],
        out_specs=[pl.BlockSpec((1, tm, 3 * D_MODEL), lambda b, i: (b, i, 0)),
                   pl.BlockSpec((1, MAX_DIL, tm // MAX_DIL, 3 * WIDTH_A), lambda b, i: (b, 0, i, 0))],
        out_shape=[jax.ShapeDtypeStruct((B, T, 3 * D_MODEL), BF16),
                   jax.ShapeDtypeStruct((B, MAX_DIL, L16, 3 * WIDTH_A), BF16)],
        scratch_shapes=[pltpu.VMEM((12, tm, 128), F32)],
        compiler_params=_params(2),
        name="qkv",
    )(x, ln_g, ln_b, w_in, cos, sin)


GROUP = 4


def _attn_pipeline(segments, lo, s_scr, p_scr, m_scr):
    bq, bk = s_scr.shape[-2:]

    def stage_a(seg, g, slot):
        for t in range(GROUP):
            q2, k2, b_lo, b_hi = seg[1](g, t)
            zero = jnp.zeros_like(q2)
            for h, (q1, bias) in enumerate(((jnp.where(lo, q2, zero), b_lo),
                                            (jnp.where(lo, zero, q2), b_hi))):
                s = lax.dot_general(q1, k2, NT_DIMS, preferred_element_type=F32) + bias
                s_scr[slot, t, h] = s
                m_scr[slot, t, h] = jnp.broadcast_to(jnp.max(s, axis=1, keepdims=True), (bq, 128))

    def stage_b(slot):
        for t in range(GROUP):
            for h in range(2):
                m = jnp.concatenate([m_scr[slot, t, h]] * (bk // 128), axis=1)
                p_scr[slot, t, h] = jnp.exp(s_scr[slot, t, h] - m).astype(BF16)

    def stage_c(seg, g, slot):
        tiles = []
        for t in range(GROUP):
            v2 = seg[2](g, t)
            v_ones = jnp.concatenate([v2, jnp.ones_like(v2)], axis=1)
            pv0 = jnp.dot(p_scr[slot, t, 0], v_ones, preferred_element_type=F32)
            pv1 = jnp.dot(p_scr[slot, t, 1], v_ones, preferred_element_type=F32)
            tiles.append((jnp.where(lo, m_scr[slot, t, 0], m_scr[slot, t, 1]),
                          jnp.where(lo, pv0[:, 128:], pv1[:, 128:]),
                          jnp.where(lo, pv0[:, :128], pv1[:, :128])))
        seg[3](g, tiles)

    stage_a(segments[0], 0, 0)
    stage_b(0)
    stage_a(segments[0], 1, 1)
    for k, seg in enumerate(segments):
        n = seg[0]
        assert n >= 2 and n % 2 == 0

        def body(j, carry, seg=seg):
            for slot in range(2):
                stage_c(seg, 2 * j + slot, slot)
                stage_b(1 - slot)
                stage_a(seg, 2 * j + slot + 2, slot)
            return carry

        lax.fori_loop(0, (n - 2) // 2, body, 0)
        nxt = segments[k + 1] if k + 1 < len(segments) else None
        stage_c(seg, n - 2, 0)
        stage_b(1)
        if nxt:
            stage_a(nxt, 0, 0)
        stage_c(seg, n - 1, 1)
        if nxt:
            stage_b(0)
            stage_a(nxt, 1, 1)


def _pipeline_scratch(bq, bk):
    return [pltpu.VMEM((2, GROUP, 2, bq, bk), F32), pltpu.VMEM((2, GROUP, 2, bq, bk), BF16),
            pltpu.VMEM((2, GROUP, 2, bq, 128), F32)]


def _merge(m_r, l_r, a_r, m_t, l_t, a_t):
    m_n = jnp.maximum(m_r, m_t)
    e_r = jnp.exp(m_r - m_n)
    e_t = jnp.exp(m_t - m_n)
    return m_n, l_r * e_r + l_t * e_t, a_r * e_r + a_t * e_t


def _dilated_kernel(qn_ref, kn_ref, vn_ref, qp_ref, kp_ref, vp_ref, o_ref,
                    m_ref, l_ref, a_ref, tmp_ref, mask_ref, s_scr, p_scr, m_scr, *, T):
    BQ, BK = 128, 256
    L16 = T // MAX_DIL
    row = lax.broadcasted_iota(jnp.int32, (BQ, BK), 0)
    col = lax.broadcasted_iota(jnp.int32, (BQ, BK), 1)
    band = row - col
    band4 = 4 * ((row % 32) - (col % 64)) + (row // 32 - col // 64)
    for n in range(3):
        mask_ref[n] = jnp.where(jnp.abs(band + N_SIDE * n) <= N_SIDE, 0.0, NEG_INF)
        mask_ref[3 + n] = jnp.where(jnp.abs(band4 + N_SIDE * n) <= N_SIDE, 0.0, NEG_INF)
    lo = lax.broadcasted_iota(jnp.int32, (BQ, 128), 1) < HEAD_DIM
    n_groups = T // (BQ * GROUP)

    nib = L16 // BQ

    def rows16(g, t):
        tile = jnp.asarray(g * GROUP + t, jnp.int32)
        c = lax.div(tile, nib)
        ib = lax.rem(tile, nib)
        kst = jnp.clip(ib * BQ - N_SIDE, 0, L16 - BK)
        return (pl.multiple_of(c * L16 + ib * BQ, BQ), pl.multiple_of(c * L16 + kst, N_SIDE),
                lax.div(ib * BQ - kst, N_SIDE))

    def load_qk16(g, t):
        q0, k0, case = rows16(g, t)
        mask = mask_ref[case]
        return qp_ref[0, pl.ds(q0, BQ), :], kp_ref[0, pl.ds(k0, BK), :], mask, mask

    def load_v16(g, t):
        return vp_ref[0, pl.ds(rows16(g, t)[1], BK), :]

    def finalize16(g, tiles):
        for t, (m, l, a) in enumerate(tiles):
            q0 = rows16(g, t)[0]
            m_ref[pl.ds(q0, BQ), :] = m
            l_ref[pl.ds(q0, BQ), :] = l
            a_ref[pl.ds(q0, BQ), :] = a

    nb4 = L16 // 32

    def rows4(g, t):
        tile = jnp.asarray(g * GROUP + t, jnp.int32)
        r4 = lax.div(tile, nb4)
        i0 = lax.rem(tile, nb4) * 32
        kst = jnp.clip(i0 - 16, 0, L16 - 64)
        return ([pl.multiple_of((r4 + 4 * w) * L16 + i0, 32) for w in range(4)],
                [pl.multiple_of((r4 + 4 * w) * L16 + kst, 16) for w in range(4)],
                3 + lax.div(i0 - kst, 16))

    def load_qk4(g, t):
        qrows, krows, case = rows4(g, t)
        mask = mask_ref[case]
        return (jnp.concatenate([qp_ref[0, pl.ds(r, 32), :] for r in qrows], axis=0),
                jnp.concatenate([kp_ref[0, pl.ds(r, 64), :] for r in krows], axis=0), mask, mask)

    def load_v4(g, t):
        return jnp.concatenate([vp_ref[0, pl.ds(r, 64), :] for r in rows4(g, t)[1]], axis=0)

    def finalize4(g, tiles):
        merged = []
        for t, tile in enumerate(tiles):
            qrows = rows4(g, t)[0]
            state = [jnp.concatenate([ref[pl.ds(r, 32), :] for r in qrows], axis=0)
                     for ref in (m_ref, l_ref, a_ref)]
            merged.append((qrows, _merge(*state, *tile)))
        for qrows, new in merged:
            for ref, val in zip((m_ref, l_ref, a_ref), new):
                for w, r in enumerate(qrows):
                    ref[pl.ds(r, 32), :] = val[w * 32:(w + 1) * 32]

    def rows1(g, t):
        t0 = pl.multiple_of(jnp.asarray(g * GROUP + t, jnp.int32) * BQ, BQ)
        kst = pl.multiple_of(jnp.clip(t0 - N_SIDE, 0, T - BK), N_SIDE)
        return t0, kst, lax.div(t0 - kst, N_SIDE)

    def load_qk1(g, t):
        t0, kst, case = rows1(g, t)
        mask = mask_ref[case]
        return qn_ref[0, pl.ds(t0, BQ), :], kn_ref[0, pl.ds(kst, BK), :], mask, mask

    def load_v1(g, t):
        return vn_ref[0, pl.ds(rows1(g, t)[1], BK), :]

    def finalize1(g, tiles):
        n_i = BQ // MAX_DIL
        for t, tile in enumerate(tiles):
            t0 = rows1(g, t)[0]
            i0 = lax.div(t0, MAX_DIL)
            for c in range(MAX_DIL):
                src = pl.ds(pl.multiple_of(c * L16 + i0, n_i), n_i)
                dst = pl.ds(c, n_i, stride=MAX_DIL)
                for n, ref in enumerate((m_ref, l_ref, a_ref)):
                    tmp_ref[t, n, dst, :] = ref[src, :]
            _, l_n, a_n = _merge(tmp_ref[t, 0], tmp_ref[t, 1], tmp_ref[t, 2], *tile)
            o_ref[0, pl.ds(t0, BQ), :] = (a_n / l_n).astype(BF16)

    _attn_pipeline([(n_groups, load_qk16, load_v16, finalize16),
                    (n_groups, load_qk4, load_v4, finalize4),
                    (n_groups, load_qk1, load_v1, finalize1)], lo, s_scr, p_scr, m_scr)


def _dilated(qkv_nat, qkv_perm):
    B, T, _ = qkv_nat.shape
    assert T % (MAX_DIL * 128) == 0 and T % (128 * GROUP) == 0 and T // MAX_DIL >= 256
    seq = lambda col0: pl.BlockSpec((1, T, 128), lambda b, j: (b, 0, col0 + j))
    return pl.pallas_call(
        functools.partial(_dilated_kernel, T=T),
        grid=(B, N_PAIRS),
        in_specs=[seq(0), seq(N_PAIRS), seq(2 * N_PAIRS), seq(0), seq(N_PAIRS), seq(2 * N_PAIRS)],
        out_specs=pl.BlockSpec((1, T, 128), lambda b, j: (b, 0, j)),
        out_shape=jax.ShapeDtypeStruct((B, T, WIDTH_A), BF16),
        scratch_shapes=[pltpu.VMEM((T, 128), F32), pltpu.VMEM((T, 128), F32),
                        pltpu.VMEM((T, 128), F32), pltpu.VMEM((GROUP, 3, 128, 128), F32),
                        pltpu.VMEM((6, 128, 256), F32)] + _pipeline_scratch(128, 256),
        compiler_params=_params(2),
        name="dilated",
    )(qkv_nat, qkv_nat, qkv_nat, qkv_perm, qkv_perm, qkv_perm)


def _nbr_kernel(q_ref, k_ref, v_ref, bias_ref, o_ref, s_scr, p_scr, m_scr, *, T):
    rows = T // GRID_W
    BK = NA_ROWS * GRID_W
    lo = lax.broadcasted_iota(jnp.int32, (GRID_W, 128), 1) < HEAD_DIM

    def where(g, t):
        r = jnp.asarray(g * GROUP + t, jnp.int32)
        rs = jnp.clip(r - NA_ROWS // 2, 0, rows - NA_ROWS)
        return (pl.multiple_of(r * GRID_W, GRID_W), pl.multiple_of(rs * GRID_W, GRID_W),
                rs - r + (NA_ROWS - 1))

    def load_qk(g, t):
        q0, k0, var = where(g, t)
        return (q_ref[0, pl.ds(q0, GRID_W), :], k_ref[0, pl.ds(k0, BK), :],
                bias_ref[0, var], bias_ref[1, var])

    def load_v(g, t):
        return v_ref[0, pl.ds(where(g, t)[1], BK), :]

    def finalize(g, tiles):
        for t, (_, l, a) in enumerate(tiles):
            o_ref[0, pl.ds(where(g, t)[0], GRID_W), :] = (a / l).astype(BF16)

    _attn_pipeline([(rows // GROUP, load_qk, load_v, finalize)], lo, s_scr, p_scr, m_scr)


def _nbr(qkv_nat, bias):
    B, T, _ = qkv_nat.shape
    assert T % (GRID_W * GROUP) == 0 and T // GRID_W >= NA_ROWS
    BK = NA_ROWS * GRID_W
    seq = lambda col0: pl.BlockSpec((1, T, 128), lambda b, j: (b, 0, col0 + j))
    return pl.pallas_call(
        functools.partial(_nbr_kernel, T=T),
        grid=(B, N_PAIRS),
        in_specs=[seq(3 * N_PAIRS), seq(4 * N_PAIRS), seq(5 * N_PAIRS),
                  pl.BlockSpec((2, NA_ROWS, GRID_W, BK), lambda b, j: (j, 0, 0, 0))],
        out_specs=pl.BlockSpec((1, T, 128), lambda b, j: (b, 0, j)),
        out_shape=jax.ShapeDtypeStruct((B, T, WIDTH_B), BF16),
        scratch_shapes=_pipeline_scratch(GRID_W, BK),
        compiler_params=_params(2),
        name="nbr",
    )(qkv_nat, qkv_nat, qkv_nat, bias)


def _nbr_bias(rpb):
    qc = jnp.arange(GRID_W)[:, None]
    kc = jnp.arange(GRID_W)[None, :]
    cidx = jnp.clip(kc - qc, -(NA_COLS - 1), NA_COLS - 1) + NA_COLS - 1
    cs = jnp.clip(qc - NA_COLS // 2, 0, GRID_W - NA_COLS)
    col_ok = (kc >= cs) & (kc < cs + NA_COLS)
    ridx = jnp.arange(NA_ROWS)[:, None] + jnp.arange(NA_ROWS)[None, :]
    t = rpb.astype(F32)[:, ridx][:, :, :, cidx]
    t = jnp.where(col_ok[None, None, None], t, NEG_INF)
    t = t.transpose(0, 1, 3, 2, 4)
    return t.reshape(rpb.shape[0], NA_ROWS, GRID_W, NA_ROWS * GRID_W)


UP_CHUNKS = 4


def _tail_kernel(oa_ref, ob_ref, x_ref, kv_ref, ga_ref, gb_ref, g0_ref, b0_ref, wo_ref, g1_ref, b1_ref,
                 wq_ref, wxo_ref, g2_ref, b2_ref, wu_ref, wd_ref, g3_ref, b3_ref, o_ref, x2_scr, r_scr):
    s = pl.program_id(0)
    cur, prev = lax.rem(s, 2), lax.rem(s + 1, 2)

    @pl.when(s == 0)
    def _():
        x2_scr[1] = jnp.zeros(x2_scr.shape[1:], F32)
        r_scr[...] = jnp.zeros(r_scr.shape, F32)

    x2_prev = x2_scr[prev]
    xb_prev = x2_prev.astype(BF16)
    cw = D_FF // UP_CHUNKS
    hidden = []

    def up_chunk():
        c = len(hidden)
        h = jnp.maximum(jnp.dot(xb_prev, wu_ref[:, c * cw:(c + 1) * cw],
                                preferred_element_type=F32), 0.0)
        hidden.append((h * h).astype(BF16))

    up_chunk()
    y = jnp.concatenate([_rms(oa_ref[0].astype(F32), ga_ref[...]),
                         _rms(ob_ref[0].astype(F32), gb_ref[...])], axis=1).astype(BF16)
    z = jnp.dot(y, wo_ref[...], preferred_element_type=F32)
    x1 = _ln(ALPHA * _ln(x_ref[0], g0_ref[...], b0_ref[...]) + z, g1_ref[...], b1_ref[...])
    up_chunk()
    q = jnp.dot(x1.astype(BF16), wq_ref[...], preferred_element_type=F32)
    q = (q * (HEAD_DIM_X ** -0.5)).astype(BF16)
    outs = []
    for h in range(N_HEADS_X):
        k = kv_ref[0, :, h * HEAD_DIM_X:(h + 1) * HEAD_DIM_X]
        v = kv_ref[0, :, D_MODEL + h * HEAD_DIM_X:D_MODEL + (h + 1) * HEAD_DIM_X]
        sc = lax.dot_general(q[:, h * HEAD_DIM_X:(h + 1) * HEAD_DIM_X], k, NT_DIMS,
                             preferred_element_type=F32)
        pr = jnp.exp(sc - jnp.max(sc, axis=1, keepdims=True))
        l = jnp.sum(pr, axis=1, keepdims=True)
        outs.append(jnp.dot(pr.astype(BF16), v, preferred_element_type=F32) / l)
    up_chunk()
    z = jnp.dot(jnp.concatenate(outs, axis=1).astype(BF16), wxo_ref[...], preferred_element_type=F32)
    x2_scr[cur] = _ln(ALPHA * x1 + z, g2_ref[...], b2_ref[...])
    while len(hidden) < UP_CHUNKS:
        up_chunk()
    o_ref[0] = _ln(r_scr[prev], g3_ref[...], b3_ref[...])
    z = jnp.dot(jnp.concatenate(hidden, axis=1), wd_ref[...], preferred_element_type=F32)
    r_scr[cur] = ALPHA * x2_prev + z


def _tail(oa, ob, x, kv, p, tm):
    B, T, _ = x.shape
    M = kv.shape[1]
    n = T // tm
    last = B * n - 1
    vec = lambda width: _const_spec((1, width))
    mat = lambda rows, cols: _const_spec((rows, cols))

    def cur(width):
        return pl.BlockSpec((1, tm, width),
                            lambda s: (jnp.minimum(s, last) // n, jnp.minimum(s, last) % n, 0))

    return pl.pallas_call(
        _tail_kernel,
        grid=(B * n + 2,),
        in_specs=[cur(WIDTH_A), cur(WIDTH_B), cur(D_MODEL),
                  pl.BlockSpec((1, M, 2 * D_MODEL), lambda s: (jnp.minimum(s, last) // n, 0, 0)),
                  vec(WIDTH_A), vec(WIDTH_B), vec(D_MODEL), vec(D_MODEL),
                  mat(D_MODEL, D_MODEL), vec(D_MODEL), vec(D_MODEL),
                  mat(D_MODEL, D_MODEL), mat(D_MODEL, D_MODEL), vec(D_MODEL), vec(D_MODEL),
                  mat(D_MODEL, D_FF), mat(D_FF, D_MODEL), vec(D_MODEL), vec(D_MODEL)],
        out_specs=pl.BlockSpec((1, tm, D_MODEL),
                               lambda s: (jnp.maximum(s - 2, 0) // n, jnp.maximum(s - 2, 0) % n, 0)),
        out_shape=jax.ShapeDtypeStruct((B, T, D_MODEL), F32),
        scratch_shapes=[pltpu.VMEM((2, tm, D_MODEL), F32), pltpu.VMEM((2, tm, D_MODEL), F32)],
        compiler_params=_params(1),
        name="tail",
    )(oa, ob, x, kv, p["g_mix_a"], p["g_mix_b"], p["ln_in_g"], p["ln_in_b"], p["w_out"],
      p["ln1_g"], p["ln1_b"], p["w_xq"], p["w_xo"], p["ln2_g"], p["ln2_b"],
      p["w_up"], p["w_down"], p["ln3_g"], p["ln3_b"])


def _rope_tables(T):
    half = HEAD_DIM // 2
    inv = ROPE_THETA ** (-jnp.arange(half, dtype=F32) / half)
    ang = jnp.arange(T, dtype=F32)[:, None] * inv[None, :]
    cos, sin = jnp.cos(ang), jnp.sin(ang)
    return (jnp.concatenate([cos, cos, cos, cos], axis=1),
            jnp.concatenate([-sin, sin, -sin, sin], axis=1))


def _trunk(x, mem, p, tm=256):
    B, T, _ = x.shape
    cos, sin = _rope_tables(T)
    kv = _mem_kv(mem, p["w_xkv"])
    qkv_nat, qkv_perm = _qkv(x, p["ln_in_g"], p["ln_in_b"], p["w_in"], cos, sin, 2 * tm)
    oa = _dilated(qkv_nat, qkv_perm.reshape(B, T, 3 * WIDTH_A))
    ob = _nbr(qkv_nat, p["bias"])
    return _tail(oa, ob, x, kv, p, tm)


def kernel(x_prompt, x_sample, mem_prompt, mem_sample, ln_in_g, ln_in_b, w_in, rpb, g_mix_a, g_mix_b,
           w_out, ln1_g, ln1_b, w_xq, w_xkv, w_xo, ln2_g, ln2_b, w_up, w_down, ln3_g, ln3_b):
    assert w_in.shape[0] == 1, "single-layer trunk"
    row = lambda v: v.reshape(1, -1).astype(F32)
    p = dict(
        ln_in_g=row(ln_in_g), ln_in_b=row(ln_in_b),
        w_in=w_in[0].astype(BF16), bias=_nbr_bias(rpb[0]),
        g_mix_a=row(g_mix_a[0]), g_mix_b=row(g_mix_b[0]), w_out=w_out[0].astype(BF16),
        ln1_g=row(ln1_g[0]), ln1_b=row(ln1_b[0]),
        w_xq=w_xq[0].astype(BF16), w_xkv=w_xkv[0].astype(BF16), w_xo=w_xo[0].astype(BF16),
        ln2_g=row(ln2_g[0]), ln2_b=row(ln2_b[0]),
        w_up=w_up[0].astype(BF16), w_down=w_down[0].astype(BF16),
        ln3_g=row(ln3_g[0]), ln3_b=row(ln3_b[0]),
    )
    return _trunk(x_prompt, mem_prompt, p), _trunk(x_sample, mem_sample, p)
```

```python
import functools

import jax
import jax.numpy as jnp
from jax import lax
from jax.experimental import pallas as pl
from jax.experimental.pallas import tpu as pltpu

F32 = jnp.float32
BF16 = jnp.bfloat16

D_MODEL = 1024
HEAD_DIM = 64
WIDTH_A = 512
WIDTH_B = 512
N_PAIRS = WIDTH_A // 128
GRID_W = 64
NA_ROWS = 8
NA_COLS = 16
N_HEADS_X = 4
HEAD_DIM_X = 256
D_FF = 4096
ROPE_THETA = 10000.0
LN_EPS = 1e-5
ALPHA = 2.0 ** 0.25
NEG_INF = -1e30
N_SIDE = 64
MAX_DIL = 16

VMEM_LIMIT = 56 * 1024 * 1024

NT_DIMS = (((1,), (1,)), ((), ()))


def _ln(x, g, b):
    mu = jnp.mean(x, axis=-1, keepdims=True)
    xc = x - mu
    var = jnp.mean(xc * xc, axis=-1, keepdims=True)
    return xc * lax.rsqrt(var + LN_EPS) * g + b


def _rms(x, g):
    return x * lax.rsqrt(jnp.mean(x * x, axis=-1, keepdims=True) + LN_EPS) * g


def _params(n_axes):
    return pltpu.CompilerParams(dimension_semantics=("arbitrary",) * n_axes,
                                vmem_limit_bytes=VMEM_LIMIT)


def _const_spec(shape):
    nd = len(shape)
    return pl.BlockSpec(shape, lambda *_: (0,) * nd, pipeline_mode=pl.Buffered(1))


def _mem_kv_kernel(mem_ref, w_ref, o_ref):
    o_ref[0] = jnp.dot(mem_ref[0].astype(BF16), w_ref[...],
                       preferred_element_type=F32).astype(BF16)


def _mem_kv(mem, w_xkv):
    B, M, _ = mem.shape
    return pl.pallas_call(
        _mem_kv_kernel,
        grid=(B,),
        in_specs=[pl.BlockSpec((1, M, D_MODEL), lambda b: (b, 0, 0)),
                  _const_spec((D_MODEL, 2 * D_MODEL))],
        out_specs=pl.BlockSpec((1, M, 2 * D_MODEL), lambda b: (b, 0, 0)),
        out_shape=jax.ShapeDtypeStruct((B, M, 2 * D_MODEL), BF16),
        compiler_params=_params(1),
        name="mem_kv",
    )(mem, w_xkv)


def _qkv_kernel(x_ref, g_ref, b_ref, w_ref, cos_ref, sin_ref, nat_ref, perm_ref, scr_ref, *, tm):
    xb = _ln(x_ref[0], g_ref[...], b_ref[...]).astype(BF16)
    cos = jnp.concatenate([cos_ref[...]] * 4, axis=1)
    sin = jnp.concatenate([sin_ref[...]] * 4, axis=1)
    lane = lax.broadcasted_iota(jnp.int32, (tm, WIDTH_A), 1)
    first_half = (lane % HEAD_DIM) < HEAD_DIM // 2
    for part in (0, 3, 1, 4, 2, 5):
        r = jnp.dot(xb, w_ref[:, part * 512:(part + 1) * 512], preferred_element_type=F32)
        if part in (0, 1):
            swapped = jnp.where(first_half, pltpu.roll(r, WIDTH_A - 32, 1), pltpu.roll(r, 32, 1))
            r = r * cos + swapped * sin
        if part in (0, 3):
            r = r * (HEAD_DIM ** -0.5)
        nat_ref[0, :, part * 512:(part + 1) * 512] = r.astype(BF16)
        if part < 3:
            for s in range(4):
                slab = part * 4 + s
                scr_ref[slab] = r[:, s * 128:(s + 1) * 128]
                for c in range(MAX_DIL):
                    perm_ref[0, c, :, slab * 128:(slab + 1) * 128] = (
                        scr_ref[slab, pl.ds(c, tm // MAX_DIL, stride=MAX_DIL), :].astype(BF16))


def _qkv(x, ln_g, ln_b, w_in, cos, sin, tm):
    B, T, _ = x.shape
    L16 = T // MAX_DIL
    return pl.pallas_call(
        functools.partial(_qkv_kernel, tm=tm),
        grid=(B, T // tm),
        in_specs=[pl.BlockSpec((1, tm, D_MODEL), lambda b, i: (b, i, 0)),
                  _const_spec((1, D_MODEL)), _const_spec((1, D_MODEL)),
                  _const_spec((D_MODEL, 3 * D_MODEL)),
                  pl.BlockSpec((tm, 128), lambda b, i: (i, 0)),
                  pl.BlockSpec((tm, 128), lambda b, i: (i, 0))],
        out_specs=[pl.BlockSpec((1, tm, 3 * D_MODEL), lambda b, i: (b, i, 0)),
                   pl.BlockSpec((1, MAX_DIL, tm // MAX_DIL, 3 * WIDTH_A), lambda b, i: (b, 0, i, 0))],
        out_shape=[jax.ShapeDtypeStruct((B, T, 3 * D_MODEL), BF16),
                   jax.ShapeDtypeStruct((B, MAX_DIL, L16, 3 * WIDTH_A), BF16)],
        scratch_shapes=[pltpu.VMEM((12, tm, 128), F32)],
        compiler_params=_params(2),
        name="qkv",
    )(x, ln_g, ln_b, w_in, cos, sin)


GROUP = 4


def _attn_pipeline(segments, lo, s_scr, p_scr, m_scr):
    bq, bk = s_scr.shape[-2:]

    def stage_a(seg, g, slot):
        for t in range(GROUP):
            q2, k2, b_lo, b_hi = seg[1](g, t)
            zero = jnp.zeros_like(q2)
            for h, (q1, bias) in enumerate(((jnp.where(lo, q2, zero), b_lo),
                                            (jnp.where(lo, zero, q2), b_hi))):
                s = lax.dot_general(q1, k2, NT_DIMS, preferred_element_type=F32) + bias
                s_scr[slot, t, h] = s
                m_scr[slot, t, h] = jnp.broadcast_to(jnp.max(s, axis=1, keepdims=True), (bq, 128))

    def stage_b(slot):
        for t in range(GROUP):
            for h in range(2):
                m = jnp.concatenate([m_scr[slot, t, h]] * (bk // 128), axis=1)
                p_scr[slot, t, h] = jnp.exp(s_scr[slot, t, h] - m).astype(BF16)

    def stage_c(seg, g, slot):
        tiles = []
        for t in range(GROUP):
            v2 = seg[2](g, t)
            v_ones = jnp.concatenate([v2, jnp.ones_like(v2)], axis=1)
            pv0 = jnp.dot(p_scr[slot, t, 0], v_ones, preferred_element_type=F32)
            pv1 = jnp.dot(p_scr[slot, t, 1], v_ones, preferred_element_type=F32)
            tiles.append((jnp.where(lo, m_scr[slot, t, 0], m_scr[slot, t, 1]),
                          jnp.where(lo, pv0[:, 128:], pv1[:, 128:]),
                          jnp.where(lo, pv0[:, :128], pv1[:, :128])))
        seg[3](g, tiles)

    stage_a(segments[0], 0, 0)
    stage_b(0)
    stage_a(segments[0], 1, 1)
    for k, seg in enumerate(segments):
        n = seg[0]
        assert n >= 2 and n % 2 == 0

        def body(j, carry, seg=seg):
            for slot in range(2):
                stage_c(seg, 2 * j + slot, slot)
                stage_b(1 - slot)
                stage_a(seg, 2 * j + slot + 2, slot)
            return carry

        lax.fori_loop(0, (n - 2) // 2, body, 0)
        nxt = segments[k + 1] if k + 1 < len(segments) else None
        stage_c(seg, n - 2, 0)
        stage_b(1)
        if nxt:
            stage_a(nxt, 0, 0)
        stage_c(seg, n - 1, 1)
        if nxt:
            stage_b(0)
            stage_a(nxt, 1, 1)


def _pipeline_scratch(bq, bk):
    return [pltpu.VMEM((2, GROUP, 2, bq, bk), F32), pltpu.VMEM((2, GROUP, 2, bq, bk), BF16),
            pltpu.VMEM((2, GROUP, 2, bq, 128), F32)]


def _merge(m_r, l_r, a_r, m_t, l_t, a_t):
    m_n = jnp.maximum(m_r, m_t)
    e_r = jnp.exp(m_r - m_n)
    e_t = jnp.exp(m_t - m_n)
    return m_n, l_r * e_r + l_t * e_t, a_r * e_r + a_t * e_t


def _dilated_kernel(qn_ref, kn_ref, vn_ref, qp_ref, kp_ref, vp_ref, o_ref,
                    m_ref, l_ref, a_ref, tmp_ref, mask_ref, s_scr, p_scr, m_scr, *, T):
    BQ, BK = 128, 256
    L16 = T // MAX_DIL
    row = lax.broadcasted_iota(jnp.int32, (BQ, BK), 0)
    col = lax.broadcasted_iota(jnp.int32, (BQ, BK), 1)
    band = row - col
    band4 = 4 * ((row % 32) - (col % 64)) + (row // 32 - col // 64)
    for n in range(3):
        mask_ref[n] = jnp.where(jnp.abs(band + N_SIDE * n) <= N_SIDE, 0.0, NEG_INF)
        mask_ref[3 + n] = jnp.where(jnp.abs(band4 + N_SIDE * n) <= N_SIDE, 0.0, NEG_INF)
    lo = lax.broadcasted_iota(jnp.int32, (BQ, 128), 1) < HEAD_DIM
    n_groups = T // (BQ * GROUP)

    nib = L16 // BQ

    def rows16(g, t):
        tile = jnp.asarray(g * GROUP + t, jnp.int32)
        c = lax.div(tile, nib)
        ib = lax.rem(tile, nib)
        kst = jnp.clip(ib * BQ - N_SIDE, 0, L16 - BK)
        return (pl.multiple_of(c * L16 + ib * BQ, BQ), pl.multiple_of(c * L16 + kst, N_SIDE),
                lax.div(ib * BQ - kst, N_SIDE))

    def load_qk16(g, t):
        q0, k0, case = rows16(g, t)
        mask = mask_ref[case]
        return qp_ref[0, pl.ds(q0, BQ), :], kp_ref[0, pl.ds(k0, BK), :], mask, mask

    def load_v16(g, t):
        return vp_ref[0, pl.ds(rows16(g, t)[1], BK), :]

    def finalize16(g, tiles):
        for t, (m, l, a) in enumerate(tiles):
            q0 = rows16(g, t)[0]
            m_ref[pl.ds(q0, BQ), :] = m
            l_ref[pl.ds(q0, BQ), :] = l
            a_ref[pl.ds(q0, BQ), :] = a

    nb4 = L16 // 32

    def rows4(g, t):
        tile = jnp.asarray(g * GROUP + t, jnp.int32)
        r4 = lax.div(tile, nb4)
        i0 = lax.rem(tile, nb4) * 32
        kst = jnp.clip(i0 - 16, 0, L16 - 64)
        return ([pl.multiple_of((r4 + 4 * w) * L16 + i0, 32) for w in range(4)],
                [pl.multiple_of((r4 + 4 * w) * L16 + kst, 16) for w in range(4)],
                3 + lax.div(i0 - kst, 16))

    def load_qk4(g, t):
        qrows, krows, case = rows4(g, t)
        mask = mask_ref[case]
        return (jnp.concatenate([qp_ref[0, pl.ds(r, 32), :] for r in qrows], axis=0),
                jnp.concatenate([kp_ref[0, pl.ds(r, 64), :] for r in krows], axis=0), mask, mask)

    def load_v4(g, t):
        return jnp.concatenate([vp_ref[0, pl.ds(r, 64), :] for r in rows4(g, t)[1]], axis=0)

    def finalize4(g, tiles):
        merged = []
        for t, tile in enumerate(tiles):
            qrows = rows4(g, t)[0]
            state = [jnp.concatenate([ref[pl.ds(r, 32), :] for r in qrows], axis=0)
                     for ref in (m_ref, l_ref, a_ref)]
            merged.append((qrows, _merge(*state, *tile)))
        for qrows, new in merged:
            for ref, val in zip((m_ref, l_ref, a_ref), new):
                for w, r in enumerate(qrows):
                    ref[pl.ds(r, 32), :] = val[w * 32:(w + 1) * 32]

    def rows1(g, t):
        t0 = pl.multiple_of(jnp.asarray(g * GROUP + t, jnp.int32) * BQ, BQ)
        kst = pl.multiple_of(jnp.clip(t0 - N_SIDE, 0, T - BK), N_SIDE)
        return t0, kst, lax.div(t0 - kst, N_SIDE)

    def load_qk1(g, t):
        t0, kst, case = rows1(g, t)
        mask = mask_ref[case]
        return qn_ref[0, pl.ds(t0, BQ), :], kn_ref[0, pl.ds(kst, BK), :], mask, mask

    def load_v1(g, t):
        return vn_ref[0, pl.ds(rows1(g, t)[1], BK), :]

    def finalize1(g, tiles):
        n_i = BQ // MAX_DIL
        for t, tile in enumerate(tiles):
            t0 = rows1(g, t)[0]
            i0 = lax.div(t0, MAX_DIL)
            for c in range(MAX_DIL):
                src = pl.ds(pl.multiple_of(c * L16 + i0, n_i), n_i)
                dst = pl.ds(c, n_i, stride=MAX_DIL)
                for n, ref in enumerate((m_ref, l_ref, a_ref)):
                    tmp_ref[t, n, dst, :] = ref[src, :]
            _, l_n, a_n = _merge(tmp_ref[t, 0], tmp_ref[t, 1], tmp_ref[t, 2], *tile)
            o_ref[0, pl.ds(t0, BQ), :] = (a_n / l_n).astype(BF16)

    _attn_pipeline([(n_groups, load_qk16, load_v16, finalize16),
                    (n_groups, load_qk4, load_v4, finalize4),
                    (n_groups, load_qk1, load_v1, finalize1)], lo, s_scr, p_scr, m_scr)


def _dilated(qkv_nat, qkv_perm):
    B, T, _ = qkv_nat.shape
    assert T % (MAX_DIL * 128) == 0 and T % (128 * GROUP) == 0 and T // MAX_DIL >= 256
    seq = lambda col0: pl.BlockSpec((1, T, 128), lambda b, j: (b, 0, col0 + j))
    return pl.pallas_call(
        functools.partial(_dilated_kernel, T=T),
        grid=(B, N_PAIRS),
        in_specs=[seq(0), seq(N_PAIRS), seq(2 * N_PAIRS), seq(0), seq(N_PAIRS), seq(2 * N_PAIRS)],
        out_specs=pl.BlockSpec((1, T, 128), lambda b, j: (b, 0, j)),
        out_shape=jax.ShapeDtypeStruct((B, T, WIDTH_A), BF16),
        scratch_shapes=[pltpu.VMEM((T, 128), F32), pltpu.VMEM((T, 128), F32),
                        pltpu.VMEM((T, 128), F32), pltpu.VMEM((GROUP, 3, 128, 128), F32),
                        pltpu.VMEM((6, 128, 256), F32)] + _pipeline_scratch(128, 256),
        compiler_params=_params(2),
        name="dilated",
    )(qkv_nat, qkv_nat, qkv_nat, qkv_perm, qkv_perm, qkv_perm)


NBR_QROWS = 2
NBR_KROWS = NA_ROWS + 2
NBR_BQ = NBR_QROWS * GRID_W
NBR_BK = NBR_KROWS * GRID_W
NBR_VARIANTS = ((0, (0, 0)), (2, (0, 0)), (4, (0, 1)), (6, (2, 2)), (8, (2, 2)))


def _nbr_kernel(q_ref, k_ref, v_ref, bias_ref, o_ref, s_scr, p_scr, m_scr, *, T):
    rows = T // GRID_W
    lo = lax.broadcasted_iota(jnp.int32, (NBR_BQ, 128), 1) < HEAD_DIM

    def where(g, t):
        r = NBR_QROWS * jnp.asarray(g * GROUP + t, jnp.int32)
        first_key_row = jnp.clip(r - NA_ROWS // 2, 0, rows - NBR_KROWS)
        variant = lax.div(r - first_key_row, 2)
        return (pl.multiple_of(r * GRID_W, NBR_BQ), pl.multiple_of(first_key_row * GRID_W, GRID_W),
                variant)

    def load_qk(g, t):
        q0, k0, variant = where(g, t)
        return (q_ref[0, pl.ds(q0, NBR_BQ), :], k_ref[0, pl.ds(k0, NBR_BK), :],
                bias_ref[0, variant], bias_ref[1, variant])

    def load_v(g, t):
        return v_ref[0, pl.ds(where(g, t)[1], NBR_BK), :]

    def finalize(g, tiles):
        for t, (_, l, a) in enumerate(tiles):
            o_ref[0, pl.ds(where(g, t)[0], NBR_BQ), :] = (a / l).astype(BF16)

    _attn_pipeline([(rows // (NBR_QROWS * GROUP), load_qk, load_v, finalize)],
                   lo, s_scr, p_scr, m_scr)


def _nbr(qkv_nat, bias):
    B, T, _ = qkv_nat.shape
    assert T % (NBR_BQ * GROUP) == 0 and T // GRID_W >= 2 * NBR_KROWS
    seq = lambda col0: pl.BlockSpec((1, T, 128), lambda b, j: (b, 0, col0 + j))
    return pl.pallas_call(
        functools.partial(_nbr_kernel, T=T),
        grid=(B, N_PAIRS),
        in_specs=[seq(3 * N_PAIRS), seq(4 * N_PAIRS), seq(5 * N_PAIRS),
                  pl.BlockSpec((2, len(NBR_VARIANTS), NBR_BQ, NBR_BK), lambda b, j: (j, 0, 0, 0))],
        out_specs=pl.BlockSpec((1, T, 128), lambda b, j: (b, 0, j)),
        out_shape=jax.ShapeDtypeStruct((B, T, WIDTH_B), BF16),
        scratch_shapes=_pipeline_scratch(NBR_BQ, NBR_BK),
        compiler_params=_params(2),
        name="nbr",
    )(qkv_nat, qkv_nat, qkv_nat, bias)


def _nbr_bias(rpb):
    qc = jnp.arange(GRID_W)[:, None]
    kc = jnp.arange(GRID_W)[None, :]
    cidx = jnp.clip(kc - qc, -(NA_COLS - 1), NA_COLS - 1) + NA_COLS - 1
    cs = jnp.clip(qc - NA_COLS // 2, 0, GRID_W - NA_COLS)
    col_ok = (kc >= cs) & (kc < cs + NA_COLS)
    base = jnp.where(col_ok, rpb.astype(F32)[:, :, cidx], NEG_INF)
    off = jnp.array([v[0] for v in NBR_VARIANTS])[:, None, None]
    first = jnp.array([v[1] for v in NBR_VARIANTS])[:, :, None]
    q = jnp.arange(NBR_QROWS)[None, :, None]
    u = jnp.arange(NBR_KROWS)[None, None, :]
    row_ok = (u >= first) & (u < first + NA_ROWS)
    ridx = jnp.clip(u - off - q + NA_ROWS - 1, 0, 2 * NA_ROWS - 2)
    t = jnp.where(row_ok[None, :, :, :, None, None], base[:, ridx], NEG_INF)
    t = t.transpose(0, 1, 2, 4, 3, 5)
    return t.reshape(rpb.shape[0], len(NBR_VARIANTS), NBR_BQ, NBR_BK)


UP_CHUNKS = 4


def _tail_kernel(oa_ref, ob_ref, x_ref, kv_ref, ga_ref, gb_ref, g0_ref, b0_ref, wo_ref, g1_ref, b1_ref,
                 wq_ref, wxo_ref, g2_ref, b2_ref, wu_ref, wd_ref, g3_ref, b3_ref, o_ref, x2_scr, r_scr):
    s = pl.program_id(0)
    cur, prev = lax.rem(s, 2), lax.rem(s + 1, 2)

    @pl.when(s == 0)
    def _():
        x2_scr[1] = jnp.zeros(x2_scr.shape[1:], F32)
        r_scr[...] = jnp.zeros(r_scr.shape, F32)

    x2_prev = x2_scr[prev]
    xb_prev = x2_prev.astype(BF16)
    cw = D_FF // UP_CHUNKS
    hidden = []

    def up_chunk():
        c = len(hidden)
        h = jnp.maximum(jnp.dot(xb_prev, wu_ref[:, c * cw:(c + 1) * cw],
                                preferred_element_type=F32), 0.0)
        hidden.append((h * h).astype(BF16))

    up_chunk()
    y = jnp.concatenate([_rms(oa_ref[0].astype(F32), ga_ref[...]),
                         _rms(ob_ref[0].astype(F32), gb_ref[...])], axis=1).astype(BF16)
    z = jnp.dot(y, wo_ref[...], preferred_element_type=F32)
    x1 = _ln(ALPHA * _ln(x_ref[0], g0_ref[...], b0_ref[...]) + z, g1_ref[...], b1_ref[...])
    up_chunk()
    q = jnp.dot(x1.astype(BF16), wq_ref[...], preferred_element_type=F32)
    q = (q * (HEAD_DIM_X ** -0.5)).astype(BF16)
    outs = []
    for h in range(N_HEADS_X):
        k = kv_ref[0, :, h * HEAD_DIM_X:(h + 1) * HEAD_DIM_X]
        v = kv_ref[0, :, D_MODEL + h * HEAD_DIM_X:D_MODEL + (h + 1) * HEAD_DIM_X]
        sc = lax.dot_general(q[:, h * HEAD_DIM_X:(h + 1) * HEAD_DIM_X], k, NT_DIMS,
                             preferred_element_type=F32)
        pr = jnp.exp(sc - jnp.max(sc, axis=1, keepdims=True))
        l = jnp.sum(pr, axis=1, keepdims=True)
        outs.append(jnp.dot(pr.astype(BF16), v, preferred_element_type=F32) / l)
    up_chunk()
    z = jnp.dot(jnp.concatenate(outs, axis=1).astype(BF16), wxo_ref[...], preferred_element_type=F32)
    x2_scr[cur] = _ln(ALPHA * x1 + z, g2_ref[...], b2_ref[...])
    while len(hidden) < UP_CHUNKS:
        up_chunk()
    o_ref[0] = _ln(r_scr[prev], g3_ref[...], b3_ref[...])
    z = jnp.dot(jnp.concatenate(hidden, axis=1), wd_ref[...], preferred_element_type=F32)
    r_scr[cur] = ALPHA * x2_prev + z


def _tail(oa, ob, x, kv, p, tm):
    B, T, _ = x.shape
    M = kv.shape[1]
    n = T // tm
    last = B * n - 1
    vec = lambda width: _const_spec((1, width))
    mat = lambda rows, cols: _const_spec((rows, cols))

    def cur(width):
        return pl.BlockSpec((1, tm, width),
                            lambda s: (jnp.minimum(s, last) // n, jnp.minimum(s, last) % n, 0))

    return pl.pallas_call(
        _tail_kernel,
        grid=(B * n + 2,),
        in_specs=[cur(WIDTH_A), cur(WIDTH_B), cur(D_MODEL),
                  pl.BlockSpec((1, M, 2 * D_MODEL), lambda s: (jnp.minimum(s, last) // n, 0, 0)),
                  vec(WIDTH_A), vec(WIDTH_B), vec(D_MODEL), vec(D_MODEL),
                  mat(D_MODEL, D_MODEL), vec(D_MODEL), vec(D_MODEL),
                  mat(D_MODEL, D_MODEL), mat(D_MODEL, D_MODEL), vec(D_MODEL), vec(D_MODEL),
                  mat(D_MODEL, D_FF), mat(D_FF, D_MODEL), vec(D_MODEL), vec(D_MODEL)],
        out_specs=pl.BlockSpec((1, tm, D_MODEL),
                               lambda s: (jnp.maximum(s - 2, 0) // n, jnp.maximum(s - 2, 0) % n, 0)),
        out_shape=jax.ShapeDtypeStruct((B, T, D_MODEL), F32),
        scratch_shapes=[pltpu.VMEM((2, tm, D_MODEL), F32), pltpu.VMEM((2, tm, D_MODEL), F32)],
        compiler_params=_params(1),
        name="tail",
    )(oa, ob, x, kv, p["g_mix_a"], p["g_mix_b"], p["ln_in_g"], p["ln_in_b"], p["w_out"],
      p["ln1_g"], p["ln1_b"], p["w_xq"], p["w_xo"], p["ln2_g"], p["ln2_b"],
      p["w_up"], p["w_down"], p["ln3_g"], p["ln3_b"])


def _rope_tables(T):
    half = HEAD_DIM // 2
    inv = ROPE_THETA ** (-jnp.arange(half, dtype=F32) / half)
    ang = jnp.arange(T, dtype=F32)[:, None] * inv[None, :]
    cos, sin = jnp.cos(ang), jnp.sin(ang)
    return (jnp.concatenate([cos, cos, cos, cos], axis=1),
            jnp.concatenate([-sin, sin, -sin, sin], axis=1))


def _trunk(x, mem, p, tm=256):
    B, T, _ = x.shape
    cos, sin = _rope_tables(T)
    kv = _mem_kv(mem, p["w_xkv"])
    qkv_nat, qkv_perm = _qkv(x, p["ln_in_g"], p["ln_in_b"], p["w_in"], cos, sin, 2 * tm)
    oa = _dilated(qkv_nat, qkv_perm.reshape(B, T, 3 * WIDTH_A))
    ob = _nbr(qkv_nat, p["bias"])
    return _tail(oa, ob, x, kv, p, tm)


def kernel(x_prompt, x_sample, mem_prompt, mem_sample, ln_in_g, ln_in_b, w_in, rpb, g_mix_a, g_mix_b,
           w_out, ln1_g, ln1_b, w_xq, w_xkv, w_xo, ln2_g, ln2_b, w_up, w_down, ln3_g, ln3_b):
    assert w_in.shape[0] == 1, "single-layer trunk"
    row = lambda v: v.reshape(1, -1).astype(F32)
    p = dict(
        ln_in_g=row(ln_in_g), ln_in_b=row(ln_in_b),
        w_in=w_in[0].astype(BF16), bias=_nbr_bias(rpb[0]),
        g_mix_a=row(g_mix_a[0]), g_mix_b=row(g_mix_b[0]), w_out=w_out[0].astype(BF16),
        ln1_g=row(ln1_g[0]), ln1_b=row(ln1_b[0]),
        w_xq=w_xq[0].astype(BF16), w_xkv=w_xkv[0].astype(BF16), w_xo=w_xo[0].astype(BF16),
        ln2_g=row(ln2_g[0]), ln2_b=row(ln2_b[0]),
        w_up=w_up[0].astype(BF16), w_down=w_down[0].astype(BF16),
        ln3_g=row(ln3_g[0]), ln3_b=row(ln3_b[0]),
    )
    return _trunk(x_prompt, mem_prompt, p), _trunk(x_sample, mem_sample, p)
```

```python
import functools

import jax
import jax.numpy as jnp
from jax import lax
from jax.experimental import pallas as pl
from jax.experimental.pallas import tpu as pltpu

F32 = jnp.float32
BF16 = jnp.bfloat16

D_MODEL = 1024
HEAD_DIM = 64
WIDTH_A = 512
WIDTH_B = 512
N_PAIRS = WIDTH_A // 128
GRID_W = 64
NA_ROWS = 8
NA_COLS = 16
N_HEADS_X = 4
HEAD_DIM_X = 256
D_FF = 4096
ROPE_THETA = 10000.0
LN_EPS = 1e-5
ALPHA = 2.0 ** 0.25
NEG_INF = -1e30
N_SIDE = 64
MAX_DIL = 16

VMEM_LIMIT = 56 * 1024 * 1024

NT_DIMS = (((1,), (1,)), ((), ()))


def _ln(x, g, b):
    mu = jnp.mean(x, axis=-1, keepdims=True)
    xc = x - mu
    var = jnp.mean(xc * xc, axis=-1, keepdims=True)
    return xc * lax.rsqrt(var + LN_EPS) * g + b


def _rms(x, g):
    return x * lax.rsqrt(jnp.mean(x * x, axis=-1, keepdims=True) + LN_EPS) * g


def _params(n_axes):
    return pltpu.CompilerParams(dimension_semantics=("arbitrary",) * n_axes,
                                vmem_limit_bytes=VMEM_LIMIT)


def _const_spec(shape):
    nd = len(shape)
    return pl.BlockSpec(shape, lambda *_: (0,) * nd, pipeline_mode=pl.Buffered(1))


def _mem_kv_kernel(mem_ref, w_ref, o_ref):
    o_ref[0] = jnp.dot(mem_ref[0].astype(BF16), w_ref[...],
                       preferred_element_type=F32).astype(BF16)


def _mem_kv(mem, w_xkv):
    B, M, _ = mem.shape
    return pl.pallas_call(
        _mem_kv_kernel,
        grid=(B,),
        in_specs=[pl.BlockSpec((1, M, D_MODEL), lambda b: (b, 0, 0)),
                  _const_spec((D_MODEL, 2 * D_MODEL))],
        out_specs=pl.BlockSpec((1, M, 2 * D_MODEL), lambda b: (b, 0, 0)),
        out_shape=jax.ShapeDtypeStruct((B, M, 2 * D_MODEL), BF16),
        compiler_params=_params(1),
        name="mem_kv",
    )(mem, w_xkv)


PERM_SUB = 256


def _qkv_kernel(x_ref, g_ref, b_ref, w_ref, cos_ref, sin_ref, nat_ref, perm_ref, *, tm):
    xb = _ln(x_ref[0], g_ref[...], b_ref[...]).astype(BF16)
    cos = jnp.concatenate([cos_ref[...]] * 4, axis=1)
    sin = jnp.concatenate([sin_ref[...]] * 4, axis=1)
    row = lax.broadcasted_iota(jnp.int32, (PERM_SUB, PERM_SUB), 0)
    col = lax.broadcasted_iota(jnp.int32, (PERM_SUB, PERM_SUB), 1)
    rows_per = PERM_SUB // MAX_DIL
    perm_mat = jnp.where(col == MAX_DIL * (row % rows_per) + row // rows_per, 1.0, 0.0).astype(BF16)
    for part in (0, 3, 1, 4, 2, 5):
        cols = slice(part * 512, (part + 1) * 512)
        r = jnp.dot(xb, w_ref[:, cols], preferred_element_type=F32)
        if part in (0, 1):
            swapped = jnp.concatenate([pltpu.roll(r[:, c:c + 128], 64, 1)
                                       for c in range(0, WIDTH_A, 128)], axis=1)
            r = r * cos + swapped * sin
        if part in (0, 3):
            r = r * (HEAD_DIM ** -0.5)
        rb = r.astype(BF16)
        nat_ref[0, :, cols] = rb
        if part < 3:
            for g in range(tm // PERM_SUB):
                moved = jnp.dot(perm_mat, rb[g * PERM_SUB:(g + 1) * PERM_SUB],
                                preferred_element_type=F32).astype(BF16)
                for c in range(MAX_DIL):
                    perm_ref[0, c, g * rows_per:(g + 1) * rows_per, cols] = (
                        moved[c * rows_per:(c + 1) * rows_per])


def _qkv(x, ln_g, ln_b, w_in, cos, sin, tm):
    B, T, _ = x.shape
    L16 = T // MAX_DIL
    assert tm % PERM_SUB == 0
    return pl.pallas_call(
        functools.partial(_qkv_kernel, tm=tm),
        grid=(B, T // tm),
        in_specs=[pl.BlockSpec((1, tm, D_MODEL), lambda b, i: (b, i, 0)),
                  _const_spec((1, D_MODEL)), _const_spec((1, D_MODEL)),
                  _const_spec((D_MODEL, 3 * D_MODEL)),
                  pl.BlockSpec((tm, 128), lambda b, i: (i, 0)),
                  pl.BlockSpec((tm, 128), lambda b, i: (i, 0))],
        out_specs=[pl.BlockSpec((1, tm, 3 * D_MODEL), lambda b, i: (b, i, 0)),
                   pl.BlockSpec((1, MAX_DIL, tm // MAX_DIL, 3 * WIDTH_A), lambda b, i: (b, 0, i, 0))],
        out_shape=[jax.ShapeDtypeStruct((B, T, 3 * D_MODEL), BF16),
                   jax.ShapeDtypeStruct((B, MAX_DIL, L16, 3 * WIDTH_A), BF16)],
        compiler_params=_params(2),
        name="qkv",
    )(x, ln_g, ln_b, w_in, cos, sin)


GROUP = 4


def _attn_pipeline(segments, lo_q, lo, s_scr, p_scr, m_scr):
    bq, bk = s_scr.shape[-2:]

    def stage_a(seg, g, slot):
        for t in range(GROUP):
            q2, k2, b_lo, b_hi = seg[1](g, t)
            zero = jnp.zeros_like(q2)
            for h, (q1, bias) in enumerate(((jnp.where(lo_q, q2, zero), b_lo),
                                            (jnp.where(lo_q, zero, q2), b_hi))):
                s = lax.dot_general(q1, k2, NT_DIMS, preferred_element_type=F32) + bias
                s_scr[slot, t, h] = s
                m_scr[slot, t, h] = jnp.broadcast_to(jnp.max(s, axis=1, keepdims=True), (bq, 128))

    def stage_b(slot):
        for t in range(GROUP):
            for h in range(2):
                m = jnp.concatenate([m_scr[slot, t, h]] * (bk // 128), axis=1)
                p_scr[slot, t, h] = jnp.exp(s_scr[slot, t, h] - m).astype(BF16)

    def stage_c(seg, g, slot):
        tiles = []
        for t in range(GROUP):
            v2 = seg[2](g, t)
            v_ones = jnp.concatenate([v2, jnp.ones_like(v2)], axis=1)
            pv0 = jnp.dot(p_scr[slot, t, 0], v_ones, preferred_element_type=F32)
            pv1 = jnp.dot(p_scr[slot, t, 1], v_ones, preferred_element_type=F32)
            tiles.append((jnp.where(lo, m_scr[slot, t, 0], m_scr[slot, t, 1]),
                          jnp.where(lo, pv0[:, 128:], pv1[:, 128:]),
                          jnp.where(lo, pv0[:, :128], pv1[:, :128])))
        seg[3](g, tiles)

    stage_a(segments[0], 0, 0)
    stage_b(0)
    stage_a(segments[0], 1, 1)
    for k, seg in enumerate(segments):
        n = seg[0]
        assert n >= 2 and n % 2 == 0

        def body(j, carry, seg=seg):
            for slot in range(2):
                stage_c(seg, 2 * j + slot, slot)
                stage_b(1 - slot)
                stage_a(seg, 2 * j + slot + 2, slot)
            return carry

        lax.fori_loop(0, (n - 2) // 2, body, 0)
        nxt = segments[k + 1] if k + 1 < len(segments) else None
        stage_c(seg, n - 2, 0)
        stage_b(1)
        if nxt:
            stage_a(nxt, 0, 0)
        stage_c(seg, n - 1, 1)
        if nxt:
            stage_b(0)
            stage_a(nxt, 1, 1)


def _pipeline_scratch(bq, bk):
    return [pltpu.VMEM((2, GROUP, 2, bq, bk), F32), pltpu.VMEM((2, GROUP, 2, bq, bk), BF16),
            pltpu.VMEM((2, GROUP, 2, bq, 128), F32)]


def _merge(m_r, l_r, a_r, m_t, l_t, a_t):
    m_n = jnp.maximum(m_r, m_t)
    e_r = jnp.exp(m_r - m_n)
    e_t = jnp.exp(m_t - m_n)
    return m_n, l_r * e_r + l_t * e_t, a_r * e_r + a_t * e_t


def _dilated_kernel(qn_ref, kn_ref, vn_ref, qp_ref, kp_ref, vp_ref, o_ref,
                    m_ref, l_ref, a_ref, tmp_ref, mask_ref, s_scr, p_scr, m_scr, *, T):
    BQ, BK = 128, 256
    L16 = T // MAX_DIL
    row = lax.broadcasted_iota(jnp.int32, (BQ, BK), 0)
    col = lax.broadcasted_iota(jnp.int32, (BQ, BK), 1)
    band = row - col
    band4 = 4 * ((row % 32) - (col % 64)) + (row // 32 - col // 64)
    for n in range(3):
        mask_ref[n] = jnp.where(jnp.abs(band + N_SIDE * n) <= N_SIDE, 0.0, NEG_INF)
        mask_ref[3 + n] = jnp.where(jnp.abs(band4 + N_SIDE * n) <= N_SIDE, 0.0, NEG_INF)
    lane = lax.broadcasted_iota(jnp.int32, (BQ, 128), 1)
    lo = lane < HEAD_DIM
    lo_q = lane % HEAD_DIM < HEAD_DIM // 2
    n_groups = T // (BQ * GROUP)

    nib = L16 // BQ

    def rows16(g, t):
        tile = jnp.asarray(g * GROUP + t, jnp.int32)
        c = lax.div(tile, nib)
        ib = lax.rem(tile, nib)
        kst = jnp.clip(ib * BQ - N_SIDE, 0, L16 - BK)
        return (pl.multiple_of(c * L16 + ib * BQ, BQ), pl.multiple_of(c * L16 + kst, N_SIDE),
                lax.div(ib * BQ - kst, N_SIDE))

    def load_qk16(g, t):
        q0, k0, case = rows16(g, t)
        mask = mask_ref[case]
        return qp_ref[0, pl.ds(q0, BQ), :], kp_ref[0, pl.ds(k0, BK), :], mask, mask

    def load_v16(g, t):
        return vp_ref[0, pl.ds(rows16(g, t)[1], BK), :]

    def finalize16(g, tiles):
        for t, (m, l, a) in enumerate(tiles):
            q0 = rows16(g, t)[0]
            m_ref[pl.ds(q0, BQ), :] = m
            l_ref[pl.ds(q0, BQ), :] = l
            a_ref[pl.ds(q0, BQ), :] = a

    nb4 = L16 // 32

    def rows4(g, t):
        tile = jnp.asarray(g * GROUP + t, jnp.int32)
        r4 = lax.div(tile, nb4)
        i0 = lax.rem(tile, nb4) * 32
        kst = jnp.clip(i0 - 16, 0, L16 - 64)
        return ([pl.multiple_of((r4 + 4 * w) * L16 + i0, 32) for w in range(4)],
                [pl.multiple_of((r4 + 4 * w) * L16 + kst, 16) for w in range(4)],
                3 + lax.div(i0 - kst, 16))

    def load_qk4(g, t):
        qrows, krows, case = rows4(g, t)
        mask = mask_ref[case]
        return (jnp.concatenate([qp_ref[0, pl.ds(r, 32), :] for r in qrows], axis=0),
                jnp.concatenate([kp_ref[0, pl.ds(r, 64), :] for r in krows], axis=0), mask, mask)

    def load_v4(g, t):
        return jnp.concatenate([vp_ref[0, pl.ds(r, 64), :] for r in rows4(g, t)[1]], axis=0)

    def finalize4(g, tiles):
        merged = []
        for t, tile in enumerate(tiles):
            qrows = rows4(g, t)[0]
            state = [jnp.concatenate([ref[pl.ds(r, 32), :] for r in qrows], axis=0)
                     for ref in (m_ref, l_ref, a_ref)]
            merged.append((qrows, _merge(*state, *tile)))
        for qrows, new in merged:
            for ref, val in zip((m_ref, l_ref, a_ref), new):
                for w, r in enumerate(qrows):
                    ref[pl.ds(r, 32), :] = val[w * 32:(w + 1) * 32]

    def rows1(g, t):
        t0 = pl.multiple_of(jnp.asarray(g * GROUP + t, jnp.int32) * BQ, BQ)
        kst = pl.multiple_of(jnp.clip(t0 - N_SIDE, 0, T - BK), N_SIDE)
        return t0, kst, lax.div(t0 - kst, N_SIDE)

    def load_qk1(g, t):
        t0, kst, case = rows1(g, t)
        mask = mask_ref[case]
        return qn_ref[0, pl.ds(t0, BQ), :], kn_ref[0, pl.ds(kst, BK), :], mask, mask

    def load_v1(g, t):
        return vn_ref[0, pl.ds(rows1(g, t)[1], BK), :]

    def finalize1(g, tiles):
        n_i = BQ // MAX_DIL
        for t, tile in enumerate(tiles):
            t0 = rows1(g, t)[0]
            i0 = lax.div(t0, MAX_DIL)
            for c in range(MAX_DIL):
                src = pl.ds(pl.multiple_of(c * L16 + i0, n_i), n_i)
                dst = pl.ds(c, n_i, stride=MAX_DIL)
                for n, ref in enumerate((m_ref, l_ref, a_ref)):
                    tmp_ref[t, n, dst, :] = ref[src, :]
            _, l_n, a_n = _merge(tmp_ref[t, 0], tmp_ref[t, 1], tmp_ref[t, 2], *tile)
            o_ref[0, pl.ds(t0, BQ), :] = (a_n / l_n).astype(BF16)

    _attn_pipeline([(n_groups, load_qk16, load_v16, finalize16),
                    (n_groups, load_qk4, load_v4, finalize4),
                    (n_groups, load_qk1, load_v1, finalize1)], lo_q, lo, s_scr, p_scr, m_scr)


def _dilated(qkv_nat, qkv_perm):
    B, T, _ = qkv_nat.shape
    assert T % (MAX_DIL * 128) == 0 and T % (128 * GROUP) == 0 and T // MAX_DIL >= 256
    seq = lambda col0: pl.BlockSpec((1, T, 128), lambda b, j: (b, 0, col0 + j))
    return pl.pallas_call(
        functools.partial(_dilated_kernel, T=T),
        grid=(B, N_PAIRS),
        in_specs=[seq(0), seq(N_PAIRS), seq(2 * N_PAIRS), seq(0), seq(N_PAIRS), seq(2 * N_PAIRS)],
        out_specs=pl.BlockSpec((1, T, 128), lambda b, j: (b, 0, j)),
        out_shape=jax.ShapeDtypeStruct((B, T, WIDTH_A), BF16),
        scratch_shapes=[pltpu.VMEM((T, 128), F32), pltpu.VMEM((T, 128), F32),
                        pltpu.VMEM((T, 128), F32), pltpu.VMEM((GROUP, 3, 128, 128), F32),
                        pltpu.VMEM((6, 128, 256), F32)] + _pipeline_scratch(128, 256),
        compiler_params=_params(2),
        name="dilated",
    )(qkv_nat, qkv_nat, qkv_nat, qkv_perm, qkv_perm, qkv_perm)


NBR_QROWS = 2
NBR_KROWS = NA_ROWS + 2
NBR_BQ = NBR_QROWS * GRID_W
NBR_BK = NBR_KROWS * GRID_W
NBR_VARIANTS = ((0, (0, 0)), (2, (0, 0)), (4, (0, 1)), (6, (2, 2)), (8, (2, 2)))


def _nbr_kernel(q_ref, k_ref, v_ref, bias_ref, o_ref, s_scr, p_scr, m_scr, *, T):
    rows = T // GRID_W
    lo = lax.broadcasted_iota(jnp.int32, (NBR_BQ, 128), 1) < HEAD_DIM

    def where(g, t):
        r = NBR_QROWS * jnp.asarray(g * GROUP + t, jnp.int32)
        first_key_row = jnp.clip(r - NA_ROWS // 2, 0, rows - NBR_KROWS)
        variant = lax.div(r - first_key_row, 2)
        return (pl.multiple_of(r * GRID_W, NBR_BQ), pl.multiple_of(first_key_row * GRID_W, GRID_W),
                variant)

    def load_qk(g, t):
        q0, k0, variant = where(g, t)
        return (q_ref[0, pl.ds(q0, NBR_BQ), :], k_ref[0, pl.ds(k0, NBR_BK), :],
                bias_ref[0, variant], bias_ref[1, variant])

    def load_v(g, t):
        return v_ref[0, pl.ds(where(g, t)[1], NBR_BK), :]

    def finalize(g, tiles):
        for t, (_, l, a) in enumerate(tiles):
            o_ref[0, pl.ds(where(g, t)[0], NBR_BQ), :] = (a / l).astype(BF16)

    _attn_pipeline([(rows // (NBR_QROWS * GROUP), load_qk, load_v, finalize)],
                   lo, lo, s_scr, p_scr, m_scr)


def _nbr(qkv_nat, bias):
    B, T, _ = qkv_nat.shape
    assert T % (NBR_BQ * GROUP) == 0 and T // GRID_W >= 2 * NBR_KROWS
    seq = lambda col0: pl.BlockSpec((1, T, 128), lambda b, j: (b, 0, col0 + j))
    return pl.pallas_call(
        functools.partial(_nbr_kernel, T=T),
        grid=(B, N_PAIRS),
        in_specs=[seq(3 * N_PAIRS), seq(4 * N_PAIRS), seq(5 * N_PAIRS),
                  pl.BlockSpec((2, len(NBR_VARIANTS), NBR_BQ, NBR_BK), lambda b, j: (j, 0, 0, 0))],
        out_specs=pl.BlockSpec((1, T, 128), lambda b, j: (b, 0, j)),
        out_shape=jax.ShapeDtypeStruct((B, T, WIDTH_B), BF16),
        scratch_shapes=_pipeline_scratch(NBR_BQ, NBR_BK),
        compiler_params=_params(2),
        name="nbr",
    )(qkv_nat, qkv_nat, qkv_nat, bias)


def _nbr_bias(rpb):
    qc = jnp.arange(GRID_W)[:, None]
    kc = jnp.arange(GRID_W)[None, :]
    cidx = jnp.clip(kc - qc, -(NA_COLS - 1), NA_COLS - 1) + NA_COLS - 1
    cs = jnp.clip(qc - NA_COLS // 2, 0, GRID_W - NA_COLS)
    col_ok = (kc >= cs) & (kc < cs + NA_COLS)
    base = jnp.where(col_ok, rpb.astype(F32)[:, :, cidx], NEG_INF)
    off = jnp.array([v[0] for v in NBR_VARIANTS])[:, None, None]
    first = jnp.array([v[1] for v in NBR_VARIANTS])[:, :, None]
    q = jnp.arange(NBR_QROWS)[None, :, None]
    u = jnp.arange(NBR_KROWS)[None, None, :]
    row_ok = (u >= first) & (u < first + NA_ROWS)
    ridx = jnp.clip(u - off - q + NA_ROWS - 1, 0, 2 * NA_ROWS - 2)
    t = jnp.where(row_ok[None, :, :, :, None, None], base[:, ridx], NEG_INF)
    t = t.transpose(0, 1, 2, 4, 3, 5)
    return t.reshape(rpb.shape[0], len(NBR_VARIANTS), NBR_BQ, NBR_BK)


UP_CHUNKS = 4


def _tail_kernel(oa_ref, ob_ref, x_ref, kv_ref, ga_ref, gb_ref, g0_ref, b0_ref, wo_ref, g1_ref, b1_ref,
                 wq_ref, wxo_ref, g2_ref, b2_ref, wu_ref, wd_ref, g3_ref, b3_ref, o_ref, x2_scr, r_scr):
    s = pl.program_id(0)
    cur, prev = lax.rem(s, 2), lax.rem(s + 1, 2)

    @pl.when(s == 0)
    def _():
        x2_scr[1] = jnp.zeros(x2_scr.shape[1:], F32)
        r_scr[...] = jnp.zeros(r_scr.shape, F32)

    x2_prev = x2_scr[prev]
    xb_prev = x2_prev.astype(BF16)
    cw = D_FF // UP_CHUNKS
    hidden = []

    def up_chunk():
        c = len(hidden)
        h = jnp.maximum(jnp.dot(xb_prev, wu_ref[:, c * cw:(c + 1) * cw],
                                preferred_element_type=F32), 0.0)
        hidden.append((h * h).astype(BF16))

    up_chunk()
    y = jnp.concatenate([_rms(oa_ref[0].astype(F32), ga_ref[...]),
                         _rms(ob_ref[0].astype(F32), gb_ref[...])], axis=1).astype(BF16)
    z = jnp.dot(y, wo_ref[...], preferred_element_type=F32)
    x1 = _ln(ALPHA * _ln(x_ref[0], g0_ref[...], b0_ref[...]) + z, g1_ref[...], b1_ref[...])
    up_chunk()
    q = jnp.dot(x1.astype(BF16), wq_ref[...], preferred_element_type=F32)
    q = (q * (HEAD_DIM_X ** -0.5)).astype(BF16)
    outs = []
    for h in range(N_HEADS_X):
        k = kv_ref[0, :, h * HEAD_DIM_X:(h + 1) * HEAD_DIM_X]
        v = kv_ref[0, :, D_MODEL + h * HEAD_DIM_X:D_MODEL + (h + 1) * HEAD_DIM_X]
        sc = lax.dot_general(q[:, h * HEAD_DIM_X:(h + 1) * HEAD_DIM_X], k, NT_DIMS,
                             preferred_element_type=F32)
        pr = jnp.exp(sc - jnp.max(sc, axis=1, keepdims=True))
        l = jnp.sum(pr, axis=1, keepdims=True)
        outs.append(jnp.dot(pr.astype(BF16), v, preferred_element_type=F32) / l)
    up_chunk()
    z = jnp.dot(jnp.concatenate(outs, axis=1).astype(BF16), wxo_ref[...], preferred_element_type=F32)
    x2_scr[cur] = _ln(ALPHA * x1 + z, g2_ref[...], b2_ref[...])
    while len(hidden) < UP_CHUNKS:
        up_chunk()
    o_ref[0] = _ln(r_scr[prev], g3_ref[...], b3_ref[...])
    z = jnp.dot(jnp.concatenate(hidden, axis=1), wd_ref[...], preferred_element_type=F32)
    r_scr[cur] = ALPHA * x2_prev + z


def _tail(oa, ob, x, kv, p, tm):
    B, T, _ = x.shape
    M = kv.shape[1]
    n = T // tm
    last = B * n - 1
    vec = lambda width: _const_spec((1, width))
    mat = lambda rows, cols: _const_spec((rows, cols))

    def cur(width):
        return pl.BlockSpec((1, tm, width),
                            lambda s: (jnp.minimum(s, last) // n, jnp.minimum(s, last) % n, 0))

    return pl.pallas_call(
        _tail_kernel,
        grid=(B * n + 2,),
        in_specs=[cur(WIDTH_A), cur(WIDTH_B), cur(D_MODEL),
                  pl.BlockSpec((1, M, 2 * D_MODEL), lambda s: (jnp.minimum(s, last) // n, 0, 0)),
                  vec(WIDTH_A), vec(WIDTH_B), vec(D_MODEL), vec(D_MODEL),
                  mat(D_MODEL, D_MODEL), vec(D_MODEL), vec(D_MODEL),
                  mat(D_MODEL, D_MODEL), mat(D_MODEL, D_MODEL), vec(D_MODEL), vec(D_MODEL),
                  mat(D_MODEL, D_FF), mat(D_FF, D_MODEL), vec(D_MODEL), vec(D_MODEL)],
        out_specs=pl.BlockSpec((1, tm, D_MODEL),
                               lambda s: (jnp.maximum(s - 2, 0) // n, jnp.maximum(s - 2, 0) % n, 0)),
        out_shape=jax.ShapeDtypeStruct((B, T, D_MODEL), F32),
        scratch_shapes=[pltpu.VMEM((2, tm, D_MODEL), F32), pltpu.VMEM((2, tm, D_MODEL), F32)],
        compiler_params=_params(1),
        name="tail",
    )(oa, ob, x, kv, p["g_mix_a"], p["g_mix_b"], p["ln_in_g"], p["ln_in_b"], p["w_out"],
      p["ln1_g"], p["ln1_b"], p["w_xq"], p["w_xo"], p["ln2_g"], p["ln2_b"],
      p["w_up"], p["w_down"], p["ln3_g"], p["ln3_b"])


def _rope_tables(T):
    half = HEAD_DIM // 2
    inv = ROPE_THETA ** (-jnp.arange(half, dtype=F32) / half)
    ang = jnp.arange(T, dtype=F32)[:, None] * inv[None, :]
    cos, sin = jnp.cos(ang), jnp.sin(ang)
    return (jnp.concatenate([cos, cos, cos, cos], axis=1),
            jnp.concatenate([-sin, -sin, sin, sin], axis=1))


def _pair_rotary_layout(w_in):
    half = HEAD_DIM // 2
    idx = jnp.arange(128).reshape(4, half)[jnp.array([0, 2, 1, 3])].reshape(128)
    cols = (jnp.arange(2 * WIDTH_A // 128)[:, None] * 128 + idx[None, :]).reshape(-1)
    return jnp.concatenate([w_in[:, cols], w_in[:, 2 * WIDTH_A:]], axis=1)


def _trunk(x, mem, p, tm=256):
    B, T, _ = x.shape
    cos, sin = _rope_tables(T)
    kv = _mem_kv(mem, p["w_xkv"])
    qkv_nat, qkv_perm = _qkv(x, p["ln_in_g"], p["ln_in_b"], p["w_in"], cos, sin, 2 * tm)
    oa = _dilated(qkv_nat, qkv_perm.reshape(B, T, 3 * WIDTH_A))
    ob = _nbr(qkv_nat, p["bias"])
    return _tail(oa, ob, x, kv, p, tm)


def kernel(x_prompt, x_sample, mem_prompt, mem_sample, ln_in_g, ln_in_b, w_in, rpb, g_mix_a, g_mix_b,
           w_out, ln1_g, ln1_b, w_xq, w_xkv, w_xo, ln2_g, ln2_b, w_up, w_down, ln3_g, ln3_b):
    assert w_in.shape[0] == 1, "single-layer trunk"
    row = lambda v: v.reshape(1, -1).astype(F32)
    p = dict(
        ln_in_g=row(ln_in_g), ln_in_b=row(ln_in_b),
        w_in=_pair_rotary_layout(w_in[0]).astype(BF16), bias=_nbr_bias(rpb[0]),
        g_mix_a=row(g_mix_a[0]), g_mix_b=row(g_mix_b[0]), w_out=w_out[0].astype(BF16),
        ln1_g=row(ln1_g[0]), ln1_b=row(ln1_b[0]),
        w_xq=w_xq[0].astype(BF16), w_xkv=w_xkv[0].astype(BF16), w_xo=w_xo[0].astype(BF16),
        ln2_g=row(ln2_g[0]), ln2_b=row(ln2_b[0]),
        w_up=w_up[0].astype(BF16), w_down=w_down[0].astype(BF16),
        ln3_g=row(ln3_g[0]), ln3_b=row(ln3_b[0]),
    )
    return _trunk(x_prompt, mem_prompt, p), _trunk(x_sample, mem_sample, p)
```

```python
import functools

import jax
import jax.numpy as jnp
from jax import lax
from jax.experimental import pallas as pl
from jax.experimental.pallas import tpu as pltpu

F32 = jnp.float32
BF16 = jnp.bfloat16

D_MODEL = 1024
HEAD_DIM = 64
WIDTH_A = 512
WIDTH_B = 512
N_PAIRS = WIDTH_A // 128
GRID_W = 64
NA_ROWS = 8
NA_COLS = 16
N_HEADS_X = 4
HEAD_DIM_X = 256
D_FF = 4096
ROPE_THETA = 10000.0
LN_EPS = 1e-5
ALPHA = 2.0 ** 0.25
NEG_INF = -1e30
LOG2E = 1.4426950408889634
N_SIDE = 64
MAX_DIL = 16

VMEM_LIMIT = 56 * 1024 * 1024

NT_DIMS = (((1,), (1,)), ((), ()))


def _ln(x, g, b):
    mu = jnp.mean(x, axis=-1, keepdims=True)
    xc = x - mu
    var = jnp.mean(xc * xc, axis=-1, keepdims=True)
    return xc * lax.rsqrt(var + LN_EPS) * g + b


def _rms(x, g):
    return x * lax.rsqrt(jnp.mean(x * x, axis=-1, keepdims=True) + LN_EPS) * g


def _params(n_axes):
    return pltpu.CompilerParams(dimension_semantics=("arbitrary",) * n_axes,
                                vmem_limit_bytes=VMEM_LIMIT)


def _const_spec(shape):
    nd = len(shape)
    return pl.BlockSpec(shape, lambda *_: (0,) * nd, pipeline_mode=pl.Buffered(1))


def _mem_kv_kernel(mem_ref, w_ref, o_ref):
    o_ref[0] = jnp.dot(mem_ref[0].astype(BF16), w_ref[...],
                       preferred_element_type=F32).astype(BF16)


def _mem_kv(mem, w_xkv):
    B, M, _ = mem.shape
    return pl.pallas_call(
        _mem_kv_kernel,
        grid=(B,),
        in_specs=[pl.BlockSpec((1, M, D_MODEL), lambda b: (b, 0, 0)),
                  _const_spec((D_MODEL, 2 * D_MODEL))],
        out_specs=pl.BlockSpec((1, M, 2 * D_MODEL), lambda b: (b, 0, 0)),
        out_shape=jax.ShapeDtypeStruct((B, M, 2 * D_MODEL), BF16),
        compiler_params=_params(1),
        name="mem_kv",
    )(mem, w_xkv)


PERM_SUB = 256


def _qkv_kernel(x_ref, g_ref, b_ref, w_ref, cos_ref, sin_ref, nat_ref, perm_ref, *, tm):
    xb = _ln(x_ref[0], g_ref[...], b_ref[...]).astype(BF16)
    cos = jnp.concatenate([cos_ref[...]] * 4, axis=1)
    sin = jnp.concatenate([sin_ref[...]] * 4, axis=1)
    row = lax.broadcasted_iota(jnp.int32, (PERM_SUB, PERM_SUB), 0)
    col = lax.broadcasted_iota(jnp.int32, (PERM_SUB, PERM_SUB), 1)
    rows_per = PERM_SUB // MAX_DIL
    perm_mat = jnp.where(col == MAX_DIL * (row % rows_per) + row // rows_per, 1.0, 0.0).astype(BF16)
    for part in (0, 3, 1, 4, 2, 5):
        cols = slice(part * 512, (part + 1) * 512)
        r = jnp.dot(xb, w_ref[:, cols], preferred_element_type=F32)
        if part in (0, 1):
            swapped = jnp.concatenate([pltpu.roll(r[:, c:c + 128], 64, 1)
                                       for c in range(0, WIDTH_A, 128)], axis=1)
            r = r * cos + swapped * sin
        if part in (0, 3):
            r = r * (HEAD_DIM ** -0.5 * LOG2E)
        rb = r.astype(BF16)
        nat_ref[0, :, cols] = rb
        if part < 3:
            for g in range(tm // PERM_SUB):
                moved = jnp.dot(perm_mat, rb[g * PERM_SUB:(g + 1) * PERM_SUB],
                                preferred_element_type=F32).astype(BF16)
                for c in range(MAX_DIL):
                    perm_ref[0, c, g * rows_per:(g + 1) * rows_per, cols] = (
                        moved[c * rows_per:(c + 1) * rows_per])


def _qkv(x, ln_g, ln_b, w_in, cos, sin, tm):
    B, T, _ = x.shape
    L16 = T // MAX_DIL
    assert tm % PERM_SUB == 0
    return pl.pallas_call(
        functools.partial(_qkv_kernel, tm=tm),
        grid=(B, T // tm),
        in_specs=[pl.BlockSpec((1, tm, D_MODEL), lambda b, i: (b, i, 0)),
                  _const_spec((1, D_MODEL)), _const_spec((1, D_MODEL)),
                  _const_spec((D_MODEL, 3 * D_MODEL)),
                  pl.BlockSpec((tm, 128), lambda b, i: (i, 0)),
                  pl.BlockSpec((tm, 128), lambda b, i: (i, 0))],
        out_specs=[pl.BlockSpec((1, tm, 3 * D_MODEL), lambda b, i: (b, i, 0)),
                   pl.BlockSpec((1, MAX_DIL, tm // MAX_DIL, 3 * WIDTH_A), lambda b, i: (b, 0, i, 0))],
        out_shape=[jax.ShapeDtypeStruct((B, T, 3 * D_MODEL), BF16),
                   jax.ShapeDtypeStruct((B, MAX_DIL, L16, 3 * WIDTH_A), BF16)],
        compiler_params=_params(2),
        name="qkv",
    )(x, ln_g, ln_b, w_in, cos, sin)


GROUP = 4


def _attn_pipeline(segments, lo_q, lo, s_scr, p_scr, m_scr):
    bq, bk = s_scr.shape[-2:]

    def stage_a(seg, g, slot):
        for t in range(GROUP):
            q2, k2, b_lo, b_hi = seg[1](g, t)
            zero = jnp.zeros_like(q2)
            for h, (q1, bias) in enumerate(((jnp.where(lo_q, q2, zero), b_lo),
                                            (jnp.where(lo_q, zero, q2), b_hi))):
                s = lax.dot_general(q1, k2, NT_DIMS, preferred_element_type=F32) + bias
                s_scr[slot, t, h] = s
                m_scr[slot, t, h] = jnp.broadcast_to(jnp.max(s, axis=1, keepdims=True), (bq, 128))

    def stage_b(slot):
        for t in range(GROUP):
            for h in range(2):
                m = jnp.concatenate([m_scr[slot, t, h]] * (bk // 128), axis=1)
                p_scr[slot, t, h] = jnp.exp2((s_scr[slot, t, h] - m).astype(BF16))

    def stage_c(seg, g, slot):
        tiles = []
        for t in range(GROUP):
            v2 = seg[2](g, t)
            v_ones = jnp.concatenate([v2, jnp.ones_like(v2)], axis=1)
            pv0 = jnp.dot(p_scr[slot, t, 0], v_ones, preferred_element_type=F32)
            pv1 = jnp.dot(p_scr[slot, t, 1], v_ones, preferred_element_type=F32)
            tiles.append((jnp.where(lo, m_scr[slot, t, 0], m_scr[slot, t, 1]),
                          jnp.where(lo, pv0[:, 128:], pv1[:, 128:]),
                          jnp.where(lo, pv0[:, :128], pv1[:, :128])))
        seg[3](g, tiles)

    stage_a(segments[0], 0, 0)
    stage_b(0)
    stage_a(segments[0], 1, 1)
    for k, seg in enumerate(segments):
        n = seg[0]
        assert n >= 2 and n % 2 == 0

        def body(j, carry, seg=seg):
            for slot in range(2):
                stage_c(seg, 2 * j + slot, slot)
                stage_b(1 - slot)
                stage_a(seg, 2 * j + slot + 2, slot)
            return carry

        lax.fori_loop(0, (n - 2) // 2, body, 0)
        nxt = segments[k + 1] if k + 1 < len(segments) else None
        stage_c(seg, n - 2, 0)
        stage_b(1)
        if nxt:
            stage_a(nxt, 0, 0)
        stage_c(seg, n - 1, 1)
        if nxt:
            stage_b(0)
            stage_a(nxt, 1, 1)


def _pipeline_scratch(bq, bk):
    return [pltpu.VMEM((2, GROUP, 2, bq, bk), F32), pltpu.VMEM((2, GROUP, 2, bq, bk), BF16),
            pltpu.VMEM((2, GROUP, 2, bq, 128), F32)]


def _merge(m_r, l_r, a_r, m_t, l_t, a_t):
    m_n = jnp.maximum(m_r, m_t)
    e_r = jnp.exp2(m_r - m_n)
    e_t = jnp.exp2(m_t - m_n)
    return m_n, l_r * e_r + l_t * e_t, a_r * e_r + a_t * e_t


def _dilated_kernel(qn_ref, kn_ref, vn_ref, qp_ref, kp_ref, vp_ref, o_ref,
                    m_ref, l_ref, a_ref, tmp_ref, mask_ref, s_scr, p_scr, m_scr, *, T):
    BQ, BK = 128, 256
    L16 = T // MAX_DIL
    row = lax.broadcasted_iota(jnp.int32, (BQ, BK), 0)
    col = lax.broadcasted_iota(jnp.int32, (BQ, BK), 1)
    band = row - col
    band4 = 4 * ((row % 32) - (col % 64)) + (row // 32 - col // 64)
    for n in range(3):
        mask_ref[n] = jnp.where(jnp.abs(band + N_SIDE * n) <= N_SIDE, 0.0, NEG_INF)
        mask_ref[3 + n] = jnp.where(jnp.abs(band4 + N_SIDE * n) <= N_SIDE, 0.0, NEG_INF)
    lane = lax.broadcasted_iota(jnp.int32, (BQ, 128), 1)
    lo = lane < HEAD_DIM
    lo_q = lane % HEAD_DIM < HEAD_DIM // 2
    n_groups = T // (BQ * GROUP)

    nib = L16 // BQ

    def rows16(g, t):
        tile = jnp.asarray(g * GROUP + t, jnp.int32)
        c = lax.div(tile, nib)
        ib = lax.rem(tile, nib)
        kst = jnp.clip(ib * BQ - N_SIDE, 0, L16 - BK)
        return (pl.multiple_of(c * L16 + ib * BQ, BQ), pl.multiple_of(c * L16 + kst, N_SIDE),
                lax.div(ib * BQ - kst, N_SIDE))

    def load_qk16(g, t):
        q0, k0, case = rows16(g, t)
        mask = mask_ref[case]
        return qp_ref[0, pl.ds(q0, BQ), :], kp_ref[0, pl.ds(k0, BK), :], mask, mask

    def load_v16(g, t):
        return vp_ref[0, pl.ds(rows16(g, t)[1], BK), :]

    def finalize16(g, tiles):
        for t, (m, l, a) in enumerate(tiles):
            q0 = rows16(g, t)[0]
            m_ref[pl.ds(q0, BQ), :] = m
            l_ref[pl.ds(q0, BQ), :] = l
            a_ref[pl.ds(q0, BQ), :] = a

    nb4 = L16 // 32

    def rows4(g, t):
        tile = jnp.asarray(g * GROUP + t, jnp.int32)
        r4 = lax.div(tile, nb4)
        i0 = lax.rem(tile, nb4) * 32
        kst = jnp.clip(i0 - 16, 0, L16 - 64)
        return ([pl.multiple_of((r4 + 4 * w) * L16 + i0, 32) for w in range(4)],
                [pl.multiple_of((r4 + 4 * w) * L16 + kst, 16) for w in range(4)],
                3 + lax.div(i0 - kst, 16))

    def load_qk4(g, t):
        qrows, krows, case = rows4(g, t)
        mask = mask_ref[case]
        return (jnp.concatenate([qp_ref[0, pl.ds(r, 32), :] for r in qrows], axis=0),
                jnp.concatenate([kp_ref[0, pl.ds(r, 64), :] for r in krows], axis=0), mask, mask)

    def load_v4(g, t):
        return jnp.concatenate([vp_ref[0, pl.ds(r, 64), :] for r in rows4(g, t)[1]], axis=0)

    def finalize4(g, tiles):
        merged = []
        for t, tile in enumerate(tiles):
            qrows = rows4(g, t)[0]
            state = [jnp.concatenate([ref[pl.ds(r, 32), :] for r in qrows], axis=0)
                     for ref in (m_ref, l_ref, a_ref)]
            merged.append((qrows, _merge(*state, *tile)))
        for qrows, new in merged:
            for ref, val in zip((m_ref, l_ref, a_ref), new):
                for w, r in enumerate(qrows):
                    ref[pl.ds(r, 32), :] = val[w * 32:(w + 1) * 32]

    def rows1(g, t):
        t0 = pl.multiple_of(jnp.asarray(g * GROUP + t, jnp.int32) * BQ, BQ)
        kst = pl.multiple_of(jnp.clip(t0 - N_SIDE, 0, T - BK), N_SIDE)
        return t0, kst, lax.div(t0 - kst, N_SIDE)

    def load_qk1(g, t):
        t0, kst, case = rows1(g, t)
        mask = mask_ref[case]
        return qn_ref[0, pl.ds(t0, BQ), :], kn_ref[0, pl.ds(kst, BK), :], mask, mask

    def load_v1(g, t):
        return vn_ref[0, pl.ds(rows1(g, t)[1], BK), :]

    def finalize1(g, tiles):
        n_i = BQ // MAX_DIL
        for t, tile in enumerate(tiles):
            t0 = rows1(g, t)[0]
            i0 = lax.div(t0, MAX_DIL)
            for c in range(MAX_DIL):
                src = pl.ds(pl.multiple_of(c * L16 + i0, n_i), n_i)
                dst = pl.ds(c, n_i, stride=MAX_DIL)
                for n, ref in enumerate((m_ref, l_ref, a_ref)):
                    tmp_ref[t, n, dst, :] = ref[src, :]
            _, l_n, a_n = _merge(tmp_ref[t, 0], tmp_ref[t, 1], tmp_ref[t, 2], *tile)
            o_ref[0, pl.ds(t0, BQ), :] = (a_n / l_n).astype(BF16)

    _attn_pipeline([(n_groups, load_qk16, load_v16, finalize16),
                    (n_groups, load_qk4, load_v4, finalize4),
                    (n_groups, load_qk1, load_v1, finalize1)], lo_q, lo, s_scr, p_scr, m_scr)


def _dilated(qkv_nat, qkv_perm):
    B, T, _ = qkv_nat.shape
    assert T % (MAX_DIL * 128) == 0 and T % (128 * GROUP) == 0 and T // MAX_DIL >= 256
    seq = lambda col0: pl.BlockSpec((1, T, 128), lambda b, j: (b, 0, col0 + j))
    return pl.pallas_call(
        functools.partial(_dilated_kernel, T=T),
        grid=(B, N_PAIRS),
        in_specs=[seq(0), seq(N_PAIRS), seq(2 * N_PAIRS), seq(0), seq(N_PAIRS), seq(2 * N_PAIRS)],
        out_specs=pl.BlockSpec((1, T, 128), lambda b, j: (b, 0, j)),
        out_shape=jax.ShapeDtypeStruct((B, T, WIDTH_A), BF16),
        scratch_shapes=[pltpu.VMEM((T, 128), F32), pltpu.VMEM((T, 128), F32),
                        pltpu.VMEM((T, 128), F32), pltpu.VMEM((GROUP, 3, 128, 128), F32),
                        pltpu.VMEM((6, 128, 256), F32)] + _pipeline_scratch(128, 256),
        compiler_params=_params(2),
        name="dilated",
    )(qkv_nat, qkv_nat, qkv_nat, qkv_perm, qkv_perm, qkv_perm)


NBR_QROWS = 2
NBR_KROWS = NA_ROWS + 2
NBR_BQ = NBR_QROWS * GRID_W
NBR_BK = NBR_KROWS * GRID_W
NBR_VARIANTS = ((0, (0, 0)), (2, (0, 0)), (4, (0, 1)), (6, (2, 2)), (8, (2, 2)))


def _nbr_kernel(q_ref, k_ref, v_ref, bias_ref, o_ref, s_scr, p_scr, m_scr, *, T):
    rows = T // GRID_W
    lo = lax.broadcasted_iota(jnp.int32, (NBR_BQ, 128), 1) < HEAD_DIM

    def where(g, t):
        r = NBR_QROWS * jnp.asarray(g * GROUP + t, jnp.int32)
        first_key_row = jnp.clip(r - NA_ROWS // 2, 0, rows - NBR_KROWS)
        variant = lax.div(r - first_key_row, 2)
        return (pl.multiple_of(r * GRID_W, NBR_BQ), pl.multiple_of(first_key_row * GRID_W, GRID_W),
                variant)

    def load_qk(g, t):
        q0, k0, variant = where(g, t)
        return (q_ref[0, pl.ds(q0, NBR_BQ), :], k_ref[0, pl.ds(k0, NBR_BK), :],
                bias_ref[0, variant], bias_ref[1, variant])

    def load_v(g, t):
        return v_ref[0, pl.ds(where(g, t)[1], NBR_BK), :]

    def finalize(g, tiles):
        for t, (_, l, a) in enumerate(tiles):
            o_ref[0, pl.ds(where(g, t)[0], NBR_BQ), :] = (a / l).astype(BF16)

    _attn_pipeline([(rows // (NBR_QROWS * GROUP), load_qk, load_v, finalize)],
                   lo, lo, s_scr, p_scr, m_scr)


def _nbr(qkv_nat, bias):
    B, T, _ = qkv_nat.shape
    assert T % (NBR_BQ * GROUP) == 0 and T // GRID_W >= 2 * NBR_KROWS
    seq = lambda col0: pl.BlockSpec((1, T, 128), lambda b, j: (b, 0, col0 + j))
    return pl.pallas_call(
        functools.partial(_nbr_kernel, T=T),
        grid=(B, N_PAIRS),
        in_specs=[seq(3 * N_PAIRS), seq(4 * N_PAIRS), seq(5 * N_PAIRS),
                  pl.BlockSpec((2, len(NBR_VARIANTS), NBR_BQ, NBR_BK), lambda b, j: (j, 0, 0, 0))],
        out_specs=pl.BlockSpec((1, T, 128), lambda b, j: (b, 0, j)),
        out_shape=jax.ShapeDtypeStruct((B, T, WIDTH_B), BF16),
        scratch_shapes=_pipeline_scratch(NBR_BQ, NBR_BK),
        compiler_params=_params(2),
        name="nbr",
    )(qkv_nat, qkv_nat, qkv_nat, bias)


def _nbr_bias(rpb):
    qc = jnp.arange(GRID_W)[:, None]
    kc = jnp.arange(GRID_W)[None, :]
    cidx = jnp.clip(kc - qc, -(NA_COLS - 1), NA_COLS - 1) + NA_COLS - 1
    cs = jnp.clip(qc - NA_COLS // 2, 0, GRID_W - NA_COLS)
    col_ok = (kc >= cs) & (kc < cs + NA_COLS)
    base = jnp.where(col_ok, LOG2E * rpb.astype(F32)[:, :, cidx], NEG_INF)
    off = jnp.array([v[0] for v in NBR_VARIANTS])[:, None, None]
    first = jnp.array([v[1] for v in NBR_VARIANTS])[:, :, None]
    q = jnp.arange(NBR_QROWS)[None, :, None]
    u = jnp.arange(NBR_KROWS)[None, None, :]
    row_ok = (u >= first) & (u < first + NA_ROWS)
    ridx = jnp.clip(u - off - q + NA_ROWS - 1, 0, 2 * NA_ROWS - 2)
    t = jnp.where(row_ok[None, :, :, :, None, None], base[:, ridx], NEG_INF)
    t = t.transpose(0, 1, 2, 4, 3, 5)
    return t.reshape(rpb.shape[0], len(NBR_VARIANTS), NBR_BQ, NBR_BK)


UP_CHUNKS = 4


def _tail_kernel(oa_ref, ob_ref, x_ref, kv_ref, ga_ref, gb_ref, g0_ref, b0_ref, wo_ref, g1_ref, b1_ref,
                 wq_ref, wxo_ref, g2_ref, b2_ref, wu_ref, wd_ref, g3_ref, b3_ref, o_ref, x2_scr, r_scr):
    s = pl.program_id(0)
    cur, prev = lax.rem(s, 2), lax.rem(s + 1, 2)

    @pl.when(s == 0)
    def _():
        x2_scr[1] = jnp.zeros(x2_scr.shape[1:], F32)
        r_scr[...] = jnp.zeros(r_scr.shape, F32)

    x2_prev = x2_scr[prev]
    xb_prev = x2_prev.astype(BF16)
    cw = D_FF // UP_CHUNKS
    hidden = []

    def up_chunk():
        c = len(hidden)
        h = jnp.maximum(jnp.dot(xb_prev, wu_ref[:, c * cw:(c + 1) * cw],
                                preferred_element_type=F32), 0.0)
        hidden.append((h * h).astype(BF16))

    up_chunk()
    y = jnp.concatenate([_rms(oa_ref[0].astype(F32), ga_ref[...]),
                         _rms(ob_ref[0].astype(F32), gb_ref[...])], axis=1).astype(BF16)
    z = jnp.dot(y, wo_ref[...], preferred_element_type=F32)
    x1 = _ln(ALPHA * _ln(x_ref[0], g0_ref[...], b0_ref[...]) + z, g1_ref[...], b1_ref[...])
    up_chunk()
    q = jnp.dot(x1.astype(BF16), wq_ref[...], preferred_element_type=F32)
    q = (q * (HEAD_DIM_X ** -0.5)).astype(BF16)
    outs = []
    for h in range(N_HEADS_X):
        k = kv_ref[0, :, h * HEAD_DIM_X:(h + 1) * HEAD_DIM_X]
        v = kv_ref[0, :, D_MODEL + h * HEAD_DIM_X:D_MODEL + (h + 1) * HEAD_DIM_X]
        sc = lax.dot_general(q[:, h * HEAD_DIM_X:(h + 1) * HEAD_DIM_X], k, NT_DIMS,
                             preferred_element_type=F32)
        pr = jnp.exp(sc - jnp.max(sc, axis=1, keepdims=True))
        l = jnp.sum(pr, axis=1, keepdims=True)
        outs.append(jnp.dot(pr.astype(BF16), v, preferred_element_type=F32) / l)
    up_chunk()
    z = jnp.dot(jnp.concatenate(outs, axis=1).astype(BF16), wxo_ref[...], preferred_element_type=F32)
    x2_scr[cur] = _ln(ALPHA * x1 + z, g2_ref[...], b2_ref[...])
    while len(hidden) < UP_CHUNKS:
        up_chunk()
    o_ref[0] = _ln(r_scr[prev], g3_ref[...], b3_ref[...])
    z = jnp.dot(jnp.concatenate(hidden, axis=1), wd_ref[...], preferred_element_type=F32)
    r_scr[cur] = ALPHA * x2_prev + z


def _tail(oa, ob, x, kv, p, tm):
    B, T, _ = x.shape
    M = kv.shape[1]
    n = T // tm
    last = B * n - 1
    vec = lambda width: _const_spec((1, width))
    mat = lambda rows, cols: _const_spec((rows, cols))

    def cur(width):
        return pl.BlockSpec((1, tm, width),
                            lambda s: (jnp.minimum(s, last) // n, jnp.minimum(s, last) % n, 0))

    return pl.pallas_call(
        _tail_kernel,
        grid=(B * n + 2,),
        in_specs=[cur(WIDTH_A), cur(WIDTH_B), cur(D_MODEL),
                  pl.BlockSpec((1, M, 2 * D_MODEL), lambda s: (jnp.minimum(s, last) // n, 0, 0)),
                  vec(WIDTH_A), vec(WIDTH_B), vec(D_MODEL), vec(D_MODEL),
                  mat(D_MODEL, D_MODEL), vec(D_MODEL), vec(D_MODEL),
                  mat(D_MODEL, D_MODEL), mat(D_MODEL, D_MODEL), vec(D_MODEL), vec(D_MODEL),
                  mat(D_MODEL, D_FF), mat(D_FF, D_MODEL), vec(D_MODEL), vec(D_MODEL)],
        out_specs=pl.BlockSpec((1, tm, D_MODEL),
                               lambda s: (jnp.maximum(s - 2, 0) // n, jnp.maximum(s - 2, 0) % n, 0)),
        out_shape=jax.ShapeDtypeStruct((B, T, D_MODEL), F32),
        scratch_shapes=[pltpu.VMEM((2, tm, D_MODEL), F32), pltpu.VMEM((2, tm, D_MODEL), F32)],
        compiler_params=_params(1),
        name="tail",
    )(oa, ob, x, kv, p["g_mix_a"], p["g_mix_b"], p["ln_in_g"], p["ln_in_b"], p["w_out"],
      p["ln1_g"], p["ln1_b"], p["w_xq"], p["w_xo"], p["ln2_g"], p["ln2_b"],
      p["w_up"], p["w_down"], p["ln3_g"], p["ln3_b"])


def _rope_tables(T):
    half = HEAD_DIM // 2
    inv = ROPE_THETA ** (-jnp.arange(half, dtype=F32) / half)
    ang = jnp.arange(T, dtype=F32)[:, None] * inv[None, :]
    cos, sin = jnp.cos(ang), jnp.sin(ang)
    return (jnp.concatenate([cos, cos, cos, cos], axis=1),
            jnp.concatenate([-sin, -sin, sin, sin], axis=1))


def _pair_rotary_layout(w_in):
    half = HEAD_DIM // 2
    idx = jnp.arange(128).reshape(4, half)[jnp.array([0, 2, 1, 3])].reshape(128)
    cols = (jnp.arange(2 * WIDTH_A // 128)[:, None] * 128 + idx[None, :]).reshape(-1)
    return jnp.concatenate([w_in[:, cols], w_in[:, 2 * WIDTH_A:]], axis=1)


def _trunk(x, mem, p, tm=256):
    B, T, _ = x.shape
    cos, sin = _rope_tables(T)
    kv = _mem_kv(mem, p["w_xkv"])
    qkv_nat, qkv_perm = _qkv(x, p["ln_in_g"], p["ln_in_b"], p["w_in"], cos, sin, 2 * tm)
    oa = _dilated(qkv_nat, qkv_perm.reshape(B, T, 3 * WIDTH_A))
    ob = _nbr(qkv_nat, p["bias"])
    return _tail(oa, ob, x, kv, p, tm)


def kernel(x_prompt, x_sample, mem_prompt, mem_sample, ln_in_g, ln_in_b, w_in, rpb, g_mix_a, g_mix_b,
           w_out, ln1_g, ln1_b, w_xq, w_xkv, w_xo, ln2_g, ln2_b, w_up, w_down, ln3_g, ln3_b):
    assert w_in.shape[0] == 1, "single-layer trunk"
    row = lambda v: v.reshape(1, -1).astype(F32)
    p = dict(
        ln_in_g=row(ln_in_g), ln_in_b=row(ln_in_b),
        w_in=_pair_rotary_layout(w_in[0]).astype(BF16), bias=_nbr_bias(rpb[0]),
        g_mix_a=row(g_mix_a[0]), g_mix_b=row(g_mix_b[0]), w_out=w_out[0].astype(BF16),
        ln1_g=row(ln1_g[0]), ln1_b=row(ln1_b[0]),
        w_xq=w_xq[0].astype(BF16), w_xkv=w_xkv[0].astype(BF16), w_xo=w_xo[0].astype(BF16),
        ln2_g=row(ln2_g[0]), ln2_b=row(ln2_b[0]),
        w_up=w_up[0].astype(BF16), w_down=w_down[0].astype(BF16),
        ln3_g=row(ln3_g[0]), ln3_b=row(ln3_b[0]),
    )
    return _trunk(x_prompt, mem_prompt, p), _trunk(x_sample, mem_sample, p)
```

```python
import functools

import jax
import jax.numpy as jnp
from jax import lax
from jax.experimental import pallas as pl
from jax.experimental.pallas import tpu as pltpu

F32 = jnp.float32
BF16 = jnp.bfloat16

D_MODEL = 1024
HEAD_DIM = 64
WIDTH_A = 512
WIDTH_B = 512
N_PAIRS = WIDTH_A // 128
GRID_W = 64
NA_ROWS = 8
NA_COLS = 16
N_HEADS_X = 4
HEAD_DIM_X = 256
D_FF = 4096
ROPE_THETA = 10000.0
LN_EPS = 1e-5
ALPHA = 2.0 ** 0.25
NEG_INF = -1e30
LOG2E = 1.4426950408889634
N_SIDE = 64
MAX_DIL = 16

VMEM_LIMIT = 56 * 1024 * 1024

NT_DIMS = (((1,), (1,)), ((), ()))


def _ln(x, g, b):
    mu = jnp.mean(x, axis=-1, keepdims=True)
    xc = x - mu
    var = jnp.mean(xc * xc, axis=-1, keepdims=True)
    return xc * lax.rsqrt(var + LN_EPS) * g + b


def _rms(x, g):
    return x * lax.rsqrt(jnp.mean(x * x, axis=-1, keepdims=True) + LN_EPS) * g


def _params(n_axes):
    return pltpu.CompilerParams(dimension_semantics=("arbitrary",) * n_axes,
                                vmem_limit_bytes=VMEM_LIMIT)


def _const_spec(shape):
    nd = len(shape)
    return pl.BlockSpec(shape, lambda *_: (0,) * nd, pipeline_mode=pl.Buffered(1))


def _mem_kv_kernel(mem_ref, w_ref, o_ref):
    o_ref[0] = jnp.dot(mem_ref[0].astype(BF16), w_ref[...],
                       preferred_element_type=F32).astype(BF16)


def _mem_kv(mem, w_xkv):
    B, M, _ = mem.shape
    return pl.pallas_call(
        _mem_kv_kernel,
        grid=(B,),
        in_specs=[pl.BlockSpec((1, M, D_MODEL), lambda b: (b, 0, 0)),
                  _const_spec((D_MODEL, 2 * D_MODEL))],
        out_specs=pl.BlockSpec((1, M, 2 * D_MODEL), lambda b: (b, 0, 0)),
        out_shape=jax.ShapeDtypeStruct((B, M, 2 * D_MODEL), BF16),
        compiler_params=_params(1),
        name="mem_kv",
    )(mem, w_xkv)


PERM_SUB = 256


def _qkv_kernel(x_ref, g_ref, b_ref, w_ref, cos_ref, sin_ref, nat_ref, perm_ref, *, tm):
    xb = _ln(x_ref[0], g_ref[...], b_ref[...]).astype(BF16)
    cos = jnp.concatenate([cos_ref[...]] * 4, axis=1)
    sin = jnp.concatenate([sin_ref[...]] * 4, axis=1)
    row = lax.broadcasted_iota(jnp.int32, (PERM_SUB, PERM_SUB), 0)
    col = lax.broadcasted_iota(jnp.int32, (PERM_SUB, PERM_SUB), 1)
    rows_per = PERM_SUB // MAX_DIL
    perm_mat = jnp.where(col == MAX_DIL * (row % rows_per) + row // rows_per, 1.0, 0.0).astype(BF16)
    for part in (0, 3, 1, 4, 2, 5):
        cols = slice(part * 512, (part + 1) * 512)
        r = jnp.dot(xb, w_ref[:, cols], preferred_element_type=F32)
        if part in (0, 1):
            swapped = jnp.concatenate([pltpu.roll(r[:, c:c + 128], 64, 1)
                                       for c in range(0, WIDTH_A, 128)], axis=1)
            r = r * cos + swapped * sin
        if part in (0, 3):
            r = r * (HEAD_DIM ** -0.5 * LOG2E)
        rb = r.astype(BF16)
        nat_ref[0, :, cols] = rb
        if part < 3:
            for g in range(tm // PERM_SUB):
                moved = jnp.dot(perm_mat, rb[g * PERM_SUB:(g + 1) * PERM_SUB],
                                preferred_element_type=F32).astype(BF16)
                for c in range(MAX_DIL):
                    perm_ref[0, c, g * rows_per:(g + 1) * rows_per, cols] = (
                        moved[c * rows_per:(c + 1) * rows_per])


def _qkv(x, ln_g, ln_b, w_in, cos, sin, tm):
    B, T, _ = x.shape
    L16 = T // MAX_DIL
    assert tm % PERM_SUB == 0
    return pl.pallas_call(
        functools.partial(_qkv_kernel, tm=tm),
        grid=(B, T // tm),
        in_specs=[pl.BlockSpec((1, tm, D_MODEL), lambda b, i: (b, i, 0)),
                  _const_spec((1, D_MODEL)), _const_spec((1, D_MODEL)),
                  _const_spec((D_MODEL, 3 * D_MODEL)),
                  pl.BlockSpec((tm, 128), lambda b, i: (i, 0)),
                  pl.BlockSpec((tm, 128), lambda b, i: (i, 0))],
        out_specs=[pl.BlockSpec((1, tm, 3 * D_MODEL), lambda b, i: (b, i, 0)),
                   pl.BlockSpec((1, MAX_DIL, tm // MAX_DIL, 3 * WIDTH_A), lambda b, i: (b, 0, i, 0))],
        out_shape=[jax.ShapeDtypeStruct((B, T, 3 * D_MODEL), BF16),
                   jax.ShapeDtypeStruct((B, MAX_DIL, L16, 3 * WIDTH_A), BF16)],
        compiler_params=_params(2),
        name="qkv",
    )(x, ln_g, ln_b, w_in, cos, sin)


GROUP = 8


def _attn_pipeline(segments, lo_q, lo, s_scr, p_scr, m_scr):
    bq, bk = s_scr.shape[-2:]

    def stage_a(seg, g, slot):
        for t in range(GROUP):
            q2, k2, b_lo, b_hi = seg[1](g, t)
            zero = jnp.zeros_like(q2)
            for h, (q1, bias) in enumerate(((jnp.where(lo_q, q2, zero), b_lo),
                                            (jnp.where(lo_q, zero, q2), b_hi))):
                s = lax.dot_general(q1, k2, NT_DIMS, preferred_element_type=F32) + bias
                s_scr[slot, t, h] = s
                m_scr[slot, t, h] = jnp.broadcast_to(jnp.max(s, axis=1, keepdims=True), (bq, 128))

    def stage_b(slot):
        for t in range(GROUP):
            for h in range(2):
                m = jnp.concatenate([m_scr[slot, t, h]] * (bk // 128), axis=1)
                p_scr[slot, t, h] = jnp.exp2((s_scr[slot, t, h] - m).astype(BF16))

    def stage_c(seg, g, slot):
        tiles = []
        for t in range(GROUP):
            v2 = seg[2](g, t)
            v_ones = jnp.concatenate([v2, jnp.ones_like(v2)], axis=1)
            pv0 = jnp.dot(p_scr[slot, t, 0], v_ones, preferred_element_type=F32)
            pv1 = jnp.dot(p_scr[slot, t, 1], v_ones, preferred_element_type=F32)
            tiles.append((jnp.where(lo, m_scr[slot, t, 0], m_scr[slot, t, 1]),
                          jnp.where(lo, pv0[:, 128:], pv1[:, 128:]),
                          jnp.where(lo, pv0[:, :128], pv1[:, :128])))
        seg[3](g, tiles)

    stage_a(segments[0], 0, 0)
    stage_b(0)
    stage_a(segments[0], 1, 1)
    for k, seg in enumerate(segments):
        n = seg[0]
        assert n >= 2 and n % 2 == 0

        def body(j, carry, seg=seg):
            for slot in range(2):
                stage_c(seg, 2 * j + slot, slot)
                stage_b(1 - slot)
                stage_a(seg, 2 * j + slot + 2, slot)
            return carry

        lax.fori_loop(0, (n - 2) // 2, body, 0)
        nxt = segments[k + 1] if k + 1 < len(segments) else None
        stage_c(seg, n - 2, 0)
        stage_b(1)
        if nxt:
            stage_a(nxt, 0, 0)
        stage_c(seg, n - 1, 1)
        if nxt:
            stage_b(0)
            stage_a(nxt, 1, 1)


def _pipeline_scratch(bq, bk):
    return [pltpu.VMEM((2, GROUP, 2, bq, bk), F32), pltpu.VMEM((2, GROUP, 2, bq, bk), BF16),
            pltpu.VMEM((2, GROUP, 2, bq, 128), F32)]


def _merge(m_r, l_r, a_r, m_t, l_t, a_t):
    m_n = jnp.maximum(m_r, m_t)
    e_r = jnp.exp2(m_r - m_n)
    e_t = jnp.exp2(m_t - m_n)
    return m_n, l_r * e_r + l_t * e_t, a_r * e_r + a_t * e_t


def _dilated_kernel(qn_ref, kn_ref, vn_ref, qp_ref, kp_ref, vp_ref, o_ref,
                    m_ref, l_ref, a_ref, tmp_ref, mask_ref, s_scr, p_scr, m_scr, *, T):
    BQ, BK = 128, 256
    L16 = T // MAX_DIL
    row = lax.broadcasted_iota(jnp.int32, (BQ, BK), 0)
    col = lax.broadcasted_iota(jnp.int32, (BQ, BK), 1)
    band = row - col
    band4 = 4 * ((row % 32) - (col % 64)) + (row // 32 - col // 64)
    for n in range(3):
        mask_ref[n] = jnp.where(jnp.abs(band + N_SIDE * n) <= N_SIDE, 0.0, NEG_INF)
        mask_ref[3 + n] = jnp.where(jnp.abs(band4 + N_SIDE * n) <= N_SIDE, 0.0, NEG_INF)
    lane = lax.broadcasted_iota(jnp.int32, (BQ, 128), 1)
    lo = lane < HEAD_DIM
    lo_q = lane % HEAD_DIM < HEAD_DIM // 2
    n_groups = T // (BQ * GROUP)

    nib = L16 // BQ

    def rows16(g, t):
        tile = jnp.asarray(g * GROUP + t, jnp.int32)
        c = lax.div(tile, nib)
        ib = lax.rem(tile, nib)
        kst = jnp.clip(ib * BQ - N_SIDE, 0, L16 - BK)
        return (pl.multiple_of(c * L16 + ib * BQ, BQ), pl.multiple_of(c * L16 + kst, N_SIDE),
                lax.div(ib * BQ - kst, N_SIDE))

    def load_qk16(g, t):
        q0, k0, case = rows16(g, t)
        mask = mask_ref[case]
        return qp_ref[0, pl.ds(q0, BQ), :], kp_ref[0, pl.ds(k0, BK), :], mask, mask

    def load_v16(g, t):
        return vp_ref[0, pl.ds(rows16(g, t)[1], BK), :]

    def finalize16(g, tiles):
        for t, (m, l, a) in enumerate(tiles):
            q0 = rows16(g, t)[0]
            m_ref[pl.ds(q0, BQ), :] = m
            l_ref[pl.ds(q0, BQ), :] = l
            a_ref[pl.ds(q0, BQ), :] = a

    nb4 = L16 // 32

    def rows4(g, t):
        tile = jnp.asarray(g * GROUP + t, jnp.int32)
        r4 = lax.div(tile, nb4)
        i0 = lax.rem(tile, nb4) * 32
        kst = jnp.clip(i0 - 16, 0, L16 - 64)
        return ([pl.multiple_of((r4 + 4 * w) * L16 + i0, 32) for w in range(4)],
                [pl.multiple_of((r4 + 4 * w) * L16 + kst, 16) for w in range(4)],
                3 + lax.div(i0 - kst, 16))

    def load_qk4(g, t):
        qrows, krows, case = rows4(g, t)
        mask = mask_ref[case]
        return (jnp.concatenate([qp_ref[0, pl.ds(r, 32), :] for r in qrows], axis=0),
                jnp.concatenate([kp_ref[0, pl.ds(r, 64), :] for r in krows], axis=0), mask, mask)

    def load_v4(g, t):
        return jnp.concatenate([vp_ref[0, pl.ds(r, 64), :] for r in rows4(g, t)[1]], axis=0)

    def finalize4(g, tiles):
        merged = []
        for t, tile in enumerate(tiles):
            qrows = rows4(g, t)[0]
            state = [jnp.concatenate([ref[pl.ds(r, 32), :] for r in qrows], axis=0)
                     for ref in (m_ref, l_ref, a_ref)]
            merged.append((qrows, _merge(*state, *tile)))
        for qrows, new in merged:
            for ref, val in zip((m_ref, l_ref, a_ref), new):
                for w, r in enumerate(qrows):
                    ref[pl.ds(r, 32), :] = val[w * 32:(w + 1) * 32]

    def rows1(g, t):
        t0 = pl.multiple_of(jnp.asarray(g * GROUP + t, jnp.int32) * BQ, BQ)
        kst = pl.multiple_of(jnp.clip(t0 - N_SIDE, 0, T - BK), N_SIDE)
        return t0, kst, lax.div(t0 - kst, N_SIDE)

    def load_qk1(g, t):
        t0, kst, case = rows1(g, t)
        mask = mask_ref[case]
        return qn_ref[0, pl.ds(t0, BQ), :], kn_ref[0, pl.ds(kst, BK), :], mask, mask

    def load_v1(g, t):
        return vn_ref[0, pl.ds(rows1(g, t)[1], BK), :]

    def finalize1(g, tiles):
        n_i = BQ // MAX_DIL
        for t, tile in enumerate(tiles):
            t0 = rows1(g, t)[0]
            i0 = lax.div(t0, MAX_DIL)
            for c in range(MAX_DIL):
                src = pl.ds(pl.multiple_of(c * L16 + i0, n_i), n_i)
                dst = pl.ds(c, n_i, stride=MAX_DIL)
                for n, ref in enumerate((m_ref, l_ref, a_ref)):
                    tmp_ref[t, n, dst, :] = ref[src, :]
            _, l_n, a_n = _merge(tmp_ref[t, 0], tmp_ref[t, 1], tmp_ref[t, 2], *tile)
            o_ref[0, pl.ds(t0, BQ), :] = (a_n / l_n).astype(BF16)

    _attn_pipeline([(n_groups, load_qk16, load_v16, finalize16),
                    (n_groups, load_qk4, load_v4, finalize4),
                    (n_groups, load_qk1, load_v1, finalize1)], lo_q, lo, s_scr, p_scr, m_scr)


def _dilated(qkv_nat, qkv_perm):
    B, T, _ = qkv_nat.shape
    assert T % (MAX_DIL * 128) == 0 and T % (128 * GROUP) == 0 and T // MAX_DIL >= 256
    seq = lambda col0: pl.BlockSpec((1, T, 128), lambda b, j: (b, 0, col0 + j))
    return pl.pallas_call(
        functools.partial(_dilated_kernel, T=T),
        grid=(B, N_PAIRS),
        in_specs=[seq(0), seq(N_PAIRS), seq(2 * N_PAIRS), seq(0), seq(N_PAIRS), seq(2 * N_PAIRS)],
        out_specs=pl.BlockSpec((1, T, 128), lambda b, j: (b, 0, j)),
        out_shape=jax.ShapeDtypeStruct((B, T, WIDTH_A), BF16),
        scratch_shapes=[pltpu.VMEM((T, 128), F32), pltpu.VMEM((T, 128), F32),
                        pltpu.VMEM((T, 128), F32), pltpu.VMEM((GROUP, 3, 128, 128), F32),
                        pltpu.VMEM((6, 128, 256), F32)] + _pipeline_scratch(128, 256),
        compiler_params=_params(2),
        name="dilated",
    )(qkv_nat, qkv_nat, qkv_nat, qkv_perm, qkv_perm, qkv_perm)


NBR_QROWS = 2
NBR_KROWS = NA_ROWS + 2
NBR_BQ = NBR_QROWS * GRID_W
NBR_BK = NBR_KROWS * GRID_W
NBR_VARIANTS = ((0, (0, 0)), (2, (0, 0)), (4, (0, 1)), (6, (2, 2)), (8, (2, 2)))


def _nbr_kernel(q_ref, k_ref, v_ref, bias_ref, o_ref, s_scr, p_scr, m_scr, *, T):
    rows = T // GRID_W
    lo = lax.broadcasted_iota(jnp.int32, (NBR_BQ, 128), 1) < HEAD_DIM

    def where(g, t):
        r = NBR_QROWS * jnp.asarray(g * GROUP + t, jnp.int32)
        first_key_row = jnp.clip(r - NA_ROWS // 2, 0, rows - NBR_KROWS)
        variant = lax.div(r - first_key_row, 2)
        return (pl.multiple_of(r * GRID_W, NBR_BQ), pl.multiple_of(first_key_row * GRID_W, GRID_W),
                variant)

    def load_qk(g, t):
        q0, k0, variant = where(g, t)
        return (q_ref[0, pl.ds(q0, NBR_BQ), :], k_ref[0, pl.ds(k0, NBR_BK), :],
                bias_ref[0, variant], bias_ref[1, variant])

    def load_v(g, t):
        return v_ref[0, pl.ds(where(g, t)[1], NBR_BK), :]

    def finalize(g, tiles):
        for t, (_, l, a) in enumerate(tiles):
            o_ref[0, pl.ds(where(g, t)[0], NBR_BQ), :] = (a / l).astype(BF16)

    _attn_pipeline([(rows // (NBR_QROWS * GROUP), load_qk, load_v, finalize)],
                   lo, lo, s_scr, p_scr, m_scr)


def _nbr(qkv_nat, bias):
    B, T, _ = qkv_nat.shape
    assert T % (NBR_BQ * GROUP) == 0 and T // GRID_W >= 2 * NBR_KROWS
    seq = lambda col0: pl.BlockSpec((1, T, 128), lambda b, j: (b, 0, col0 + j))
    return pl.pallas_call(
        functools.partial(_nbr_kernel, T=T),
        grid=(B, N_PAIRS),
        in_specs=[seq(3 * N_PAIRS), seq(4 * N_PAIRS), seq(5 * N_PAIRS),
                  pl.BlockSpec((2, len(NBR_VARIANTS), NBR_BQ, NBR_BK), lambda b, j: (j, 0, 0, 0))],
        out_specs=pl.BlockSpec((1, T, 128), lambda b, j: (b, 0, j)),
        out_shape=jax.ShapeDtypeStruct((B, T, WIDTH_B), BF16),
        scratch_shapes=_pipeline_scratch(NBR_BQ, NBR_BK),
        compiler_params=_params(2),
        name="nbr",
    )(qkv_nat, qkv_nat, qkv_nat, bias)


def _nbr_bias(rpb):
    qc = jnp.arange(GRID_W)[:, None]
    kc = jnp.arange(GRID_W)[None, :]
    cidx = jnp.clip(kc - qc, -(NA_COLS - 1), NA_COLS - 1) + NA_COLS - 1
    cs = jnp.clip(qc - NA_COLS // 2, 0, GRID_W - NA_COLS)
    col_ok = (kc >= cs) & (kc < cs + NA_COLS)
    base = jnp.where(col_ok, LOG2E * rpb.astype(F32)[:, :, cidx], NEG_INF)
    off = jnp.array([v[0] for v in NBR_VARIANTS])[:, None, None]
    first = jnp.array([v[1] for v in NBR_VARIANTS])[:, :, None]
    q = jnp.arange(NBR_QROWS)[None, :, None]
    u = jnp.arange(NBR_KROWS)[None, None, :]
    row_ok = (u >= first) & (u < first + NA_ROWS)
    ridx = jnp.clip(u - off - q + NA_ROWS - 1, 0, 2 * NA_ROWS - 2)
    t = jnp.where(row_ok[None, :, :, :, None, None], base[:, ridx], NEG_INF)
    t = t.transpose(0, 1, 2, 4, 3, 5)
    return t.reshape(rpb.shape[0], len(NBR_VARIANTS), NBR_BQ, NBR_BK)


UP_CHUNKS = 4


def _tail_kernel(oa_ref, ob_ref, x_ref, kv_ref, ga_ref, gb_ref, g0_ref, b0_ref, wo_ref, g1_ref, b1_ref,
                 wq_ref, wxo_ref, g2_ref, b2_ref, wu_ref, wd_ref, g3_ref, b3_ref, o_ref, x2_scr, r_scr):
    s = pl.program_id(0)
    cur, prev = lax.rem(s, 2), lax.rem(s + 1, 2)

    @pl.when(s == 0)
    def _():
        x2_scr[1] = jnp.zeros(x2_scr.shape[1:], F32)
        r_scr[...] = jnp.zeros(r_scr.shape, F32)

    x2_prev = x2_scr[prev]
    xb_prev = x2_prev.astype(BF16)
    cw = D_FF // UP_CHUNKS
    hidden = []

    def up_chunk():
        c = len(hidden)
        h = jnp.maximum(jnp.dot(xb_prev, wu_ref[:, c * cw:(c + 1) * cw],
                                preferred_element_type=F32), 0.0)
        hidden.append((h * h).astype(BF16))

    up_chunk()
    y = jnp.concatenate([_rms(oa_ref[0].astype(F32), ga_ref[...]),
                         _rms(ob_ref[0].astype(F32), gb_ref[...])], axis=1).astype(BF16)
    z = jnp.dot(y, wo_ref[...], preferred_element_type=F32)
    x1 = _ln(ALPHA * _ln(x_ref[0], g0_ref[...], b0_ref[...]) + z, g1_ref[...], b1_ref[...])
    up_chunk()
    q = jnp.dot(x1.astype(BF16), wq_ref[...], preferred_element_type=F32)
    q = (q * (HEAD_DIM_X ** -0.5)).astype(BF16)
    outs = []
    for h in range(N_HEADS_X):
        k = kv_ref[0, :, h * HEAD_DIM_X:(h + 1) * HEAD_DIM_X]
        v = kv_ref[0, :, D_MODEL + h * HEAD_DIM_X:D_MODEL + (h + 1) * HEAD_DIM_X]
        sc = lax.dot_general(q[:, h * HEAD_DIM_X:(h + 1) * HEAD_DIM_X], k, NT_DIMS,
                             preferred_element_type=F32)
        pr = jnp.exp(sc - jnp.max(sc, axis=1, keepdims=True))
        l = jnp.sum(pr, axis=1, keepdims=True)
        outs.append(jnp.dot(pr.astype(BF16), v, preferred_element_type=F32) / l)
    up_chunk()
    z = jnp.dot(jnp.concatenate(outs, axis=1).astype(BF16), wxo_ref[...], preferred_element_type=F32)
    x2_scr[cur] = _ln(ALPHA * x1 + z, g2_ref[...], b2_ref[...])
    while len(hidden) < UP_CHUNKS:
        up_chunk()
    o_ref[0] = _ln(r_scr[prev], g3_ref[...], b3_ref[...])
    z = jnp.dot(jnp.concatenate(hidden, axis=1), wd_ref[...], preferred_element_type=F32)
    r_scr[cur] = ALPHA * x2_prev + z


def _tail(oa, ob, x, kv, p, tm):
    B, T, _ = x.shape
    M = kv.shape[1]
    n = T // tm
    last = B * n - 1
    vec = lambda width: _const_spec((1, width))
    mat = lambda rows, cols: _const_spec((rows, cols))

    def cur(width):
        return pl.BlockSpec((1, tm, width),
                            lambda s: (jnp.minimum(s, last) // n, jnp.minimum(s, last) % n, 0))

    return pl.pallas_call(
        _tail_kernel,
        grid=(B * n + 2,),
        in_specs=[cur(WIDTH_A), cur(WIDTH_B), cur(D_MODEL),
                  pl.BlockSpec((1, M, 2 * D_MODEL), lambda s: (jnp.minimum(s, last) // n, 0, 0)),
                  vec(WIDTH_A), vec(WIDTH_B), vec(D_MODEL), vec(D_MODEL),
                  mat(D_MODEL, D_MODEL), vec(D_MODEL), vec(D_MODEL),
                  mat(D_MODEL, D_MODEL), mat(D_MODEL, D_MODEL), vec(D_MODEL), vec(D_MODEL),
                  mat(D_MODEL, D_FF), mat(D_FF, D_MODEL), vec(D_MODEL), vec(D_MODEL)],
        out_specs=pl.BlockSpec((1, tm, D_MODEL),
                               lambda s: (jnp.maximum(s - 2, 0) // n, jnp.maximum(s - 2, 0) % n, 0)),
        out_shape=jax.ShapeDtypeStruct((B, T, D_MODEL), F32),
        scratch_shapes=[pltpu.VMEM((2, tm, D_MODEL), F32), pltpu.VMEM((2, tm, D_MODEL), F32)],
        compiler_params=_params(1),
        name="tail",
    )(oa, ob, x, kv, p["g_mix_a"], p["g_mix_b"], p["ln_in_g"], p["ln_in_b"], p["w_out"],
      p["ln1_g"], p["ln1_b"], p["w_xq"], p["w_xo"], p["ln2_g"], p["ln2_b"],
      p["w_up"], p["w_down"], p["ln3_g"], p["ln3_b"])


def _rope_tables(T):
    half = HEAD_DIM // 2
    inv = ROPE_THETA ** (-jnp.arange(half, dtype=F32) / half)
    ang = jnp.arange(T, dtype=F32)[:, None] * inv[None, :]
    cos, sin = jnp.cos(ang), jnp.sin(ang)
    return (jnp.concatenate([cos, cos, cos, cos], axis=1),
            jnp.concatenate([-sin, -sin, sin, sin], axis=1))


def _pair_rotary_layout(w_in):
    half = HEAD_DIM // 2
    idx = jnp.arange(128).reshape(4, half)[jnp.array([0, 2, 1, 3])].reshape(128)
    cols = (jnp.arange(2 * WIDTH_A // 128)[:, None] * 128 + idx[None, :]).reshape(-1)
    return jnp.concatenate([w_in[:, cols], w_in[:, 2 * WIDTH_A:]], axis=1)


def _trunk(x, mem, p, tm=256):
    B, T, _ = x.shape
    cos, sin = _rope_tables(T)
    kv = _mem_kv(mem, p["w_xkv"])
    qkv_nat, qkv_perm = _qkv(x, p["ln_in_g"], p["ln_in_b"], p["w_in"], cos, sin, 2 * tm)
    oa = _dilated(qkv_nat, qkv_perm.reshape(B, T, 3 * WIDTH_A))
    ob = _nbr(qkv_nat, p["bias"])
    return _tail(oa, ob, x, kv, p, tm)


def kernel(x_prompt, x_sample, mem_prompt, mem_sample, ln_in_g, ln_in_b, w_in, rpb, g_mix_a, g_mix_b,
           w_out, ln1_g, ln1_b, w_xq, w_xkv, w_xo, ln2_g, ln2_b, w_up, w_down, ln3_g, ln3_b):
    assert w_in.shape[0] == 1, "single-layer trunk"
    row = lambda v: v.reshape(1, -1).astype(F32)
    p = dict(
        ln_in_g=row(ln_in_g), ln_in_b=row(ln_in_b),
        w_in=_pair_rotary_layout(w_in[0]).astype(BF16), bias=_nbr_bias(rpb[0]),
        g_mix_a=row(g_mix_a[0]), g_mix_b=row(g_mix_b[0]), w_out=w_out[0].astype(BF16),
        ln1_g=row(ln1_g[0]), ln1_b=row(ln1_b[0]),
        w_xq=w_xq[0].astype(BF16), w_xkv=w_xkv[0].astype(BF16), w_xo=w_xo[0].astype(BF16),
        ln2_g=row(ln2_g[0]), ln2_b=row(ln2_b[0]),
        w_up=w_up[0].astype(BF16), w_down=w_down[0].astype(BF16),
        ln3_g=row(ln3_g[0]), ln3_b=row(ln3_b[0]),
    )
    return _trunk(x_prompt, mem_prompt, p), _trunk(x_sample, mem_sample, p)
```

```python
import functools

import jax
import jax.numpy as jnp
from jax import lax
from jax.experimental import pallas as pl
from jax.experimental.pallas import tpu as pltpu

F32 = jnp.float32
BF16 = jnp.bfloat16

D_MODEL = 1024
HEAD_DIM = 64
WIDTH_A = 512
WIDTH_B = 512
N_PAIRS = WIDTH_A // 128
GRID_W = 64
NA_ROWS = 8
NA_COLS = 16
N_HEADS_X = 4
HEAD_DIM_X = 256
D_FF = 4096
ROPE_THETA = 10000.0
LN_EPS = 1e-5
ALPHA = 2.0 ** 0.25
NEG_INF = -1e30
LOG2E = 1.4426950408889634
N_SIDE = 64
MAX_DIL = 16

VMEM_LIMIT = 56 * 1024 * 1024

NT_DIMS = (((1,), (1,)), ((), ()))


def _ln(x, g, b):
    mu = jnp.mean(x, axis=-1, keepdims=True)
    xc = x - mu
    var = jnp.mean(xc * xc, axis=-1, keepdims=True)
    return xc * lax.rsqrt(var + LN_EPS) * g + b


def _rms(x, g):
    return x * lax.rsqrt(jnp.mean(x * x, axis=-1, keepdims=True) + LN_EPS) * g


def _params(n_axes):
    return pltpu.CompilerParams(dimension_semantics=("arbitrary",) * n_axes,
                                vmem_limit_bytes=VMEM_LIMIT)


def _const_spec(shape):
    nd = len(shape)
    return pl.BlockSpec(shape, lambda *_: (0,) * nd, pipeline_mode=pl.Buffered(1))


def _mem_kv_kernel(mem_ref, w_ref, o_ref):
    o_ref[0] = jnp.dot(mem_ref[0].astype(BF16), w_ref[...],
                       preferred_element_type=F32).astype(BF16)


def _mem_kv(mem, w_xkv):
    B, M, _ = mem.shape
    return pl.pallas_call(
        _mem_kv_kernel,
        grid=(B,),
        in_specs=[pl.BlockSpec((1, M, D_MODEL), lambda b: (b, 0, 0)),
                  _const_spec((D_MODEL, 2 * D_MODEL))],
        out_specs=pl.BlockSpec((1, M, 2 * D_MODEL), lambda b: (b, 0, 0)),
        out_shape=jax.ShapeDtypeStruct((B, M, 2 * D_MODEL), BF16),
        compiler_params=_params(1),
        name="mem_kv",
    )(mem, w_xkv)


PERM_SUB = 256


def _qkv_kernel(x_ref, g_ref, b_ref, w_ref, cos_ref, sin_ref, nat_ref, perm_ref, *, tm):
    xb = _ln(x_ref[0], g_ref[...], b_ref[...]).astype(BF16)
    cos = jnp.concatenate([cos_ref[...]] * 4, axis=1)
    sin = jnp.concatenate([sin_ref[...]] * 4, axis=1)
    row = lax.broadcasted_iota(jnp.int32, (PERM_SUB, PERM_SUB), 0)
    col = lax.broadcasted_iota(jnp.int32, (PERM_SUB, PERM_SUB), 1)
    rows_per = PERM_SUB // MAX_DIL
    perm_mat = jnp.where(col == MAX_DIL * (row % rows_per) + row // rows_per, 1.0, 0.0).astype(BF16)
    for part in (0, 3, 1, 4, 2, 5):
        cols = slice(part * 512, (part + 1) * 512)
        r = jnp.dot(xb, w_ref[:, cols], preferred_element_type=F32)
        if part in (0, 1):
            swapped = jnp.concatenate([pltpu.roll(r[:, c:c + 128], 64, 1)
                                       for c in range(0, WIDTH_A, 128)], axis=1)
            r = r * cos + swapped * sin
        if part in (0, 3):
            r = r * (HEAD_DIM ** -0.5 * LOG2E)
        rb = r.astype(BF16)
        nat_ref[0, :, cols] = rb
        if part < 3:
            for g in range(tm // PERM_SUB):
                moved = jnp.dot(perm_mat, rb[g * PERM_SUB:(g + 1) * PERM_SUB],
                                preferred_element_type=F32).astype(BF16)
                for c in range(MAX_DIL):
                    perm_ref[0, c, g * rows_per:(g + 1) * rows_per, cols] = (
                        moved[c * rows_per:(c + 1) * rows_per])


def _qkv(x, ln_g, ln_b, w_in, cos, sin, tm):
    B, T, _ = x.shape
    L16 = T // MAX_DIL
    assert tm % PERM_SUB == 0
    return pl.pallas_call(
        functools.partial(_qkv_kernel, tm=tm),
        grid=(B, T // tm),
        in_specs=[pl.BlockSpec((1, tm, D_MODEL), lambda b, i: (b, i, 0)),
                  _const_spec((1, D_MODEL)), _const_spec((1, D_MODEL)),
                  _const_spec((D_MODEL, 3 * D_MODEL)),
                  pl.BlockSpec((tm, 128), lambda b, i: (i, 0)),
                  pl.BlockSpec((tm, 128), lambda b, i: (i, 0))],
        out_specs=[pl.BlockSpec((1, tm, 3 * D_MODEL), lambda b, i: (b, i, 0)),
                   pl.BlockSpec((1, MAX_DIL, tm // MAX_DIL, 3 * WIDTH_A), lambda b, i: (b, 0, i, 0))],
        out_shape=[jax.ShapeDtypeStruct((B, T, 3 * D_MODEL), BF16),
                   jax.ShapeDtypeStruct((B, MAX_DIL, L16, 3 * WIDTH_A), BF16)],
        compiler_params=_params(2),
        name="qkv",
    )(x, ln_g, ln_b, w_in, cos, sin)


GROUP = 8


def _attn_pipeline(segments, lo_q, lo, s_scr, p_scr, m_scr):
    bq, bk = s_scr.shape[-2:]

    def stage_a(seg, g, slot):
        for t in range(GROUP):
            q2, k2, b_lo, b_hi = seg[1](g, t)
            zero = jnp.zeros_like(q2)
            for h, (q1, bias) in enumerate(((jnp.where(lo_q, q2, zero), b_lo),
                                            (jnp.where(lo_q, zero, q2), b_hi))):
                s = lax.dot_general(q1, k2, NT_DIMS, preferred_element_type=F32) + bias
                s_scr[slot, t, h] = s
                m_scr[slot, t, h] = jnp.broadcast_to(jnp.max(s, axis=1, keepdims=True), (bq, 128))

    def stage_b(slot):
        for t in range(GROUP):
            for h in range(2):
                m = jnp.concatenate([m_scr[slot, t, h]] * (bk // 128), axis=1)
                p_scr[slot, t, h] = jnp.exp2((s_scr[slot, t, h] - m).astype(BF16))

    def stage_c(seg, g, slot):
        tiles = []
        for t in range(GROUP):
            v2 = seg[2](g, t)
            v_ones = jnp.concatenate([v2, jnp.ones_like(v2)], axis=1)
            pv0 = jnp.dot(p_scr[slot, t, 0], v_ones, preferred_element_type=F32)
            pv1 = jnp.dot(p_scr[slot, t, 1], v_ones, preferred_element_type=F32)
            tiles.append((jnp.where(lo, m_scr[slot, t, 0], m_scr[slot, t, 1]),
                          jnp.where(lo, pv0[:, 128:], pv1[:, 128:]),
                          jnp.where(lo, pv0[:, :128], pv1[:, :128])))
        seg[3](g, tiles)

    stage_a(segments[0], 0, 0)
    stage_b(0)
    stage_a(segments[0], 1, 1)
    for k, seg in enumerate(segments):
        n = seg[0]
        assert n >= 2 and n % 2 == 0

        for i in range(n - 2):
            stage_c(seg, i, i % 2)
            stage_b(1 - i % 2)
            stage_a(seg, i + 2, i % 2)
        nxt = segments[k + 1] if k + 1 < len(segments) else None
        stage_c(seg, n - 2, 0)
        stage_b(1)
        if nxt:
            stage_a(nxt, 0, 0)
        stage_c(seg, n - 1, 1)
        if nxt:
            stage_b(0)
            stage_a(nxt, 1, 1)


def _pipeline_scratch(bq, bk):
    return [pltpu.VMEM((2, GROUP, 2, bq, bk), F32), pltpu.VMEM((2, GROUP, 2, bq, bk), BF16),
            pltpu.VMEM((2, GROUP, 2, bq, 128), F32)]


def _merge(m_r, l_r, a_r, m_t, l_t, a_t):
    m_n = jnp.maximum(m_r, m_t)
    e_r = jnp.exp2(m_r - m_n)
    e_t = jnp.exp2(m_t - m_n)
    return m_n, l_r * e_r + l_t * e_t, a_r * e_r + a_t * e_t


def _dilated_kernel(qn_ref, kn_ref, vn_ref, qp_ref, kp_ref, vp_ref, o_ref,
                    m_ref, l_ref, a_ref, tmp_ref, mask_ref, s_scr, p_scr, m_scr, *, T):
    BQ, BK = 128, 256
    L16 = T // MAX_DIL
    row = lax.broadcasted_iota(jnp.int32, (BQ, BK), 0)
    col = lax.broadcasted_iota(jnp.int32, (BQ, BK), 1)
    band = row - col
    band4 = 4 * ((row % 32) - (col % 64)) + (row // 32 - col // 64)
    for n in range(3):
        mask_ref[n] = jnp.where(jnp.abs(band + N_SIDE * n) <= N_SIDE, 0.0, NEG_INF)
        mask_ref[3 + n] = jnp.where(jnp.abs(band4 + N_SIDE * n) <= N_SIDE, 0.0, NEG_INF)
    lane = lax.broadcasted_iota(jnp.int32, (BQ, 128), 1)
    lo = lane < HEAD_DIM
    lo_q = lane % HEAD_DIM < HEAD_DIM // 2
    n_groups = T // (BQ * GROUP)

    nib = L16 // BQ

    def rows16(g, t):
        tile = jnp.asarray(g * GROUP + t, jnp.int32)
        c = lax.div(tile, nib)
        ib = lax.rem(tile, nib)
        kst = jnp.clip(ib * BQ - N_SIDE, 0, L16 - BK)
        return (pl.multiple_of(c * L16 + ib * BQ, BQ), pl.multiple_of(c * L16 + kst, N_SIDE),
                lax.div(ib * BQ - kst, N_SIDE))

    def load_qk16(g, t):
        q0, k0, case = rows16(g, t)
        mask = mask_ref[case]
        return qp_ref[0, pl.ds(q0, BQ), :], kp_ref[0, pl.ds(k0, BK), :], mask, mask

    def load_v16(g, t):
        return vp_ref[0, pl.ds(rows16(g, t)[1], BK), :]

    def finalize16(g, tiles):
        for t, (m, l, a) in enumerate(tiles):
            q0 = rows16(g, t)[0]
            m_ref[pl.ds(q0, BQ), :] = m
            l_ref[pl.ds(q0, BQ), :] = l
            a_ref[pl.ds(q0, BQ), :] = a

    nb4 = L16 // 32

    def rows4(g, t):
        tile = jnp.asarray(g * GROUP + t, jnp.int32)
        r4 = lax.div(tile, nb4)
        i0 = lax.rem(tile, nb4) * 32
        kst = jnp.clip(i0 - 16, 0, L16 - 64)
        return ([pl.multiple_of((r4 + 4 * w) * L16 + i0, 32) for w in range(4)],
                [pl.multiple_of((r4 + 4 * w) * L16 + kst, 16) for w in range(4)],
                3 + lax.div(i0 - kst, 16))

    def load_qk4(g, t):
        qrows, krows, case = rows4(g, t)
        mask = mask_ref[case]
        return (jnp.concatenate([qp_ref[0, pl.ds(r, 32), :] for r in qrows], axis=0),
                jnp.concatenate([kp_ref[0, pl.ds(r, 64), :] for r in krows], axis=0), mask, mask)

    def load_v4(g, t):
        return jnp.concatenate([vp_ref[0, pl.ds(r, 64), :] for r in rows4(g, t)[1]], axis=0)

    def finalize4(g, tiles):
        merged = []
        for t, tile in enumerate(tiles):
            qrows = rows4(g, t)[0]
            state = [jnp.concatenate([ref[pl.ds(r, 32), :] for r in qrows], axis=0)
                     for ref in (m_ref, l_ref, a_ref)]
            merged.append((qrows, _merge(*state, *tile)))
        for qrows, new in merged:
            for ref, val in zip((m_ref, l_ref, a_ref), new):
                for w, r in enumerate(qrows):
                    ref[pl.ds(r, 32), :] = val[w * 32:(w + 1) * 32]

    def rows1(g, t):
        t0 = pl.multiple_of(jnp.asarray(g * GROUP + t, jnp.int32) * BQ, BQ)
        kst = pl.multiple_of(jnp.clip(t0 - N_SIDE, 0, T - BK), N_SIDE)
        return t0, kst, lax.div(t0 - kst, N_SIDE)

    def load_qk1(g, t):
        t0, kst, case = rows1(g, t)
        mask = mask_ref[case]
        return qn_ref[0, pl.ds(t0, BQ), :], kn_ref[0, pl.ds(kst, BK), :], mask, mask

    def load_v1(g, t):
        return vn_ref[0, pl.ds(rows1(g, t)[1], BK), :]

    def finalize1(g, tiles):
        n_i = BQ // MAX_DIL
        for t, tile in enumerate(tiles):
            t0 = rows1(g, t)[0]
            i0 = lax.div(t0, MAX_DIL)
            for c in range(MAX_DIL):
                src = pl.ds(pl.multiple_of(c * L16 + i0, n_i), n_i)
                dst = pl.ds(c, n_i, stride=MAX_DIL)
                for n, ref in enumerate((m_ref, l_ref, a_ref)):
                    tmp_ref[t, n, dst, :] = ref[src, :]
            _, l_n, a_n = _merge(tmp_ref[t, 0], tmp_ref[t, 1], tmp_ref[t, 2], *tile)
            o_ref[0, pl.ds(t0, BQ), :] = (a_n / l_n).astype(BF16)

    _attn_pipeline([(n_groups, load_qk16, load_v16, finalize16),
                    (n_groups, load_qk4, load_v4, finalize4),
                    (n_groups, load_qk1, load_v1, finalize1)], lo_q, lo, s_scr, p_scr, m_scr)


def _dilated(qkv_nat, qkv_perm):
    B, T, _ = qkv_nat.shape
    assert T % (MAX_DIL * 128) == 0 and T % (128 * GROUP) == 0 and T // MAX_DIL >= 256
    seq = lambda col0: pl.BlockSpec((1, T, 128), lambda b, j: (b, 0, col0 + j))
    return pl.pallas_call(
        functools.partial(_dilated_kernel, T=T),
        grid=(B, N_PAIRS),
        in_specs=[seq(0), seq(N_PAIRS), seq(2 * N_PAIRS), seq(0), seq(N_PAIRS), seq(2 * N_PAIRS)],
        out_specs=pl.BlockSpec((1, T, 128), lambda b, j: (b, 0, j)),
        out_shape=jax.ShapeDtypeStruct((B, T, WIDTH_A), BF16),
        scratch_shapes=[pltpu.VMEM((T, 128), F32), pltpu.VMEM((T, 128), F32),
                        pltpu.VMEM((T, 128), F32), pltpu.VMEM((GROUP, 3, 128, 128), F32),
                        pltpu.VMEM((6, 128, 256), F32)] + _pipeline_scratch(128, 256),
        compiler_params=_params(2),
        name="dilated",
    )(qkv_nat, qkv_nat, qkv_nat, qkv_perm, qkv_perm, qkv_perm)


NBR_QROWS = 2
NBR_KROWS = NA_ROWS + 2
NBR_BQ = NBR_QROWS * GRID_W
NBR_BK = NBR_KROWS * GRID_W
NBR_VARIANTS = ((0, (0, 0)), (2, (0, 0)), (4, (0, 1)), (6, (2, 2)), (8, (2, 2)))


def _nbr_kernel(q_ref, k_ref, v_ref, bias_ref, o_ref, s_scr, p_scr, m_scr, *, T):
    rows = T // GRID_W
    lo = lax.broadcasted_iota(jnp.int32, (NBR_BQ, 128), 1) < HEAD_DIM

    def where(g, t):
        r = NBR_QROWS * jnp.asarray(g * GROUP + t, jnp.int32)
        first_key_row = jnp.clip(r - NA_ROWS // 2, 0, rows - NBR_KROWS)
        variant = lax.div(r - first_key_row, 2)
        return (pl.multiple_of(r * GRID_W, NBR_BQ), pl.multiple_of(first_key_row * GRID_W, GRID_W),
                variant)

    def load_qk(g, t):
        q0, k0, variant = where(g, t)
        return (q_ref[0, pl.ds(q0, NBR_BQ), :], k_ref[0, pl.ds(k0, NBR_BK), :],
                bias_ref[0, variant], bias_ref[1, variant])

    def load_v(g, t):
        return v_ref[0, pl.ds(where(g, t)[1], NBR_BK), :]

    def finalize(g, tiles):
        for t, (_, l, a) in enumerate(tiles):
            o_ref[0, pl.ds(where(g, t)[0], NBR_BQ), :] = (a / l).astype(BF16)

    _attn_pipeline([(rows // (NBR_QROWS * GROUP), load_qk, load_v, finalize)],
                   lo, lo, s_scr, p_scr, m_scr)


def _nbr(qkv_nat, bias):
    B, T, _ = qkv_nat.shape
    assert T % (NBR_BQ * GROUP) == 0 and T // GRID_W >= 2 * NBR_KROWS
    seq = lambda col0: pl.BlockSpec((1, T, 128), lambda b, j: (b, 0, col0 + j))
    return pl.pallas_call(
        functools.partial(_nbr_kernel, T=T),
        grid=(B, N_PAIRS),
        in_specs=[seq(3 * N_PAIRS), seq(4 * N_PAIRS), seq(5 * N_PAIRS),
                  pl.BlockSpec((2, len(NBR_VARIANTS), NBR_BQ, NBR_BK), lambda b, j: (j, 0, 0, 0))],
        out_specs=pl.BlockSpec((1, T, 128), lambda b, j: (b, 0, j)),
        out_shape=jax.ShapeDtypeStruct((B, T, WIDTH_B), BF16),
        scratch_shapes=_pipeline_scratch(NBR_BQ, NBR_BK),
        compiler_params=_params(2),
        name="nbr",
    )(qkv_nat, qkv_nat, qkv_nat, bias)


def _nbr_bias(rpb):
    qc = jnp.arange(GRID_W)[:, None]
    kc = jnp.arange(GRID_W)[None, :]
    cidx = jnp.clip(kc - qc, -(NA_COLS - 1), NA_COLS - 1) + NA_COLS - 1
    cs = jnp.clip(qc - NA_COLS // 2, 0, GRID_W - NA_COLS)
    col_ok = (kc >= cs) & (kc < cs + NA_COLS)
    base = jnp.where(col_ok, LOG2E * rpb.astype(F32)[:, :, cidx], NEG_INF)
    off = jnp.array([v[0] for v in NBR_VARIANTS])[:, None, None]
    first = jnp.array([v[1] for v in NBR_VARIANTS])[:, :, None]
    q = jnp.arange(NBR_QROWS)[None, :, None]
    u = jnp.arange(NBR_KROWS)[None, None, :]
    row_ok = (u >= first) & (u < first + NA_ROWS)
    ridx = jnp.clip(u - off - q + NA_ROWS - 1, 0, 2 * NA_ROWS - 2)
    t = jnp.where(row_ok[None, :, :, :, None, None], base[:, ridx], NEG_INF)
    t = t.transpose(0, 1, 2, 4, 3, 5)
    return t.reshape(rpb.shape[0], len(NBR_VARIANTS), NBR_BQ, NBR_BK)


UP_CHUNKS = 4


def _tail_kernel(oa_ref, ob_ref, x_ref, kv_ref, ga_ref, gb_ref, g0_ref, b0_ref, wo_ref, g1_ref, b1_ref,
                 wq_ref, wxo_ref, g2_ref, b2_ref, wu_ref, wd_ref, g3_ref, b3_ref, o_ref, x2_scr, r_scr):
    s = pl.program_id(0)
    cur, prev = lax.rem(s, 2), lax.rem(s + 1, 2)

    @pl.when(s == 0)
    def _():
        x2_scr[1] = jnp.zeros(x2_scr.shape[1:], F32)
        r_scr[...] = jnp.zeros(r_scr.shape, F32)

    x2_prev = x2_scr[prev]
    xb_prev = x2_prev.astype(BF16)
    cw = D_FF // UP_CHUNKS
    hidden = []

    def up_chunk():
        c = len(hidden)
        h = jnp.maximum(jnp.dot(xb_prev, wu_ref[:, c * cw:(c + 1) * cw],
                                preferred_element_type=F32), 0.0)
        hidden.append((h * h).astype(BF16))

    up_chunk()
    y = jnp.concatenate([_rms(oa_ref[0].astype(F32), ga_ref[...]),
                         _rms(ob_ref[0].astype(F32), gb_ref[...])], axis=1).astype(BF16)
    z = jnp.dot(y, wo_ref[...], preferred_element_type=F32)
    x1 = _ln(ALPHA * _ln(x_ref[0], g0_ref[...], b0_ref[...]) + z, g1_ref[...], b1_ref[...])
    up_chunk()
    q = jnp.dot(x1.astype(BF16), wq_ref[...], preferred_element_type=F32)
    q = (q * (HEAD_DIM_X ** -0.5)).astype(BF16)
    outs = []
    for h in range(N_HEADS_X):
        k = kv_ref[0, :, h * HEAD_DIM_X:(h + 1) * HEAD_DIM_X]
        v = kv_ref[0, :, D_MODEL + h * HEAD_DIM_X:D_MODEL + (h + 1) * HEAD_DIM_X]
        sc = lax.dot_general(q[:, h * HEAD_DIM_X:(h + 1) * HEAD_DIM_X], k, NT_DIMS,
                             preferred_element_type=F32)
        pr = jnp.exp(sc - jnp.max(sc, axis=1, keepdims=True))
        l = jnp.sum(pr, axis=1, keepdims=True)
        outs.append(jnp.dot(pr.astype(BF16), v, preferred_element_type=F32) / l)
    up_chunk()
    z = jnp.dot(jnp.concatenate(outs, axis=1).astype(BF16), wxo_ref[...], preferred_element_type=F32)
    x2_scr[cur] = _ln(ALPHA * x1 + z, g2_ref[...], b2_ref[...])
    while len(hidden) < UP_CHUNKS:
        up_chunk()
    o_ref[0] = _ln(r_scr[prev], g3_ref[...], b3_ref[...])
    z = jnp.dot(jnp.concatenate(hidden, axis=1), wd_ref[...], preferred_element_type=F32)
    r_scr[cur] = ALPHA * x2_prev + z


def _tail(oa, ob, x, kv, p, tm):
    B, T, _ = x.shape
    M = kv.shape[1]
    n = T // tm
    last = B * n - 1
    vec = lambda width: _const_spec((1, width))
    mat = lambda rows, cols: _const_spec((rows, cols))

    def cur(width):
        return pl.BlockSpec((1, tm, width),
                            lambda s: (jnp.minimum(s, last) // n, jnp.minimum(s, last) % n, 0))

    return pl.pallas_call(
        _tail_kernel,
        grid=(B * n + 2,),
        in_specs=[cur(WIDTH_A), cur(WIDTH_B), cur(D_MODEL),
                  pl.BlockSpec((1, M, 2 * D_MODEL), lambda s: (jnp.minimum(s, last) // n, 0, 0)),
                  vec(WIDTH_A), vec(WIDTH_B), vec(D_MODEL), vec(D_MODEL),
                  mat(D_MODEL, D_MODEL), vec(D_MODEL), vec(D_MODEL),
                  mat(D_MODEL, D_MODEL), mat(D_MODEL, D_MODEL), vec(D_MODEL), vec(D_MODEL),
                  mat(D_MODEL, D_FF), mat(D_FF, D_MODEL), vec(D_MODEL), vec(D_MODEL)],
        out_specs=pl.BlockSpec((1, tm, D_MODEL),
                               lambda s: (jnp.maximum(s - 2, 0) // n, jnp.maximum(s - 2, 0) % n, 0)),
        out_shape=jax.ShapeDtypeStruct((B, T, D_MODEL), F32),
        scratch_shapes=[pltpu.VMEM((2, tm, D_MODEL), F32), pltpu.VMEM((2, tm, D_MODEL), F32)],
        compiler_params=_params(1),
        name="tail",
    )(oa, ob, x, kv, p["g_mix_a"], p["g_mix_b"], p["ln_in_g"], p["ln_in_b"], p["w_out"],
      p["ln1_g"], p["ln1_b"], p["w_xq"], p["w_xo"], p["ln2_g"], p["ln2_b"],
      p["w_up"], p["w_down"], p["ln3_g"], p["ln3_b"])


def _rope_tables(T):
    half = HEAD_DIM // 2
    inv = ROPE_THETA ** (-jnp.arange(half, dtype=F32) / half)
    ang = jnp.arange(T, dtype=F32)[:, None] * inv[None, :]
    cos, sin = jnp.cos(ang), jnp.sin(ang)
    return (jnp.concatenate([cos, cos, cos, cos], axis=1),
            jnp.concatenate([-sin, -sin, sin, sin], axis=1))


def _pair_rotary_layout(w_in):
    half = HEAD_DIM // 2
    idx = jnp.arange(128).reshape(4, half)[jnp.array([0, 2, 1, 3])].reshape(128)
    cols = (jnp.arange(2 * WIDTH_A // 128)[:, None] * 128 + idx[None, :]).reshape(-1)
    return jnp.concatenate([w_in[:, cols], w_in[:, 2 * WIDTH_A:]], axis=1)


def _trunk(x, mem, p, tm=256):
    B, T, _ = x.shape
    cos, sin = _rope_tables(T)
    kv = _mem_kv(mem, p["w_xkv"])
    qkv_nat, qkv_perm = _qkv(x, p["ln_in_g"], p["ln_in_b"], p["w_in"], cos, sin, 2 * tm)
    oa = _dilated(qkv_nat, qkv_perm.reshape(B, T, 3 * WIDTH_A))
    ob = _nbr(qkv_nat, p["bias"])
    return _tail(oa, ob, x, kv, p, tm)


def kernel(x_prompt, x_sample, mem_prompt, mem_sample, ln_in_g, ln_in_b, w_in, rpb, g_mix_a, g_mix_b,
           w_out, ln1_g, ln1_b, w_xq, w_xkv, w_xo, ln2_g, ln2_b, w_up, w_down, ln3_g, ln3_b):
    assert w_in.shape[0] == 1, "single-layer trunk"
    row = lambda v: v.reshape(1, -1).astype(F32)
    p = dict(
        ln_in_g=row(ln_in_g), ln_in_b=row(ln_in_b),
        w_in=_pair_rotary_layout(w_in[0]).astype(BF16), bias=_nbr_bias(rpb[0]),
        g_mix_a=row(g_mix_a[0]), g_mix_b=row(g_mix_b[0]), w_out=w_out[0].astype(BF16),
        ln1_g=row(ln1_g[0]), ln1_b=row(ln1_b[0]),
        w_xq=w_xq[0].astype(BF16), w_xkv=w_xkv[0].astype(BF16), w_xo=w_xo[0].astype(BF16),
        ln2_g=row(ln2_g[0]), ln2_b=row(ln2_b[0]),
        w_up=w_up[0].astype(BF16), w_down=w_down[0].astype(BF16),
        ln3_g=row(ln3_g[0]), ln3_b=row(ln3_b[0]),
    )
    return _trunk(x_prompt, mem_prompt, p), _trunk(x_sample, mem_sample, p)
```

```python
import functools

import jax
import jax.numpy as jnp
from jax import lax
from jax.experimental import pallas as pl
from jax.experimental.pallas import tpu as pltpu

F32 = jnp.float32
BF16 = jnp.bfloat16

D_MODEL = 1024
HEAD_DIM = 64
WIDTH_A = 512
WIDTH_B = 512
N_PAIRS = WIDTH_A // 128
GRID_W = 64
NA_ROWS = 8
NA_COLS = 16
N_HEADS_X = 4
HEAD_DIM_X = 256
D_FF = 4096
ROPE_THETA = 10000.0
LN_EPS = 1e-5
ALPHA = 2.0 ** 0.25
NEG_INF = -1e30
LOG2E = 1.4426950408889634
N_SIDE = 64
MAX_DIL = 16

VMEM_LIMIT = 56 * 1024 * 1024

NT_DIMS = (((1,), (1,)), ((), ()))


def _ln(x, g, b):
    mu = jnp.mean(x, axis=-1, keepdims=True)
    xc = x - mu
    var = jnp.mean(xc * xc, axis=-1, keepdims=True)
    return xc * lax.rsqrt(var + LN_EPS) * g + b


def _rms(x, g):
    return x * lax.rsqrt(jnp.mean(x * x, axis=-1, keepdims=True) + LN_EPS) * g


def _params(n_axes):
    return pltpu.CompilerParams(dimension_semantics=("arbitrary",) * n_axes,
                                vmem_limit_bytes=VMEM_LIMIT)


def _const_spec(shape):
    nd = len(shape)
    return pl.BlockSpec(shape, lambda *_: (0,) * nd, pipeline_mode=pl.Buffered(1))


def _mem_kv_kernel(mem_ref, w_ref, o_ref):
    o_ref[0] = jnp.dot(mem_ref[0].astype(BF16), w_ref[...],
                       preferred_element_type=F32).astype(BF16)


def _mem_kv(mem, w_xkv):
    B, M, _ = mem.shape
    return pl.pallas_call(
        _mem_kv_kernel,
        grid=(B,),
        in_specs=[pl.BlockSpec((1, M, D_MODEL), lambda b: (b, 0, 0)),
                  _const_spec((D_MODEL, 2 * D_MODEL))],
        out_specs=pl.BlockSpec((1, M, 2 * D_MODEL), lambda b: (b, 0, 0)),
        out_shape=jax.ShapeDtypeStruct((B, M, 2 * D_MODEL), BF16),
        compiler_params=_params(1),
        name="mem_kv",
    )(mem, w_xkv)


PERM_SUB = 256


def _qkv_kernel(x_ref, g_ref, b_ref, w_ref, cos_ref, sin_ref, nat_ref, perm_ref, *, tm):
    xb = _ln(x_ref[0], g_ref[...], b_ref[...]).astype(BF16)
    cos = jnp.concatenate([cos_ref[...]] * 4, axis=1)
    sin = jnp.concatenate([sin_ref[...]] * 4, axis=1)
    row = lax.broadcasted_iota(jnp.int32, (PERM_SUB, PERM_SUB), 0)
    col = lax.broadcasted_iota(jnp.int32, (PERM_SUB, PERM_SUB), 1)
    rows_per = PERM_SUB // MAX_DIL
    perm_mat = jnp.where(col == MAX_DIL * (row % rows_per) + row // rows_per, 1.0, 0.0).astype(BF16)
    for part in (0, 3, 1, 4, 2, 5):
        cols = slice(part * 512, (part + 1) * 512)
        r = jnp.dot(xb, w_ref[:, cols], preferred_element_type=F32)
        if part in (0, 1):
            swapped = jnp.concatenate([pltpu.roll(r[:, c:c + 128], 64, 1)
                                       for c in range(0, WIDTH_A, 128)], axis=1)
            r = r * cos + swapped * sin
        if part in (0, 3):
            r = r * (HEAD_DIM ** -0.5 * LOG2E)
        rb = r.astype(BF16)
        nat_ref[0, :, cols] = rb
        if part < 3:
            for g in range(tm // PERM_SUB):
                moved = jnp.dot(perm_mat, rb[g * PERM_SUB:(g + 1) * PERM_SUB],
                                preferred_element_type=F32).astype(BF16)
                for c in range(MAX_DIL):
                    perm_ref[0, c, g * rows_per:(g + 1) * rows_per, cols] = (
                        moved[c * rows_per:(c + 1) * rows_per])


def _qkv(x, ln_g, ln_b, w_in, cos, sin, tm):
    B, T, _ = x.shape
    L16 = T // MAX_DIL
    assert tm % PERM_SUB == 0
    return pl.pallas_call(
        functools.partial(_qkv_kernel, tm=tm),
        grid=(B, T // tm),
        in_specs=[pl.BlockSpec((1, tm, D_MODEL), lambda b, i: (b, i, 0)),
                  _const_spec((1, D_MODEL)), _const_spec((1, D_MODEL)),
                  _const_spec((D_MODEL, 3 * D_MODEL)),
                  pl.BlockSpec((tm, 128), lambda b, i: (i, 0)),
                  pl.BlockSpec((tm, 128), lambda b, i: (i, 0))],
        out_specs=[pl.BlockSpec((1, tm, 3 * D_MODEL), lambda b, i: (b, i, 0)),
                   pl.BlockSpec((1, MAX_DIL, tm // MAX_DIL, 3 * WIDTH_A), lambda b, i: (b, 0, i, 0))],
        out_shape=[jax.ShapeDtypeStruct((B, T, 3 * D_MODEL), BF16),
                   jax.ShapeDtypeStruct((B, MAX_DIL, L16, 3 * WIDTH_A), BF16)],
        compiler_params=_params(2),
        name="qkv",
    )(x, ln_g, ln_b, w_in, cos, sin)


GROUP = 8


def _attn_pipeline(segments, lo_q, lo, s_scr, p_scr, m_scr, unroll):
    bq, bk = s_scr.shape[-2:]

    def stage_a(seg, g, slot):
        for t in range(GROUP):
            q2, k2, b_lo, b_hi = seg[1](g, t)
            zero = jnp.zeros_like(q2)
            for h, (q1, bias) in enumerate(((jnp.where(lo_q, q2, zero), b_lo),
                                            (jnp.where(lo_q, zero, q2), b_hi))):
                s = lax.dot_general(q1, k2, NT_DIMS, preferred_element_type=F32) + bias
                s_scr[slot, t, h] = s
                m_scr[slot, t, h] = jnp.broadcast_to(jnp.max(s, axis=1, keepdims=True), (bq, 128))

    def stage_b(slot):
        for t in range(GROUP):
            for h in range(2):
                m = jnp.concatenate([m_scr[slot, t, h]] * (bk // 128), axis=1)
                p_scr[slot, t, h] = jnp.exp2((s_scr[slot, t, h] - m).astype(BF16))

    def stage_c(seg, g, slot):
        tiles = []
        for t in range(GROUP):
            v2 = seg[2](g, t)
            v_ones = jnp.concatenate([v2, jnp.ones_like(v2)], axis=1)
            pv0 = jnp.dot(p_scr[slot, t, 0], v_ones, preferred_element_type=F32)
            pv1 = jnp.dot(p_scr[slot, t, 1], v_ones, preferred_element_type=F32)
            tiles.append((jnp.where(lo, m_scr[slot, t, 0], m_scr[slot, t, 1]),
                          jnp.where(lo, pv0[:, 128:], pv1[:, 128:]),
                          jnp.where(lo, pv0[:, :128], pv1[:, :128])))
        seg[3](g, tiles)

    stage_a(segments[0], 0, 0)
    stage_b(0)
    stage_a(segments[0], 1, 1)
    for k, seg in enumerate(segments):
        n = seg[0]
        assert n >= 2 and n % 2 == 0

        def step(i, slot):
            stage_c(seg, i, slot)
            stage_b(1 - slot)
            stage_a(seg, i + 2, slot)

        def body(j, carry):
            step(2 * j, 0)
            step(2 * j + 1, 1)
            return carry

        if unroll:
            for j in range((n - 2) // 2):
                body(j, 0)
        else:
            lax.fori_loop(0, (n - 2) // 2, body, 0)
        nxt = segments[k + 1] if k + 1 < len(segments) else None
        stage_c(seg, n - 2, 0)
        stage_b(1)
        if nxt:
            stage_a(nxt, 0, 0)
        stage_c(seg, n - 1, 1)
        if nxt:
            stage_b(0)
            stage_a(nxt, 1, 1)


def _pipeline_scratch(bq, bk):
    return [pltpu.VMEM((2, GROUP, 2, bq, bk), F32), pltpu.VMEM((2, GROUP, 2, bq, bk), BF16),
            pltpu.VMEM((2, GROUP, 2, bq, 128), F32)]


def _merge(m_r, l_r, a_r, m_t, l_t, a_t):
    m_n = jnp.maximum(m_r, m_t)
    e_r = jnp.exp2(m_r - m_n)
    e_t = jnp.exp2(m_t - m_n)
    return m_n, l_r * e_r + l_t * e_t, a_r * e_r + a_t * e_t


def _dilated_kernel(qn_ref, kn_ref, vn_ref, qp_ref, kp_ref, vp_ref, o_ref,
                    m_ref, l_ref, a_ref, tmp_ref, mask_ref, s_scr, p_scr, m_scr, *, T):
    BQ, BK = 128, 256
    L16 = T // MAX_DIL
    row = lax.broadcasted_iota(jnp.int32, (BQ, BK), 0)
    col = lax.broadcasted_iota(jnp.int32, (BQ, BK), 1)
    band = row - col
    band4 = 4 * ((row % 32) - (col % 64)) + (row // 32 - col // 64)
    for n in range(3):
        mask_ref[n] = jnp.where(jnp.abs(band + N_SIDE * n) <= N_SIDE, 0.0, NEG_INF)
        mask_ref[3 + n] = jnp.where(jnp.abs(band4 + N_SIDE * n) <= N_SIDE, 0.0, NEG_INF)
    lane = lax.broadcasted_iota(jnp.int32, (BQ, 128), 1)
    lo = lane < HEAD_DIM
    lo_q = lane % HEAD_DIM < HEAD_DIM // 2
    n_groups = T // (BQ * GROUP)

    nib = L16 // BQ

    def rows16(g, t):
        tile = jnp.asarray(g * GROUP + t, jnp.int32)
        c = lax.div(tile, nib)
        ib = lax.rem(tile, nib)
        kst = jnp.clip(ib * BQ - N_SIDE, 0, L16 - BK)
        return (pl.multiple_of(c * L16 + ib * BQ, BQ), pl.multiple_of(c * L16 + kst, N_SIDE),
                lax.div(ib * BQ - kst, N_SIDE))

    def load_qk16(g, t):
        q0, k0, case = rows16(g, t)
        mask = mask_ref[case]
        return qp_ref[0, pl.ds(q0, BQ), :], kp_ref[0, pl.ds(k0, BK), :], mask, mask

    def load_v16(g, t):
        return vp_ref[0, pl.ds(rows16(g, t)[1], BK), :]

    def finalize16(g, tiles):
        for t, (m, l, a) in enumerate(tiles):
            q0 = rows16(g, t)[0]
            m_ref[pl.ds(q0, BQ), :] = m
            l_ref[pl.ds(q0, BQ), :] = l
            a_ref[pl.ds(q0, BQ), :] = a

    nb4 = L16 // 32

    def rows4(g, t):
        tile = jnp.asarray(g * GROUP + t, jnp.int32)
        r4 = lax.div(tile, nb4)
        i0 = lax.rem(tile, nb4) * 32
        kst = jnp.clip(i0 - 16, 0, L16 - 64)
        return ([pl.multiple_of((r4 + 4 * w) * L16 + i0, 32) for w in range(4)],
                [pl.multiple_of((r4 + 4 * w) * L16 + kst, 16) for w in range(4)],
                3 + lax.div(i0 - kst, 16))

    def load_qk4(g, t):
        qrows, krows, case = rows4(g, t)
        mask = mask_ref[case]
        return (jnp.concatenate([qp_ref[0, pl.ds(r, 32), :] for r in qrows], axis=0),
                jnp.concatenate([kp_ref[0, pl.ds(r, 64), :] for r in krows], axis=0), mask, mask)

    def load_v4(g, t):
        return jnp.concatenate([vp_ref[0, pl.ds(r, 64), :] for r in rows4(g, t)[1]], axis=0)

    def finalize4(g, tiles):
        merged = []
        for t, tile in enumerate(tiles):
            qrows = rows4(g, t)[0]
            state = [jnp.concatenate([ref[pl.ds(r, 32), :] for r in qrows], axis=0)
                     for ref in (m_ref, l_ref, a_ref)]
            merged.append((qrows, _merge(*state, *tile)))
        for qrows, new in merged:
            for ref, val in zip((m_ref, l_ref, a_ref), new):
                for w, r in enumerate(qrows):
                    ref[pl.ds(r, 32), :] = val[w * 32:(w + 1) * 32]

    def rows1(g, t):
        t0 = pl.multiple_of(jnp.asarray(g * GROUP + t, jnp.int32) * BQ, BQ)
        kst = pl.multiple_of(jnp.clip(t0 - N_SIDE, 0, T - BK), N_SIDE)
        return t0, kst, lax.div(t0 - kst, N_SIDE)

    def load_qk1(g, t):
        t0, kst, case = rows1(g, t)
        mask = mask_ref[case]
        return qn_ref[0, pl.ds(t0, BQ), :], kn_ref[0, pl.ds(kst, BK), :], mask, mask

    def load_v1(g, t):
        return vn_ref[0, pl.ds(rows1(g, t)[1], BK), :]

    def finalize1(g, tiles):
        n_i = BQ // MAX_DIL
        for t, tile in enumerate(tiles):
            t0 = rows1(g, t)[0]
            i0 = lax.div(t0, MAX_DIL)
            for c in range(MAX_DIL):
                src = pl.ds(pl.multiple_of(c * L16 + i0, n_i), n_i)
                dst = pl.ds(c, n_i, stride=MAX_DIL)
                for n, ref in enumerate((m_ref, l_ref, a_ref)):
                    tmp_ref[t, n, dst, :] = ref[src, :]
            _, l_n, a_n = _merge(tmp_ref[t, 0], tmp_ref[t, 1], tmp_ref[t, 2], *tile)
            o_ref[0, pl.ds(t0, BQ), :] = (a_n / l_n).astype(BF16)

    _attn_pipeline([(n_groups, load_qk16, load_v16, finalize16),
                    (n_groups, load_qk4, load_v4, finalize4),
                    (n_groups, load_qk1, load_v1, finalize1)], lo_q, lo, s_scr, p_scr, m_scr,
                   unroll=True)


def _dilated(qkv_nat, qkv_perm):
    B, T, _ = qkv_nat.shape
    assert T % (MAX_DIL * 128) == 0 and T % (128 * GROUP) == 0 and T // MAX_DIL >= 256
    seq = lambda col0: pl.BlockSpec((1, T, 128), lambda b, j: (b, 0, col0 + j))
    return pl.pallas_call(
        functools.partial(_dilated_kernel, T=T),
        grid=(B, N_PAIRS),
        in_specs=[seq(0), seq(N_PAIRS), seq(2 * N_PAIRS), seq(0), seq(N_PAIRS), seq(2 * N_PAIRS)],
        out_specs=pl.BlockSpec((1, T, 128), lambda b, j: (b, 0, j)),
        out_shape=jax.ShapeDtypeStruct((B, T, WIDTH_A), BF16),
        scratch_shapes=[pltpu.VMEM((T, 128), F32), pltpu.VMEM((T, 128), F32),
                        pltpu.VMEM((T, 128), F32), pltpu.VMEM((GROUP, 3, 128, 128), F32),
                        pltpu.VMEM((6, 128, 256), F32)] + _pipeline_scratch(128, 256),
        compiler_params=_params(2),
        name="dilated",
    )(qkv_nat, qkv_nat, qkv_nat, qkv_perm, qkv_perm, qkv_perm)


NBR_QROWS = 2
NBR_KROWS = NA_ROWS + 2
NBR_BQ = NBR_QROWS * GRID_W
NBR_BK = NBR_KROWS * GRID_W
NBR_VARIANTS = ((0, (0, 0)), (2, (0, 0)), (4, (0, 1)), (6, (2, 2)), (8, (2, 2)))


def _nbr_kernel(q_ref, k_ref, v_ref, bias_ref, o_ref, s_scr, p_scr, m_scr, *, T):
    rows = T // GRID_W
    lo = lax.broadcasted_iota(jnp.int32, (NBR_BQ, 128), 1) < HEAD_DIM

    def where(g, t):
        r = NBR_QROWS * jnp.asarray(g * GROUP + t, jnp.int32)
        first_key_row = jnp.clip(r - NA_ROWS // 2, 0, rows - NBR_KROWS)
        variant = lax.div(r - first_key_row, 2)
        return (pl.multiple_of(r * GRID_W, NBR_BQ), pl.multiple_of(first_key_row * GRID_W, GRID_W),
                variant)

    def load_qk(g, t):
        q0, k0, variant = where(g, t)
        return (q_ref[0, pl.ds(q0, NBR_BQ), :], k_ref[0, pl.ds(k0, NBR_BK), :],
                bias_ref[0, variant], bias_ref[1, variant])

    def load_v(g, t):
        return v_ref[0, pl.ds(where(g, t)[1], NBR_BK), :]

    def finalize(g, tiles):
        for t, (_, l, a) in enumerate(tiles):
            o_ref[0, pl.ds(where(g, t)[0], NBR_BQ), :] = (a / l).astype(BF16)

    _attn_pipeline([(rows // (NBR_QROWS * GROUP), load_qk, load_v, finalize)],
                   lo, lo, s_scr, p_scr, m_scr, unroll=False)


def _nbr(qkv_nat, bias):
    B, T, _ = qkv_nat.shape
    assert T % (NBR_BQ * GROUP) == 0 and T // GRID_W >= 2 * NBR_KROWS
    seq = lambda col0: pl.BlockSpec((1, T, 128), lambda b, j: (b, 0, col0 + j))
    return pl.pallas_call(
        functools.partial(_nbr_kernel, T=T),
        grid=(B, N_PAIRS),
        in_specs=[seq(3 * N_PAIRS), seq(4 * N_PAIRS), seq(5 * N_PAIRS),
                  pl.BlockSpec((2, len(NBR_VARIANTS), NBR_BQ, NBR_BK), lambda b, j: (j, 0, 0, 0))],
        out_specs=pl.BlockSpec((1, T, 128), lambda b, j: (b, 0, j)),
        out_shape=jax.ShapeDtypeStruct((B, T, WIDTH_B), BF16),
        scratch_shapes=_pipeline_scratch(NBR_BQ, NBR_BK),
        compiler_params=_params(2),
        name="nbr",
    )(qkv_nat, qkv_nat, qkv_nat, bias)


def _nbr_bias(rpb):
    qc = jnp.arange(GRID_W)[:, None]
    kc = jnp.arange(GRID_W)[None, :]
    cidx = jnp.clip(kc - qc, -(NA_COLS - 1), NA_COLS - 1) + NA_COLS - 1
    cs = jnp.clip(qc - NA_COLS // 2, 0, GRID_W - NA_COLS)
    col_ok = (kc >= cs) & (kc < cs + NA_COLS)
    base = jnp.where(col_ok, LOG2E * rpb.astype(F32)[:, :, cidx], NEG_INF)
    off = jnp.array([v[0] for v in NBR_VARIANTS])[:, None, None]
    first = jnp.array([v[1] for v in NBR_VARIANTS])[:, :, None]
    q = jnp.arange(NBR_QROWS)[None, :, None]
    u = jnp.arange(NBR_KROWS)[None, None, :]
    row_ok = (u >= first) & (u < first + NA_ROWS)
    ridx = jnp.clip(u - off - q + NA_ROWS - 1, 0, 2 * NA_ROWS - 2)
    t = jnp.where(row_ok[None, :, :, :, None, None], base[:, ridx], NEG_INF)
    t = t.transpose(0, 1, 2, 4, 3, 5)
    return t.reshape(rpb.shape[0], len(NBR_VARIANTS), NBR_BQ, NBR_BK)


UP_CHUNKS = 4


def _tail_kernel(oa_ref, ob_ref, x_ref, kv_ref, ga_ref, gb_ref, g0_ref, b0_ref, wo_ref, g1_ref, b1_ref,
                 wq_ref, wxo_ref, g2_ref, b2_ref, wu_ref, wd_ref, g3_ref, b3_ref, o_ref, x2_scr, r_scr):
    s = pl.program_id(0)
    cur, prev = lax.rem(s, 2), lax.rem(s + 1, 2)

    @pl.when(s == 0)
    def _():
        x2_scr[1] = jnp.zeros(x2_scr.shape[1:], F32)
        r_scr[...] = jnp.zeros(r_scr.shape, F32)

    x2_prev = x2_scr[prev]
    xb_prev = x2_prev.astype(BF16)
    cw = D_FF // UP_CHUNKS
    hidden = []

    def up_chunk():
        c = len(hidden)
        h = jnp.maximum(jnp.dot(xb_prev, wu_ref[:, c * cw:(c + 1) * cw],
                                preferred_element_type=F32), 0.0)
        hidden.append((h * h).astype(BF16))

    up_chunk()
    y = jnp.concatenate([_rms(oa_ref[0].astype(F32), ga_ref[...]),
                         _rms(ob_ref[0].astype(F32), gb_ref[...])], axis=1).astype(BF16)
    z = jnp.dot(y, wo_ref[...], preferred_element_type=F32)
    x1 = _ln(ALPHA * _ln(x_ref[0], g0_ref[...], b0_ref[...]) + z, g1_ref[...], b1_ref[...])
    up_chunk()
    q = jnp.dot(x1.astype(BF16), wq_ref[...], preferred_element_type=F32)
    q = (q * (HEAD_DIM_X ** -0.5)).astype(BF16)
    outs = []
    for h in range(N_HEADS_X):
        k = kv_ref[0, :, h * HEAD_DIM_X:(h + 1) * HEAD_DIM_X]
        v = kv_ref[0, :, D_MODEL + h * HEAD_DIM_X:D_MODEL + (h + 1) * HEAD_DIM_X]
        sc = lax.dot_general(q[:, h * HEAD_DIM_X:(h + 1) * HEAD_DIM_X], k, NT_DIMS,
                             preferred_element_type=F32)
        pr = jnp.exp(sc - jnp.max(sc, axis=1, keepdims=True))
        l = jnp.sum(pr, axis=1, keepdims=True)
        outs.append(jnp.dot(pr.astype(BF16), v, preferred_element_type=F32) / l)
    up_chunk()
    z = jnp.dot(jnp.concatenate(outs, axis=1).astype(BF16), wxo_ref[...], preferred_element_type=F32)
    x2_scr[cur] = _ln(ALPHA * x1 + z, g2_ref[...], b2_ref[...])
    while len(hidden) < UP_CHUNKS:
        up_chunk()
    o_ref[0] = _ln(r_scr[prev], g3_ref[...], b3_ref[...])
    z = jnp.dot(jnp.concatenate(hidden, axis=1), wd_ref[...], preferred_element_type=F32)
    r_scr[cur] = ALPHA * x2_prev + z


def _tail(oa, ob, x, kv, p, tm):
    B, T, _ = x.shape
    M = kv.shape[1]
    n = T // tm
    last = B * n - 1
    vec = lambda width: _const_spec((1, width))
    mat = lambda rows, cols: _const_spec((rows, cols))

    def cur(width):
        return pl.BlockSpec((1, tm, width),
                            lambda s: (jnp.minimum(s, last) // n, jnp.minimum(s, last) % n, 0))

    return pl.pallas_call(
        _tail_kernel,
        grid=(B * n + 2,),
        in_specs=[cur(WIDTH_A), cur(WIDTH_B), cur(D_MODEL),
                  pl.BlockSpec((1, M, 2 * D_MODEL), lambda s: (jnp.minimum(s, last) // n, 0, 0)),
                  vec(WIDTH_A), vec(WIDTH_B), vec(D_MODEL), vec(D_MODEL),
                  mat(D_MODEL, D_MODEL), vec(D_MODEL), vec(D_MODEL),
                  mat(D_MODEL, D_MODEL), mat(D_MODEL, D_MODEL), vec(D_MODEL), vec(D_MODEL),
                  mat(D_MODEL, D_FF), mat(D_FF, D_MODEL), vec(D_MODEL), vec(D_MODEL)],
        out_specs=pl.BlockSpec((1, tm, D_MODEL),
                               lambda s: (jnp.maximum(s - 2, 0) // n, jnp.maximum(s - 2, 0) % n, 0)),
        out_shape=jax.ShapeDtypeStruct((B, T, D_MODEL), F32),
        scratch_shapes=[pltpu.VMEM((2, tm, D_MODEL), F32), pltpu.VMEM((2, tm, D_MODEL), F32)],
        compiler_params=_params(1),
        name="tail",
    )(oa, ob, x, kv, p["g_mix_a"], p["g_mix_b"], p["ln_in_g"], p["ln_in_b"], p["w_out"],
      p["ln1_g"], p["ln1_b"], p["w_xq"], p["w_xo"], p["ln2_g"], p["ln2_b"],
      p["w_up"], p["w_down"], p["ln3_g"], p["ln3_b"])


def _rope_tables(T):
    half = HEAD_DIM // 2
    inv = ROPE_THETA ** (-jnp.arange(half, dtype=F32) / half)
    ang = jnp.arange(T, dtype=F32)[:, None] * inv[None, :]
    cos, sin = jnp.cos(ang), jnp.sin(ang)
    return (jnp.concatenate([cos, cos, cos, cos], axis=1),
            jnp.concatenate([-sin, -sin, sin, sin], axis=1))


def _pair_rotary_layout(w_in):
    half = HEAD_DIM // 2
    idx = jnp.arange(128).reshape(4, half)[jnp.array([0, 2, 1, 3])].reshape(128)
    cols = (jnp.arange(2 * WIDTH_A // 128)[:, None] * 128 + idx[None, :]).reshape(-1)
    return jnp.concatenate([w_in[:, cols], w_in[:, 2 * WIDTH_A:]], axis=1)


def _trunk(x, mem, p, tm=256):
    B, T, _ = x.shape
    cos, sin = _rope_tables(T)
    kv = _mem_kv(mem, p["w_xkv"])
    qkv_nat, qkv_perm = _qkv(x, p["ln_in_g"], p["ln_in_b"], p["w_in"], cos, sin, 4 * tm)
    oa = _dilated(qkv_nat, qkv_perm.reshape(B, T, 3 * WIDTH_A))
    ob = _nbr(qkv_nat, p["bias"])
    return _tail(oa, ob, x, kv, p, 2 * tm)


def kernel(x_prompt, x_sample, mem_prompt, mem_sample, ln_in_g, ln_in_b, w_in, rpb, g_mix_a, g_mix_b,
           w_out, ln1_g, ln1_b, w_xq, w_xkv, w_xo, ln2_g, ln2_b, w_up, w_down, ln3_g, ln3_b):
    assert w_in.shape[0] == 1, "single-layer trunk"
    row = lambda v: v.reshape(1, -1).astype(F32)
    p = dict(
        ln_in_g=row(ln_in_g), ln_in_b=row(ln_in_b),
        w_in=_pair_rotary_layout(w_in[0]).astype(BF16), bias=_nbr_bias(rpb[0]),
        g_mix_a=row(g_mix_a[0]), g_mix_b=row(g_mix_b[0]), w_out=w_out[0].astype(BF16),
        ln1_g=row(ln1_g[0]), ln1_b=row(ln1_b[0]),
        w_xq=w_xq[0].astype(BF16), w_xkv=w_xkv[0].astype(BF16), w_xo=w_xo[0].astype(BF16),
        ln2_g=row(ln2_g[0]), ln2_b=row(ln2_b[0]),
        w_up=w_up[0].astype(BF16), w_down=w_down[0].astype(BF16),
        ln3_g=row(ln3_g[0]), ln3_b=row(ln3_b[0]),
    )
    return _trunk(x_prompt, mem_prompt, p), _trunk(x_sample, mem_sample, p)
```

```python
import functools

import jax
import jax.numpy as jnp
from jax import lax
from jax.experimental import pallas as pl
from jax.experimental.pallas import tpu as pltpu

F32 = jnp.float32
BF16 = jnp.bfloat16

D_MODEL = 1024
HEAD_DIM = 64
WIDTH_A = 512
WIDTH_B = 512
N_PAIRS = WIDTH_A // 128
GRID_W = 64
NA_ROWS = 8
NA_COLS = 16
N_HEADS_X = 4
HEAD_DIM_X = 256
D_FF = 4096
ROPE_THETA = 10000.0
LN_EPS = 1e-5
ALPHA = 2.0 ** 0.25
NEG_INF = -1e30
LOG2E = 1.4426950408889634
N_SIDE = 64
MAX_DIL = 16

VMEM_LIMIT = 56 * 1024 * 1024

NT_DIMS = (((1,), (1,)), ((), ()))


def _ln(x, g, b):
    mu = jnp.mean(x, axis=-1, keepdims=True)
    xc = x - mu
    var = jnp.mean(xc * xc, axis=-1, keepdims=True)
    return xc * lax.rsqrt(var + LN_EPS) * g + b


def _rms(x, g):
    return x * lax.rsqrt(jnp.mean(x * x, axis=-1, keepdims=True) + LN_EPS) * g


def _params(n_axes):
    return pltpu.CompilerParams(dimension_semantics=("arbitrary",) * n_axes,
                                vmem_limit_bytes=VMEM_LIMIT)


def _const_spec(shape):
    nd = len(shape)
    return pl.BlockSpec(shape, lambda *_: (0,) * nd, pipeline_mode=pl.Buffered(1))


def _mem_kv_kernel(mem_ref, w_ref, o_ref):
    o_ref[0] = jnp.dot(mem_ref[0].astype(BF16), w_ref[...],
                       preferred_element_type=F32).astype(BF16)


def _mem_kv(mem, w_xkv):
    B, M, _ = mem.shape
    return pl.pallas_call(
        _mem_kv_kernel,
        grid=(B,),
        in_specs=[pl.BlockSpec((1, M, D_MODEL), lambda b: (b, 0, 0)),
                  _const_spec((D_MODEL, 2 * D_MODEL))],
        out_specs=pl.BlockSpec((1, M, 2 * D_MODEL), lambda b: (b, 0, 0)),
        out_shape=jax.ShapeDtypeStruct((B, M, 2 * D_MODEL), BF16),
        compiler_params=_params(1),
        name="mem_kv",
    )(mem, w_xkv)


PERM_SUB = 256


def _qkv_kernel(x_ref, g_ref, b_ref, w_ref, cos_ref, sin_ref, nat_ref, perm_ref, *, tm):
    xb = _ln(x_ref[0], g_ref[...], b_ref[...]).astype(BF16)
    cos = jnp.concatenate([cos_ref[...]] * 4, axis=1)
    sin = jnp.concatenate([sin_ref[...]] * 4, axis=1)
    row = lax.broadcasted_iota(jnp.int32, (PERM_SUB, PERM_SUB), 0)
    col = lax.broadcasted_iota(jnp.int32, (PERM_SUB, PERM_SUB), 1)
    rows_per = PERM_SUB // MAX_DIL
    perm_mat = jnp.where(col == MAX_DIL * (row % rows_per) + row // rows_per, 1.0, 0.0).astype(BF16)
    for part in (0, 3, 1, 4, 2, 5):
        cols = slice(part * 512, (part + 1) * 512)
        r = jnp.dot(xb, w_ref[:, cols], preferred_element_type=F32)
        if part in (0, 1):
            swapped = jnp.concatenate([pltpu.roll(r[:, c:c + 128], 64, 1)
                                       for c in range(0, WIDTH_A, 128)], axis=1)
            r = r * cos + swapped * sin
        if part in (0, 3):
            r = r * (HEAD_DIM ** -0.5 * LOG2E)
        rb = r.astype(BF16)
        nat_ref[0, :, cols] = rb
        if part < 3:
            for g in range(tm // PERM_SUB):
                moved = jnp.dot(perm_mat, rb[g * PERM_SUB:(g + 1) * PERM_SUB],
                                preferred_element_type=F32).astype(BF16)
                for c in range(MAX_DIL):
                    perm_ref[0, c, g * rows_per:(g + 1) * rows_per, cols] = (
                        moved[c * rows_per:(c + 1) * rows_per])


def _qkv(x, ln_g, ln_b, w_in, cos, sin, tm):
    B, T, _ = x.shape
    L16 = T // MAX_DIL
    assert tm % PERM_SUB == 0 and T % tm == 0
    return pl.pallas_call(
        functools.partial(_qkv_kernel, tm=tm),
        grid=(B, T // tm),
        in_specs=[pl.BlockSpec((1, tm, D_MODEL), lambda b, i: (b, i, 0)),
                  _const_spec((1, D_MODEL)), _const_spec((1, D_MODEL)),
                  _const_spec((D_MODEL, 3 * D_MODEL)),
                  pl.BlockSpec((tm, 128), lambda b, i: (i, 0)),
                  pl.BlockSpec((tm, 128), lambda b, i: (i, 0))],
        out_specs=[pl.BlockSpec((1, tm, 3 * D_MODEL), lambda b, i: (b, i, 0)),
                   pl.BlockSpec((1, MAX_DIL, tm // MAX_DIL, 3 * WIDTH_A), lambda b, i: (b, 0, i, 0))],
        out_shape=[jax.ShapeDtypeStruct((B, T, 3 * D_MODEL), BF16),
                   jax.ShapeDtypeStruct((B, MAX_DIL, L16, 3 * WIDTH_A), BF16)],
        compiler_params=_params(2),
        name="qkv",
    )(x, ln_g, ln_b, w_in, cos, sin)


GROUP = 8


def _attn_pipeline(segments, lo_q, lo, s_scr, p_scr, m_scr, unroll):
    bq, bk = s_scr.shape[-2:]

    def stage_a(seg, g, slot):
        for t in range(GROUP):
            q2, k2, b_lo, b_hi = seg[1](g, t)
            zero = jnp.zeros_like(q2)
            for h, (q1, bias) in enumerate(((jnp.where(lo_q, q2, zero), b_lo),
                                            (jnp.where(lo_q, zero, q2), b_hi))):
                s = lax.dot_general(q1, k2, NT_DIMS, preferred_element_type=F32) + bias
                s_scr[slot, t, h] = s
                m_scr[slot, t, h] = jnp.broadcast_to(jnp.max(s, axis=1, keepdims=True), (bq, 128))

    def stage_b(slot):
        for t in range(GROUP):
            for h in range(2):
                m = jnp.concatenate([m_scr[slot, t, h]] * (bk // 128), axis=1)
                p_scr[slot, t, h] = jnp.exp2((s_scr[slot, t, h] - m).astype(BF16))

    def stage_c(seg, g, slot):
        tiles = []
        for t in range(GROUP):
            v2 = seg[2](g, t)
            v_ones = jnp.concatenate([v2, jnp.ones_like(v2)], axis=1)
            pv0 = jnp.dot(p_scr[slot, t, 0], v_ones, preferred_element_type=F32)
            pv1 = jnp.dot(p_scr[slot, t, 1], v_ones, preferred_element_type=F32)
            tiles.append((jnp.where(lo, m_scr[slot, t, 0], m_scr[slot, t, 1]),
                          jnp.where(lo, pv0[:, 128:], pv1[:, 128:]),
                          jnp.where(lo, pv0[:, :128], pv1[:, :128])))
        seg[3](g, tiles)

    stage_a(segments[0], 0, 0)
    stage_b(0)
    stage_a(segments[0], 1, 1)
    for k, seg in enumerate(segments):
        n = seg[0]
        assert n >= 2 and n % 2 == 0

        def step(i, slot):
            stage_c(seg, i, slot)
            stage_b(1 - slot)
            stage_a(seg, i + 2, slot)

        def body(j, carry):
            step(2 * j, 0)
            step(2 * j + 1, 1)
            return carry

        if unroll:
            for j in range((n - 2) // 2):
                body(j, 0)
        else:
            lax.fori_loop(0, (n - 2) // 2, body, 0)
        nxt = segments[k + 1] if k + 1 < len(segments) else None
        stage_c(seg, n - 2, 0)
        stage_b(1)
        if nxt:
            stage_a(nxt, 0, 0)
        stage_c(seg, n - 1, 1)
        if nxt:
            stage_b(0)
            stage_a(nxt, 1, 1)


def _pipeline_scratch(bq, bk):
    return [pltpu.VMEM((2, GROUP, 2, bq, bk), F32), pltpu.VMEM((2, GROUP, 2, bq, bk), BF16),
            pltpu.VMEM((2, GROUP, 2, bq, 128), F32)]


def _merge(m_r, l_r, a_r, m_t, l_t, a_t):
    m_n = jnp.maximum(m_r, m_t)
    e_r = jnp.exp2(m_r - m_n)
    e_t = jnp.exp2(m_t - m_n)
    return m_n, l_r * e_r + l_t * e_t, a_r * e_r + a_t * e_t


def _dilated_kernel(qn_ref, kn_ref, vn_ref, qp_ref, kp_ref, vp_ref, o_ref,
                    m_ref, l_ref, a_ref, tmp_ref, mask_ref, s_scr, p_scr, m_scr, *, T):
    BQ, BK = 128, 256
    L16 = T // MAX_DIL
    row = lax.broadcasted_iota(jnp.int32, (BQ, BK), 0)
    col = lax.broadcasted_iota(jnp.int32, (BQ, BK), 1)
    band = row - col
    band4 = 4 * ((row % 32) - (col % 64)) + (row // 32 - col // 64)
    for n in range(3):
        mask_ref[n] = jnp.where(jnp.abs(band + N_SIDE * n) <= N_SIDE, 0.0, NEG_INF)
        mask_ref[3 + n] = jnp.where(jnp.abs(band4 + N_SIDE * n) <= N_SIDE, 0.0, NEG_INF)
    lane = lax.broadcasted_iota(jnp.int32, (BQ, 128), 1)
    lo = lane < HEAD_DIM
    lo_q = lane % HEAD_DIM < HEAD_DIM // 2
    n_groups = T // (BQ * GROUP)

    nib = L16 // BQ

    def rows16(g, t):
        tile = jnp.asarray(g * GROUP + t, jnp.int32)
        c = lax.div(tile, nib)
        ib = lax.rem(tile, nib)
        kst = jnp.clip(ib * BQ - N_SIDE, 0, L16 - BK)
        return (pl.multiple_of(c * L16 + ib * BQ, BQ), pl.multiple_of(c * L16 + kst, N_SIDE),
                lax.div(ib * BQ - kst, N_SIDE))

    def load_qk16(g, t):
        q0, k0, case = rows16(g, t)
        mask = mask_ref[case]
        return qp_ref[0, pl.ds(q0, BQ), :], kp_ref[0, pl.ds(k0, BK), :], mask, mask

    def load_v16(g, t):
        return vp_ref[0, pl.ds(rows16(g, t)[1], BK), :]

    def finalize16(g, tiles):
        for t, (m, l, a) in enumerate(tiles):
            q0 = rows16(g, t)[0]
            m_ref[pl.ds(q0, BQ), :] = m
            l_ref[pl.ds(q0, BQ), :] = l
            a_ref[pl.ds(q0, BQ), :] = a

    nb4 = L16 // 32

    def rows4(g, t):
        tile = jnp.asarray(g * GROUP + t, jnp.int32)
        r4 = lax.div(tile, nb4)
        i0 = lax.rem(tile, nb4) * 32
        kst = jnp.clip(i0 - 16, 0, L16 - 64)
        return ([pl.multiple_of((r4 + 4 * w) * L16 + i0, 32) for w in range(4)],
                [pl.multiple_of((r4 + 4 * w) * L16 + kst, 16) for w in range(4)],
                3 + lax.div(i0 - kst, 16))

    def load_qk4(g, t):
        qrows, krows, case = rows4(g, t)
        mask = mask_ref[case]
        return (jnp.concatenate([qp_ref[0, pl.ds(r, 32), :] for r in qrows], axis=0),
                jnp.concatenate([kp_ref[0, pl.ds(r, 64), :] for r in krows], axis=0), mask, mask)

    def load_v4(g, t):
        return jnp.concatenate([vp_ref[0, pl.ds(r, 64), :] for r in rows4(g, t)[1]], axis=0)

    def finalize4(g, tiles):
        merged = []
        for t, tile in enumerate(tiles):
            qrows = rows4(g, t)[0]
            state = [jnp.concatenate([ref[pl.ds(r, 32), :] for r in qrows], axis=0)
                     for ref in (m_ref, l_ref, a_ref)]
            merged.append((qrows, _merge(*state, *tile)))
        for qrows, new in merged:
            for ref, val in zip((m_ref, l_ref, a_ref), new):
                for w, r in enumerate(qrows):
                    ref[pl.ds(r, 32), :] = val[w * 32:(w + 1) * 32]

    def rows1(g, t):
        t0 = pl.multiple_of(jnp.asarray(g * GROUP + t, jnp.int32) * BQ, BQ)
        kst = pl.multiple_of(jnp.clip(t0 - N_SIDE, 0, T - BK), N_SIDE)
        return t0, kst, lax.div(t0 - kst, N_SIDE)

    def load_qk1(g, t):
        t0, kst, case = rows1(g, t)
        mask = mask_ref[case]
        return qn_ref[0, pl.ds(t0, BQ), :], kn_ref[0, pl.ds(kst, BK), :], mask, mask

    def load_v1(g, t):
        return vn_ref[0, pl.ds(rows1(g, t)[1], BK), :]

    def finalize1(g, tiles):
        n_i = BQ // MAX_DIL
        for t, tile in enumerate(tiles):
            t0 = rows1(g, t)[0]
            i0 = lax.div(t0, MAX_DIL)
            for c in range(MAX_DIL):
                src = pl.ds(pl.multiple_of(c * L16 + i0, n_i), n_i)
                dst = pl.ds(c, n_i, stride=MAX_DIL)
                for n, ref in enumerate((m_ref, l_ref, a_ref)):
                    tmp_ref[t, n, dst, :] = ref[src, :]
            _, l_n, a_n = _merge(tmp_ref[t, 0], tmp_ref[t, 1], tmp_ref[t, 2], *tile)
            o_ref[0, pl.ds(t0, BQ), :] = (a_n / l_n).astype(BF16)

    _attn_pipeline([(n_groups, load_qk16, load_v16, finalize16),
                    (n_groups, load_qk4, load_v4, finalize4),
                    (n_groups, load_qk1, load_v1, finalize1)], lo_q, lo, s_scr, p_scr, m_scr,
                   unroll=True)


def _dilated(qkv_nat, qkv_perm):
    B, T, _ = qkv_nat.shape
    assert T % (MAX_DIL * 128) == 0 and T % (128 * GROUP) == 0 and T // MAX_DIL >= 256
    seq = lambda col0: pl.BlockSpec((1, T, 128), lambda b, j: (b, 0, col0 + j))
    return pl.pallas_call(
        functools.partial(_dilated_kernel, T=T),
        grid=(B, N_PAIRS),
        in_specs=[seq(0), seq(N_PAIRS), seq(2 * N_PAIRS), seq(0), seq(N_PAIRS), seq(2 * N_PAIRS)],
        out_specs=pl.BlockSpec((1, T, 128), lambda b, j: (b, 0, j)),
        out_shape=jax.ShapeDtypeStruct((B, T, WIDTH_A), BF16),
        scratch_shapes=[pltpu.VMEM((T, 128), F32), pltpu.VMEM((T, 128), F32),
                        pltpu.VMEM((T, 128), F32), pltpu.VMEM((GROUP, 3, 128, 128), F32),
                        pltpu.VMEM((6, 128, 256), F32)] + _pipeline_scratch(128, 256),
        compiler_params=_params(2),
        name="dilated",
    )(qkv_nat, qkv_nat, qkv_nat, qkv_perm, qkv_perm, qkv_perm)


NBR_QROWS = 2
NBR_KROWS = NA_ROWS + 2
NBR_BQ = NBR_QROWS * GRID_W
NBR_BK = NBR_KROWS * GRID_W
NBR_VARIANTS = ((0, (0, 0)), (2, (0, 0)), (4, (0, 1)), (6, (2, 2)), (8, (2, 2)))


def _nbr_kernel(q_ref, k_ref, v_ref, bias_ref, o_ref, s_scr, p_scr, m_scr, *, T):
    rows = T // GRID_W
    lo = lax.broadcasted_iota(jnp.int32, (NBR_BQ, 128), 1) < HEAD_DIM

    def where(g, t):
        r = NBR_QROWS * jnp.asarray(g * GROUP + t, jnp.int32)
        first_key_row = jnp.clip(r - NA_ROWS // 2, 0, rows - NBR_KROWS)
        variant = lax.div(r - first_key_row, 2)
        return (pl.multiple_of(r * GRID_W, NBR_BQ), pl.multiple_of(first_key_row * GRID_W, GRID_W),
                variant)

    def load_qk(g, t):
        q0, k0, variant = where(g, t)
        return (q_ref[0, pl.ds(q0, NBR_BQ), :], k_ref[0, pl.ds(k0, NBR_BK), :],
                bias_ref[0, variant], bias_ref[1, variant])

    def load_v(g, t):
        return v_ref[0, pl.ds(where(g, t)[1], NBR_BK), :]

    def finalize(g, tiles):
        for t, (_, l, a) in enumerate(tiles):
            o_ref[0, pl.ds(where(g, t)[0], NBR_BQ), :] = (a / l).astype(BF16)

    _attn_pipeline([(rows // (NBR_QROWS * GROUP), load_qk, load_v, finalize)],
                   lo, lo, s_scr, p_scr, m_scr, unroll=False)


def _nbr(qkv_nat, bias):
    B, T, _ = qkv_nat.shape
    assert T % (NBR_BQ * GROUP) == 0 and T // GRID_W >= 2 * NBR_KROWS
    seq = lambda col0: pl.BlockSpec((1, T, 128), lambda b, j: (b, 0, col0 + j))
    return pl.pallas_call(
        functools.partial(_nbr_kernel, T=T),
        grid=(B, N_PAIRS),
        in_specs=[seq(3 * N_PAIRS), seq(4 * N_PAIRS), seq(5 * N_PAIRS),
                  pl.BlockSpec((2, len(NBR_VARIANTS), NBR_BQ, NBR_BK), lambda b, j: (j, 0, 0, 0))],
        out_specs=pl.BlockSpec((1, T, 128), lambda b, j: (b, 0, j)),
        out_shape=jax.ShapeDtypeStruct((B, T, WIDTH_B), BF16),
        scratch_shapes=_pipeline_scratch(NBR_BQ, NBR_BK),
        compiler_params=_params(2),
        name="nbr",
    )(qkv_nat, qkv_nat, qkv_nat, bias)


def _nbr_bias(rpb):
    qc = jnp.arange(GRID_W)[:, None]
    kc = jnp.arange(GRID_W)[None, :]
    cidx = jnp.clip(kc - qc, -(NA_COLS - 1), NA_COLS - 1) + NA_COLS - 1
    cs = jnp.clip(qc - NA_COLS // 2, 0, GRID_W - NA_COLS)
    col_ok = (kc >= cs) & (kc < cs + NA_COLS)
    base = jnp.where(col_ok, LOG2E * rpb.astype(F32)[:, :, cidx], NEG_INF)
    base = base.transpose(0, 2, 1, 3)
    n_heads = rpb.shape[0]
    blocks = []
    for off, first in NBR_VARIANTS:
        for q in range(NBR_QROWS):
            start = first[q] - off - q + NA_ROWS - 1
            pads = [jnp.full((n_heads, GRID_W, n, GRID_W), NEG_INF, F32)
                    for n in (first[q], NBR_KROWS - NA_ROWS - first[q])]
            rows = jnp.concatenate([pads[0], base[:, :, start:start + NA_ROWS], pads[1]], axis=2)
            blocks.append(rows.reshape(n_heads, GRID_W, NBR_BK))
    return jnp.stack(blocks, axis=1).reshape(n_heads, len(NBR_VARIANTS), NBR_BQ, NBR_BK)


UP_CHUNKS = 4


def _tail_kernel(oa_ref, ob_ref, x_ref, kv_ref, ga_ref, gb_ref, g0_ref, b0_ref, wo_ref, g1_ref, b1_ref,
                 wq_ref, wxo_ref, g2_ref, b2_ref, wu_ref, wd_ref, g3_ref, b3_ref, o_ref, x2_scr, r_scr):
    s = pl.program_id(0)
    cur, prev = lax.rem(s, 2), lax.rem(s + 1, 2)

    @pl.when(s == 0)
    def _():
        x2_scr[1] = jnp.zeros(x2_scr.shape[1:], F32)
        r_scr[...] = jnp.zeros(r_scr.shape, F32)

    x2_prev = x2_scr[prev]
    xb_prev = x2_prev.astype(BF16)
    cw = D_FF // UP_CHUNKS
    hidden = []

    def up_chunk():
        c = len(hidden)
        h = jnp.maximum(jnp.dot(xb_prev, wu_ref[:, c * cw:(c + 1) * cw],
                                preferred_element_type=F32), 0.0)
        hidden.append((h * h).astype(BF16))

    up_chunk()
    y = jnp.concatenate([_rms(oa_ref[0].astype(F32), ga_ref[...]),
                         _rms(ob_ref[0].astype(F32), gb_ref[...])], axis=1).astype(BF16)
    z = jnp.dot(y, wo_ref[...], preferred_element_type=F32)
    x1 = _ln(ALPHA * _ln(x_ref[0], g0_ref[...], b0_ref[...]) + z, g1_ref[...], b1_ref[...])
    up_chunk()
    q = jnp.dot(x1.astype(BF16), wq_ref[...], preferred_element_type=F32)
    q = (q * (HEAD_DIM_X ** -0.5)).astype(BF16)
    outs = []
    for h in range(N_HEADS_X):
        k = kv_ref[0, :, h * HEAD_DIM_X:(h + 1) * HEAD_DIM_X]
        v = kv_ref[0, :, D_MODEL + h * HEAD_DIM_X:D_MODEL + (h + 1) * HEAD_DIM_X]
        sc = lax.dot_general(q[:, h * HEAD_DIM_X:(h + 1) * HEAD_DIM_X], k, NT_DIMS,
                             preferred_element_type=F32)
        pr = jnp.exp(sc - jnp.max(sc, axis=1, keepdims=True))
        l = jnp.sum(pr, axis=1, keepdims=True)
        outs.append(jnp.dot(pr.astype(BF16), v, preferred_element_type=F32) / l)
    up_chunk()
    z = jnp.dot(jnp.concatenate(outs, axis=1).astype(BF16), wxo_ref[...], preferred_element_type=F32)
    x2_scr[cur] = _ln(ALPHA * x1 + z, g2_ref[...], b2_ref[...])
    while len(hidden) < UP_CHUNKS:
        up_chunk()
    o_ref[0] = _ln(r_scr[prev], g3_ref[...], b3_ref[...])
    z = jnp.dot(jnp.concatenate(hidden, axis=1), wd_ref[...], preferred_element_type=F32)
    r_scr[cur] = ALPHA * x2_prev + z


def _tail(oa, ob, x, kv, p, tm):
    B, T, _ = x.shape
    M = kv.shape[1]
    assert T % tm == 0
    n = T // tm
    last = B * n - 1
    vec = lambda width: _const_spec((1, width))
    mat = lambda rows, cols: _const_spec((rows, cols))

    def cur(width):
        return pl.BlockSpec((1, tm, width),
                            lambda s: (jnp.minimum(s, last) // n, jnp.minimum(s, last) % n, 0))

    return pl.pallas_call(
        _tail_kernel,
        grid=(B * n + 2,),
        in_specs=[cur(WIDTH_A), cur(WIDTH_B), cur(D_MODEL),
                  pl.BlockSpec((1, M, 2 * D_MODEL), lambda s: (jnp.minimum(s, last) // n, 0, 0)),
                  vec(WIDTH_A), vec(WIDTH_B), vec(D_MODEL), vec(D_MODEL),
                  mat(D_MODEL, D_MODEL), vec(D_MODEL), vec(D_MODEL),
                  mat(D_MODEL, D_MODEL), mat(D_MODEL, D_MODEL), vec(D_MODEL), vec(D_MODEL),
                  mat(D_MODEL, D_FF), mat(D_FF, D_MODEL), vec(D_MODEL), vec(D_MODEL)],
        out_specs=pl.BlockSpec((1, tm, D_MODEL),
                               lambda s: (jnp.maximum(s - 2, 0) // n, jnp.maximum(s - 2, 0) % n, 0)),
        out_shape=jax.ShapeDtypeStruct((B, T, D_MODEL), F32),
        scratch_shapes=[pltpu.VMEM((2, tm, D_MODEL), F32), pltpu.VMEM((2, tm, D_MODEL), F32)],
        compiler_params=_params(1),
        name="tail",
    )(oa, ob, x, kv, p["g_mix_a"], p["g_mix_b"], p["ln_in_g"], p["ln_in_b"], p["w_out"],
      p["ln1_g"], p["ln1_b"], p["w_xq"], p["w_xo"], p["ln2_g"], p["ln2_b"],
      p["w_up"], p["w_down"], p["ln3_g"], p["ln3_b"])


def _rope_tables(T):
    half = HEAD_DIM // 2
    inv = ROPE_THETA ** (-jnp.arange(half, dtype=F32) / half)
    ang = jnp.arange(T, dtype=F32)[:, None] * inv[None, :]
    cos, sin = jnp.cos(ang), jnp.sin(ang)
    return (jnp.concatenate([cos, cos, cos, cos], axis=1),
            jnp.concatenate([-sin, -sin, sin, sin], axis=1))


def _pair_rotary_layout(w_in):
    half = HEAD_DIM // 2
    idx = jnp.arange(128).reshape(4, half)[jnp.array([0, 2, 1, 3])].reshape(128)
    cols = (jnp.arange(2 * WIDTH_A // 128)[:, None] * 128 + idx[None, :]).reshape(-1)
    return jnp.concatenate([w_in[:, cols], w_in[:, 2 * WIDTH_A:]], axis=1)


QKV_TILE = 1024
TAIL_TILE = 512


def _trunk(x, mem, p):
    B, T, _ = x.shape
    kv = _mem_kv(mem, p["w_xkv"])
    qkv_nat, qkv_perm = _qkv(x, p["ln_in_g"], p["ln_in_b"], p["w_in"], p["cos"], p["sin"], QKV_TILE)
    oa = _dilated(qkv_nat, qkv_perm.reshape(B, T, 3 * WIDTH_A))
    ob = _nbr(qkv_nat, p["bias"])
    return _tail(oa, ob, x, kv, p, TAIL_TILE)


def kernel(x_prompt, x_sample, mem_prompt, mem_sample, ln_in_g, ln_in_b, w_in, rpb, g_mix_a, g_mix_b,
           w_out, ln1_g, ln1_b, w_xq, w_xkv, w_xo, ln2_g, ln2_b, w_up, w_down, ln3_g, ln3_b):
    assert w_in.shape[0] == 1, "single-layer trunk"
    row = lambda v: v.reshape(1, -1).astype(F32)
    cos, sin = _rope_tables(max(x_prompt.shape[1], x_sample.shape[1]))
    p = dict(
        cos=cos, sin=sin, ln_in_g=row(ln_in_g), ln_in_b=row(ln_in_b),
        w_in=_pair_rotary_layout(w_in[0].astype(BF16)), bias=_nbr_bias(rpb[0]),
        g_mix_a=row(g_mix_a[0]), g_mix_b=row(g_mix_b[0]), w_out=w_out[0].astype(BF16),
        ln1_g=row(ln1_g[0]), ln1_b=row(ln1_b[0]),
        w_xq=w_xq[0].astype(BF16), w_xkv=w_xkv[0].astype(BF16), w_xo=w_xo[0].astype(BF16),
        ln2_g=row(ln2_g[0]), ln2_b=row(ln2_b[0]),
        w_up=w_up[0].astype(BF16), w_down=w_down[0].astype(BF16),
        ln3_g=row(ln3_g[0]), ln3_b=row(ln3_b[0]),
    )
    return _trunk(x_prompt, mem_prompt, p), _trunk(x_sample, mem_sample, p)
```

```python
import functools

import jax
import jax.numpy as jnp
from jax import lax
from jax.experimental import pallas as pl
from jax.experimental.pallas import tpu as pltpu

F32 = jnp.float32
BF16 = jnp.bfloat16

D_MODEL = 1024
HEAD_DIM = 64
WIDTH_A = 512
WIDTH_B = 512
N_PAIRS = WIDTH_A // 128
GRID_W = 64
NA_ROWS = 8
NA_COLS = 16
N_HEADS_X = 4
HEAD_DIM_X = 256
D_FF = 4096
ROPE_THETA = 10000.0
LN_EPS = 1e-5
ALPHA = 2.0 ** 0.25
NEG_INF = -1e30
LOG2E = 1.4426950408889634
N_SIDE = 64
MAX_DIL = 16

VMEM_LIMIT = 56 * 1024 * 1024

NT_DIMS = (((1,), (1,)), ((), ()))


def _ln(x, g, b):
    mu = jnp.mean(x, axis=-1, keepdims=True)
    xc = x - mu
    var = jnp.mean(xc * xc, axis=-1, keepdims=True)
    return xc * lax.rsqrt(var + LN_EPS) * g + b


def _rms(x, g):
    return x * lax.rsqrt(jnp.mean(x * x, axis=-1, keepdims=True) + LN_EPS) * g


def _params(n_axes):
    return pltpu.CompilerParams(dimension_semantics=("arbitrary",) * n_axes,
                                vmem_limit_bytes=VMEM_LIMIT)


def _const_spec(shape):
    nd = len(shape)
    return pl.BlockSpec(shape, lambda *_: (0,) * nd, pipeline_mode=pl.Buffered(1))


def _mem_kv_kernel(mem_ref, w_ref, o_ref):
    o_ref[0] = jnp.dot(mem_ref[0].astype(BF16), w_ref[...],
                       preferred_element_type=F32).astype(BF16)


def _mem_kv(mem, w_xkv):
    B, M, _ = mem.shape
    return pl.pallas_call(
        _mem_kv_kernel,
        grid=(B,),
        in_specs=[pl.BlockSpec((1, M, D_MODEL), lambda b: (b, 0, 0)),
                  _const_spec((D_MODEL, 2 * D_MODEL))],
        out_specs=pl.BlockSpec((1, M, 2 * D_MODEL), lambda b: (b, 0, 0)),
        out_shape=jax.ShapeDtypeStruct((B, M, 2 * D_MODEL), BF16),
        compiler_params=_params(1),
        name="mem_kv",
    )(mem, w_xkv)


PERM_SUB = 256


def _qkv_kernel(x_ref, g_ref, b_ref, w_ref, cos_ref, sin_ref, nat_ref, perm_ref, *, tm):
    xb = _ln(x_ref[0], g_ref[...], b_ref[...]).astype(BF16)
    cos = jnp.concatenate([cos_ref[...]] * 4, axis=1)
    sin = jnp.concatenate([sin_ref[...]] * 4, axis=1)
    row = lax.broadcasted_iota(jnp.int32, (PERM_SUB, PERM_SUB), 0)
    col = lax.broadcasted_iota(jnp.int32, (PERM_SUB, PERM_SUB), 1)
    rows_per = PERM_SUB // MAX_DIL
    perm_mat = jnp.where(col == MAX_DIL * (row % rows_per) + row // rows_per, 1.0, 0.0).astype(BF16)
    for part in (0, 3, 1, 4, 2, 5):
        cols = slice(part * 512, (part + 1) * 512)
        r = jnp.dot(xb, w_ref[:, cols], preferred_element_type=F32)
        if part in (0, 1):
            swapped = jnp.concatenate([pltpu.roll(r[:, c:c + 128], 64, 1)
                                       for c in range(0, WIDTH_A, 128)], axis=1)
            r = r * cos + swapped * sin
        if part in (0, 3):
            r = r * (HEAD_DIM ** -0.5 * LOG2E)
        rb = r.astype(BF16)
        if part > 0:
            nat_ref[0, :, cols] = rb
        if part < 3:
            for g in range(tm // PERM_SUB):
                moved = jnp.dot(perm_mat, rb[g * PERM_SUB:(g + 1) * PERM_SUB],
                                preferred_element_type=F32)
                for c in range(MAX_DIL):
                    perm_ref[0, c, g * rows_per:(g + 1) * rows_per, cols] = (
                        moved[c * rows_per:(c + 1) * rows_per].astype(BF16))
                if part == 0:
                    half = rows_per // 2
                    for h in range(2):
                        tile = jnp.concatenate(
                            [moved[c * rows_per + h * half:c * rows_per + (h + 1) * half]
                             for c in range(MAX_DIL)], axis=0)
                        row0 = g * PERM_SUB + h * (PERM_SUB // 2)
                        nat_ref[0, row0:row0 + PERM_SUB // 2, cols] = tile.astype(BF16)


def _qkv(x, ln_g, ln_b, w_in, cos, sin, tm):
    B, T, _ = x.shape
    L16 = T // MAX_DIL
    assert tm % PERM_SUB == 0 and T % tm == 0
    return pl.pallas_call(
        functools.partial(_qkv_kernel, tm=tm),
        grid=(B, T // tm),
        in_specs=[pl.BlockSpec((1, tm, D_MODEL), lambda b, i: (b, i, 0)),
                  _const_spec((1, D_MODEL)), _const_spec((1, D_MODEL)),
                  _const_spec((D_MODEL, 3 * D_MODEL)),
                  pl.BlockSpec((tm, 128), lambda b, i: (i, 0)),
                  pl.BlockSpec((tm, 128), lambda b, i: (i, 0))],
        out_specs=[pl.BlockSpec((1, tm, 3 * D_MODEL), lambda b, i: (b, i, 0)),
                   pl.BlockSpec((1, MAX_DIL, tm // MAX_DIL, 3 * WIDTH_A), lambda b, i: (b, 0, i, 0))],
        out_shape=[jax.ShapeDtypeStruct((B, T, 3 * D_MODEL), BF16),
                   jax.ShapeDtypeStruct((B, MAX_DIL, L16, 3 * WIDTH_A), BF16)],
        compiler_params=_params(2),
        name="qkv",
    )(x, ln_g, ln_b, w_in, cos, sin)


GROUP = 8


def _attn_pipeline(segments, lo_q, lo, s_scr, p_scr, m_scr, unroll):
    bq, bk = s_scr.shape[-2:]

    def stage_a(seg, g, slot):
        for t in range(GROUP):
            q2, k2, b_lo, b_hi = seg[1](g, t)
            zero = jnp.zeros_like(q2)
            for h, (q1, bias) in enumerate(((jnp.where(lo_q, q2, zero), b_lo),
                                            (jnp.where(lo_q, zero, q2), b_hi))):
                s = lax.dot_general(q1, k2, NT_DIMS, preferred_element_type=F32) + bias
                s_scr[slot, t, h] = s
                m_scr[slot, t, h] = jnp.broadcast_to(jnp.max(s, axis=1, keepdims=True), (bq, 128))

    def stage_b(slot):
        for t in range(GROUP):
            for h in range(2):
                m = jnp.concatenate([m_scr[slot, t, h]] * (bk // 128), axis=1)
                p_scr[slot, t, h] = jnp.exp2((s_scr[slot, t, h] - m).astype(BF16))

    def stage_c(seg, g, slot):
        tiles = []
        for t in range(GROUP):
            v2 = seg[2](g, t)
            v_ones = jnp.concatenate([v2, jnp.ones_like(v2)], axis=1)
            pv0 = jnp.dot(p_scr[slot, t, 0], v_ones, preferred_element_type=F32)
            pv1 = jnp.dot(p_scr[slot, t, 1], v_ones, preferred_element_type=F32)
            tiles.append((jnp.where(lo, m_scr[slot, t, 0], m_scr[slot, t, 1]),
                          jnp.where(lo, pv0[:, 128:], pv1[:, 128:]),
                          jnp.where(lo, pv0[:, :128], pv1[:, :128])))
        seg[3](g, tiles)

    stage_a(segments[0], 0, 0)
    stage_b(0)
    stage_a(segments[0], 1, 1)
    for k, seg in enumerate(segments):
        n = seg[0]
        assert n >= 2 and n % 2 == 0

        def step(i, slot):
            stage_c(seg, i, slot)
            stage_b(1 - slot)
            stage_a(seg, i + 2, slot)

        def body(j, carry):
            step(2 * j, 0)
            step(2 * j + 1, 1)
            return carry

        if unroll:
            for j in range((n - 2) // 2):
                body(j, 0)
        else:
            lax.fori_loop(0, (n - 2) // 2, body, 0)
        nxt = segments[k + 1] if k + 1 < len(segments) else None
        stage_c(seg, n - 2, 0)
        stage_b(1)
        if nxt:
            stage_a(nxt, 0, 0)
        stage_c(seg, n - 1, 1)
        if nxt:
            stage_b(0)
            stage_a(nxt, 1, 1)


def _pipeline_scratch(bq, bk):
    return [pltpu.VMEM((2, GROUP, 2, bq, bk), F32), pltpu.VMEM((2, GROUP, 2, bq, bk), BF16),
            pltpu.VMEM((2, GROUP, 2, bq, 128), F32)]


def _merge(m_r, l_r, a_r, m_t, l_t, a_t):
    m_n = jnp.maximum(m_r, m_t)
    e_r = jnp.exp2(m_r - m_n)
    e_t = jnp.exp2(m_t - m_n)
    return m_n, l_r * e_r + l_t * e_t, a_r * e_r + a_t * e_t


def _dilated_kernel(qn_ref, kn_ref, vn_ref, qp_ref, kp_ref, vp_ref, o_ref,
                    m_ref, l_ref, a_ref, tmp_ref, mask_ref, s_scr, p_scr, m_scr, *, T):
    BQ, BK = 128, 256
    L16 = T // MAX_DIL
    row = lax.broadcasted_iota(jnp.int32, (BQ, BK), 0)
    col = lax.broadcasted_iota(jnp.int32, (BQ, BK), 1)
    band = row - col
    band4 = 4 * ((row % 32) - (col % 64)) + (row // 32 - col // 64)
    rows_per = BQ // MAX_DIL
    band1 = MAX_DIL * (row % rows_per) + row // rows_per - col
    for n in range(3):
        mask_ref[n] = jnp.where(jnp.abs(band + N_SIDE * n) <= N_SIDE, 0.0, NEG_INF)
        mask_ref[3 + n] = jnp.where(jnp.abs(band4 + N_SIDE * n) <= N_SIDE, 0.0, NEG_INF)
        mask_ref[6 + n] = jnp.where(jnp.abs(band1 + N_SIDE * n) <= N_SIDE, 0.0, NEG_INF)
    lane = lax.broadcasted_iota(jnp.int32, (BQ, 128), 1)
    lo = lane < HEAD_DIM
    lo_q = lane % HEAD_DIM < HEAD_DIM // 2
    n_groups = T // (BQ * GROUP)

    nib = L16 // BQ

    def rows16(g, t):
        tile = jnp.asarray(g * GROUP + t, jnp.int32)
        c = lax.div(tile, nib)
        ib = lax.rem(tile, nib)
        kst = jnp.clip(ib * BQ - N_SIDE, 0, L16 - BK)
        return (pl.multiple_of(c * L16 + ib * BQ, BQ), pl.multiple_of(c * L16 + kst, N_SIDE),
                lax.div(ib * BQ - kst, N_SIDE))

    def load_qk16(g, t):
        q0, k0, case = rows16(g, t)
        mask = mask_ref[case]
        return qp_ref[0, pl.ds(q0, BQ), :], kp_ref[0, pl.ds(k0, BK), :], mask, mask

    def load_v16(g, t):
        return vp_ref[0, pl.ds(rows16(g, t)[1], BK), :]

    def finalize16(g, tiles):
        for t, (m, l, a) in enumerate(tiles):
            q0 = rows16(g, t)[0]
            m_ref[pl.ds(q0, BQ), :] = m
            l_ref[pl.ds(q0, BQ), :] = l
            a_ref[pl.ds(q0, BQ), :] = a

    nb4 = L16 // 32

    def rows4(g, t):
        tile = jnp.asarray(g * GROUP + t, jnp.int32)
        r4 = lax.div(tile, nb4)
        i0 = lax.rem(tile, nb4) * 32
        kst = jnp.clip(i0 - 16, 0, L16 - 64)
        return ([pl.multiple_of((r4 + 4 * w) * L16 + i0, 32) for w in range(4)],
                [pl.multiple_of((r4 + 4 * w) * L16 + kst, 16) for w in range(4)],
                3 + lax.div(i0 - kst, 16))

    def load_qk4(g, t):
        qrows, krows, case = rows4(g, t)
        mask = mask_ref[case]
        return (jnp.concatenate([qp_ref[0, pl.ds(r, 32), :] for r in qrows], axis=0),
                jnp.concatenate([kp_ref[0, pl.ds(r, 64), :] for r in krows], axis=0), mask, mask)

    def load_v4(g, t):
        return jnp.concatenate([vp_ref[0, pl.ds(r, 64), :] for r in rows4(g, t)[1]], axis=0)

    def finalize4(g, tiles):
        merged = []
        for t, tile in enumerate(tiles):
            qrows = rows4(g, t)[0]
            state = [jnp.concatenate([ref[pl.ds(r, 32), :] for r in qrows], axis=0)
                     for ref in (m_ref, l_ref, a_ref)]
            merged.append((qrows, _merge(*state, *tile)))
        for qrows, new in merged:
            for ref, val in zip((m_ref, l_ref, a_ref), new):
                for w, r in enumerate(qrows):
                    ref[pl.ds(r, 32), :] = val[w * 32:(w + 1) * 32]

    def rows1(g, t):
        t0 = pl.multiple_of(jnp.asarray(g * GROUP + t, jnp.int32) * BQ, BQ)
        kst = pl.multiple_of(jnp.clip(t0 - N_SIDE, 0, T - BK), N_SIDE)
        return t0, kst, 6 + lax.div(t0 - kst, N_SIDE)

    def load_qk1(g, t):
        t0, kst, case = rows1(g, t)
        mask = mask_ref[case]
        return qn_ref[0, pl.ds(t0, BQ), :], kn_ref[0, pl.ds(kst, BK), :], mask, mask

    def load_v1(g, t):
        return vn_ref[0, pl.ds(rows1(g, t)[1], BK), :]

    def finalize1(g, tiles):
        for t, tile in enumerate(tiles):
            t0 = rows1(g, t)[0]
            i0 = lax.div(t0, MAX_DIL)
            state = [jnp.concatenate([ref[pl.ds(pl.multiple_of(c * L16 + i0, rows_per), rows_per), :]
                                      for c in range(MAX_DIL)], axis=0)
                     for ref in (m_ref, l_ref, a_ref)]
            _, l_n, a_n = _merge(*state, *tile)
            out = a_n / l_n
            for c in range(MAX_DIL):
                tmp_ref[t, pl.ds(c, rows_per, stride=MAX_DIL), :] = out[c * rows_per:(c + 1) * rows_per]
            o_ref[0, pl.ds(t0, BQ), :] = tmp_ref[t].astype(BF16)

    _attn_pipeline([(n_groups, load_qk16, load_v16, finalize16),
                    (n_groups, load_qk4, load_v4, finalize4),
                    (n_groups, load_qk1, load_v1, finalize1)], lo_q, lo, s_scr, p_scr, m_scr,
                   unroll=True)


def _dilated(qkv_nat, qkv_perm):
    B, T, _ = qkv_nat.shape
    assert T % (MAX_DIL * 128) == 0 and T % (128 * GROUP) == 0 and T // MAX_DIL >= 256
    seq = lambda col0: pl.BlockSpec((1, T, 128), lambda b, j: (b, 0, col0 + j))
    return pl.pallas_call(
        functools.partial(_dilated_kernel, T=T),
        grid=(B, N_PAIRS),
        in_specs=[seq(0), seq(N_PAIRS), seq(2 * N_PAIRS), seq(0), seq(N_PAIRS), seq(2 * N_PAIRS)],
        out_specs=pl.BlockSpec((1, T, 128), lambda b, j: (b, 0, j)),
        out_shape=jax.ShapeDtypeStruct((B, T, WIDTH_A), BF16),
        scratch_shapes=[pltpu.VMEM((T, 128), F32), pltpu.VMEM((T, 128), F32),
                        pltpu.VMEM((T, 128), F32), pltpu.VMEM((GROUP, 128, 128), F32),
                        pltpu.VMEM((9, 128, 256), F32)] + _pipeline_scratch(128, 256),
        compiler_params=_params(2),
        name="dilated",
    )(qkv_nat, qkv_nat, qkv_nat, qkv_perm, qkv_perm, qkv_perm)


NBR_QROWS = 2
NBR_KROWS = NA_ROWS + 2
NBR_BQ = NBR_QROWS * GRID_W
NBR_BK = NBR_KROWS * GRID_W
NBR_VARIANTS = ((0, (0, 0)), (2, (0, 0)), (4, (0, 1)), (6, (2, 2)), (8, (2, 2)))


def _nbr_kernel(q_ref, k_ref, v_ref, bias_ref, o_ref, s_scr, p_scr, m_scr, *, T):
    rows = T // GRID_W
    lo = lax.broadcasted_iota(jnp.int32, (NBR_BQ, 128), 1) < HEAD_DIM

    def where(g, t):
        r = NBR_QROWS * jnp.asarray(g * GROUP + t, jnp.int32)
        first_key_row = jnp.clip(r - NA_ROWS // 2, 0, rows - NBR_KROWS)
        variant = lax.div(r - first_key_row, 2)
        return (pl.multiple_of(r * GRID_W, NBR_BQ), pl.multiple_of(first_key_row * GRID_W, GRID_W),
                variant)

    def load_qk(g, t):
        q0, k0, variant = where(g, t)
        return (q_ref[0, pl.ds(q0, NBR_BQ), :], k_ref[0, pl.ds(k0, NBR_BK), :],
                bias_ref[0, variant], bias_ref[1, variant])

    def load_v(g, t):
        return v_ref[0, pl.ds(where(g, t)[1], NBR_BK), :]

    def finalize(g, tiles):
        for t, (_, l, a) in enumerate(tiles):
            o_ref[0, pl.ds(where(g, t)[0], NBR_BQ), :] = (a / l).astype(BF16)

    _attn_pipeline([(rows // (NBR_QROWS * GROUP), load_qk, load_v, finalize)],
                   lo, lo, s_scr, p_scr, m_scr, unroll=False)


def _nbr(qkv_nat, bias):
    B, T, _ = qkv_nat.shape
    assert T % (NBR_BQ * GROUP) == 0 and T // GRID_W >= 2 * NBR_KROWS
    seq = lambda col0: pl.BlockSpec((1, T, 128), lambda b, j: (b, 0, col0 + j))
    return pl.pallas_call(
        functools.partial(_nbr_kernel, T=T),
        grid=(B, N_PAIRS),
        in_specs=[seq(3 * N_PAIRS), seq(4 * N_PAIRS), seq(5 * N_PAIRS),
                  pl.BlockSpec((2, len(NBR_VARIANTS), NBR_BQ, NBR_BK), lambda b, j: (j, 0, 0, 0))],
        out_specs=pl.BlockSpec((1, T, 128), lambda b, j: (b, 0, j)),
        out_shape=jax.ShapeDtypeStruct((B, T, WIDTH_B), BF16),
        scratch_shapes=_pipeline_scratch(NBR_BQ, NBR_BK),
        compiler_params=_params(2),
        name="nbr",
    )(qkv_nat, qkv_nat, qkv_nat, bias)


def _nbr_bias(rpb):
    qc = jnp.arange(GRID_W)[:, None]
    kc = jnp.arange(GRID_W)[None, :]
    cidx = jnp.clip(kc - qc, -(NA_COLS - 1), NA_COLS - 1) + NA_COLS - 1
    cs = jnp.clip(qc - NA_COLS // 2, 0, GRID_W - NA_COLS)
    col_ok = (kc >= cs) & (kc < cs + NA_COLS)
    scaled = (LOG2E * rpb.astype(F32))[:, None, :, None, :]
    base = sum(jnp.where((cidx == d)[None, :, None, :], scaled[..., d], 0.0)
               for d in range(2 * NA_COLS - 1))
    base = jnp.where(col_ok[None, :, None, :], base, NEG_INF)
    n_heads = rpb.shape[0]
    blocks = []
    for off, first in NBR_VARIANTS:
        for q in range(NBR_QROWS):
            start = first[q] - off - q + NA_ROWS - 1
            pads = [jnp.full((n_heads, GRID_W, n, GRID_W), NEG_INF, F32)
                    for n in (first[q], NBR_KROWS - NA_ROWS - first[q])]
            rows = jnp.concatenate([pads[0], base[:, :, start:start + NA_ROWS], pads[1]], axis=2)
            blocks.append(rows.reshape(n_heads, GRID_W, NBR_BK))
    return jnp.stack(blocks, axis=1).reshape(n_heads, len(NBR_VARIANTS), NBR_BQ, NBR_BK)


UP_CHUNKS = 4


def _tail_kernel(oa_ref, ob_ref, x_ref, kv_ref, ga_ref, gb_ref, g0_ref, b0_ref, wo_ref, g1_ref, b1_ref,
                 wq_ref, wxo_ref, g2_ref, b2_ref, wu_ref, wd_ref, g3_ref, b3_ref, o_ref, x2_scr, r_scr):
    s = pl.program_id(0)
    cur, prev = lax.rem(s, 2), lax.rem(s + 1, 2)

    @pl.when(s == 0)
    def _():
        x2_scr[1] = jnp.zeros(x2_scr.shape[1:], F32)
        r_scr[...] = jnp.zeros(r_scr.shape, F32)

    x2_prev = x2_scr[prev]
    xb_prev = x2_prev.astype(BF16)
    cw = D_FF // UP_CHUNKS
    hidden = []

    def up_chunk():
        c = len(hidden)
        h = jnp.maximum(jnp.dot(xb_prev, wu_ref[:, c * cw:(c + 1) * cw],
                                preferred_element_type=F32), 0.0)
        hidden.append((h * h).astype(BF16))

    up_chunk()
    y = jnp.concatenate([_rms(oa_ref[0].astype(F32), ga_ref[...]),
                         _rms(ob_ref[0].astype(F32), gb_ref[...])], axis=1).astype(BF16)
    z = jnp.dot(y, wo_ref[...], preferred_element_type=F32)
    x1 = _ln(ALPHA * _ln(x_ref[0], g0_ref[...], b0_ref[...]) + z, g1_ref[...], b1_ref[...])
    up_chunk()
    q = jnp.dot(x1.astype(BF16), wq_ref[...], preferred_element_type=F32)
    q = (q * (HEAD_DIM_X ** -0.5)).astype(BF16)
    outs = []
    for h in range(N_HEADS_X):
        k = kv_ref[0, :, h * HEAD_DIM_X:(h + 1) * HEAD_DIM_X]
        v = kv_ref[0, :, D_MODEL + h * HEAD_DIM_X:D_MODEL + (h + 1) * HEAD_DIM_X]
        sc = lax.dot_general(q[:, h * HEAD_DIM_X:(h + 1) * HEAD_DIM_X], k, NT_DIMS,
                             preferred_element_type=F32)
        pr = jnp.exp(sc - jnp.max(sc, axis=1, keepdims=True))
        l = jnp.sum(pr, axis=1, keepdims=True)
        outs.append(jnp.dot(pr.astype(BF16), v, preferred_element_type=F32) / l)
    up_chunk()
    z = jnp.dot(jnp.concatenate(outs, axis=1).astype(BF16), wxo_ref[...], preferred_element_type=F32)
    x2_scr[cur] = _ln(ALPHA * x1 + z, g2_ref[...], b2_ref[...])
    while len(hidden) < UP_CHUNKS:
        up_chunk()
    o_ref[0] = _ln(r_scr[prev], g3_ref[...], b3_ref[...])
    z = jnp.dot(jnp.concatenate(hidden, axis=1), wd_ref[...], preferred_element_type=F32)
    r_scr[cur] = ALPHA * x2_prev + z


def _tail(oa, ob, x, kv, p, tm):
    B, T, _ = x.shape
    M = kv.shape[1]
    assert T % tm == 0
    n = T // tm
    last = B * n - 1
    vec = lambda width: _const_spec((1, width))
    mat = lambda rows, cols: _const_spec((rows, cols))

    def cur(width):
        return pl.BlockSpec((1, tm, width),
                            lambda s: (jnp.minimum(s, last) // n, jnp.minimum(s, last) % n, 0))

    return pl.pallas_call(
        _tail_kernel,
        grid=(B * n + 2,),
        in_specs=[cur(WIDTH_A), cur(WIDTH_B), cur(D_MODEL),
                  pl.BlockSpec((1, M, 2 * D_MODEL), lambda s: (jnp.minimum(s, last) // n, 0, 0)),
                  vec(WIDTH_A), vec(WIDTH_B), vec(D_MODEL), vec(D_MODEL),
                  mat(D_MODEL, D_MODEL), vec(D_MODEL), vec(D_MODEL),
                  mat(D_MODEL, D_MODEL), mat(D_MODEL, D_MODEL), vec(D_MODEL), vec(D_MODEL),
                  mat(D_MODEL, D_FF), mat(D_FF, D_MODEL), vec(D_MODEL), vec(D_MODEL)],
        out_specs=pl.BlockSpec((1, tm, D_MODEL),
                               lambda s: (jnp.maximum(s - 2, 0) // n, jnp.maximum(s - 2, 0) % n, 0)),
        out_shape=jax.ShapeDtypeStruct((B, T, D_MODEL), F32),
        scratch_shapes=[pltpu.VMEM((2, tm, D_MODEL), F32), pltpu.VMEM((2, tm, D_MODEL), F32)],
        compiler_params=_params(1),
        name="tail",
    )(oa, ob, x, kv, p["g_mix_a"], p["g_mix_b"], p["ln_in_g"], p["ln_in_b"], p["w_out"],
      p["ln1_g"], p["ln1_b"], p["w_xq"], p["w_xo"], p["ln2_g"], p["ln2_b"],
      p["w_up"], p["w_down"], p["ln3_g"], p["ln3_b"])


def _rope_tables(T):
    half = HEAD_DIM // 2
    inv = ROPE_THETA ** (-jnp.arange(half, dtype=F32) / half)
    ang = jnp.arange(T, dtype=F32)[:, None] * inv[None, :]
    cos, sin = jnp.cos(ang), jnp.sin(ang)
    return (jnp.concatenate([cos, cos, cos, cos], axis=1),
            jnp.concatenate([-sin, -sin, sin, sin], axis=1))


def _pair_rotary_layout(w_in):
    half = HEAD_DIM // 2
    idx = jnp.arange(128).reshape(4, half)[jnp.array([0, 2, 1, 3])].reshape(128)
    cols = (jnp.arange(2 * WIDTH_A // 128)[:, None] * 128 + idx[None, :]).reshape(-1)
    return jnp.concatenate([w_in[:, cols], w_in[:, 2 * WIDTH_A:]], axis=1)


QKV_TILE = 1024
TAIL_TILE = 512


def _trunk(x, mem, p):
    B, T, _ = x.shape
    kv = _mem_kv(mem, p["w_xkv"])
    qkv_nat, qkv_perm = _qkv(x, p["ln_in_g"], p["ln_in_b"], p["w_in"], p["cos"], p["sin"], QKV_TILE)
    oa = _dilated(qkv_nat, qkv_perm.reshape(B, T, 3 * WIDTH_A))
    ob = _nbr(qkv_nat, p["bias"])
    return _tail(oa, ob, x, kv, p, TAIL_TILE)


def kernel(x_prompt, x_sample, mem_prompt, mem_sample, ln_in_g, ln_in_b, w_in, rpb, g_mix_a, g_mix_b,
           w_out, ln1_g, ln1_b, w_xq, w_xkv, w_xo, ln2_g, ln2_b, w_up, w_down, ln3_g, ln3_b):
    assert w_in.shape[0] == 1, "single-layer trunk"
    row = lambda v: v.reshape(1, -1).astype(F32)
    cos, sin = _rope_tables(max(x_prompt.shape[1], x_sample.shape[1]))
    p = dict(
        cos=cos, sin=sin, ln_in_g=row(ln_in_g), ln_in_b=row(ln_in_b),
        w_in=_pair_rotary_layout(w_in[0].astype(BF16)), bias=_nbr_bias(rpb[0]),
        g_mix_a=row(g_mix_a[0]), g_mix_b=row(g_mix_b[0]), w_out=w_out[0].astype(BF16),
        ln1_g=row(ln1_g[0]), ln1_b=row(ln1_b[0]),
        w_xq=w_xq[0].astype(BF16), w_xkv=w_xkv[0].astype(BF16), w_xo=w_xo[0].astype(BF16),
        ln2_g=row(ln2_g[0]), ln2_b=row(ln2_b[0]),
        w_up=w_up[0].astype(BF16), w_down=w_down[0].astype(BF16),
        ln3_g=row(ln3_g[0]), ln3_b=row(ln3_b[0]),
    )
    return _trunk(x_prompt, mem_prompt, p), _trunk(x_sample, mem_sample, p)
```

```python
import functools

import jax
import jax.numpy as jnp
from jax import lax
from jax.experimental import pallas as pl
from jax.experimental.pallas import tpu as pltpu

F32 = jnp.float32
BF16 = jnp.bfloat16

D_MODEL = 1024
HEAD_DIM = 64
WIDTH_A = 512
WIDTH_B = 512
N_PAIRS = WIDTH_A // 128
GRID_W = 64
NA_ROWS = 8
NA_COLS = 16
N_HEADS_X = 4
HEAD_DIM_X = 256
D_FF = 4096
ROPE_THETA = 10000.0
LN_EPS = 1e-5
ALPHA = 2.0 ** 0.25
NEG_INF = -1e30
LOG2E = 1.4426950408889634
N_SIDE = 64
MAX_DIL = 16

VMEM_LIMIT = 56 * 1024 * 1024

NT_DIMS = (((1,), (1,)), ((), ()))


def _ln(x, g, b):
    mu = jnp.mean(x, axis=-1, keepdims=True)
    xc = x - mu
    var = jnp.mean(xc * xc, axis=-1, keepdims=True)
    return xc * lax.rsqrt(var + LN_EPS) * g + b


def _rms(x, g):
    return x * lax.rsqrt(jnp.mean(x * x, axis=-1, keepdims=True) + LN_EPS) * g


def _params(n_axes):
    return pltpu.CompilerParams(dimension_semantics=("arbitrary",) * n_axes,
                                vmem_limit_bytes=VMEM_LIMIT)


def _const_spec(shape):
    nd = len(shape)
    return pl.BlockSpec(shape, lambda *_: (0,) * nd, pipeline_mode=pl.Buffered(1))


def _mem_kv_kernel(mem_ref, w_ref, o_ref):
    o_ref[0] = jnp.dot(mem_ref[0].astype(BF16), w_ref[...],
                       preferred_element_type=F32).astype(BF16)


def _mem_kv(mem, w_xkv):
    B, M, _ = mem.shape
    return pl.pallas_call(
        _mem_kv_kernel,
        grid=(B,),
        in_specs=[pl.BlockSpec((1, M, D_MODEL), lambda b: (b, 0, 0)),
                  _const_spec((D_MODEL, 2 * D_MODEL))],
        out_specs=pl.BlockSpec((1, M, 2 * D_MODEL), lambda b: (b, 0, 0)),
        out_shape=jax.ShapeDtypeStruct((B, M, 2 * D_MODEL), BF16),
        compiler_params=_params(1),
        name="mem_kv",
    )(mem, w_xkv)


PERM_SUB = 256


def _qkv_kernel(x_ref, g_ref, b_ref, w_ref, cos_ref, sin_ref, nat_ref, perm_ref, *, tm):
    xb = _ln(x_ref[0], g_ref[...], b_ref[...]).astype(BF16)
    cos = jnp.concatenate([cos_ref[...]] * 4, axis=1)
    sin = jnp.concatenate([sin_ref[...]] * 4, axis=1)
    row = lax.broadcasted_iota(jnp.int32, (PERM_SUB, PERM_SUB), 0)
    col = lax.broadcasted_iota(jnp.int32, (PERM_SUB, PERM_SUB), 1)
    rows_per = PERM_SUB // MAX_DIL
    perm_mat = jnp.where(col == MAX_DIL * (row % rows_per) + row // rows_per, 1.0, 0.0).astype(BF16)
    for part in (0, 3, 1, 4, 2, 5):
        cols = slice(part * 512, (part + 1) * 512)
        r = jnp.dot(xb, w_ref[:, cols], preferred_element_type=F32)
        if part in (0, 1):
            swapped = jnp.concatenate([pltpu.roll(r[:, c:c + 128], 64, 1)
                                       for c in range(0, WIDTH_A, 128)], axis=1)
            r = r * cos + swapped * sin
        if part in (0, 3):
            r = r * (HEAD_DIM ** -0.5 * LOG2E)
        rb = r.astype(BF16)
        if part > 0:
            nat_ref[0, :, cols] = rb
        if part < 3:
            for g in range(tm // PERM_SUB):
                moved = jnp.dot(perm_mat, rb[g * PERM_SUB:(g + 1) * PERM_SUB],
                                preferred_element_type=F32)
                for c in range(MAX_DIL):
                    perm_ref[0, c, g * rows_per:(g + 1) * rows_per, cols] = (
                        moved[c * rows_per:(c + 1) * rows_per].astype(BF16))
                if part == 0:
                    half = rows_per // 2
                    for h in range(2):
                        tile = jnp.concatenate(
                            [moved[c * rows_per + h * half:c * rows_per + (h + 1) * half]
                             for c in range(MAX_DIL)], axis=0)
                        row0 = g * PERM_SUB + h * (PERM_SUB // 2)
                        nat_ref[0, row0:row0 + PERM_SUB // 2, cols] = tile.astype(BF16)


def _qkv(x, ln_g, ln_b, w_in, cos, sin, tm):
    B, T, _ = x.shape
    L16 = T // MAX_DIL
    assert tm % PERM_SUB == 0 and T % tm == 0
    return pl.pallas_call(
        functools.partial(_qkv_kernel, tm=tm),
        grid=(B, T // tm),
        in_specs=[pl.BlockSpec((1, tm, D_MODEL), lambda b, i: (b, i, 0)),
                  _const_spec((1, D_MODEL)), _const_spec((1, D_MODEL)),
                  _const_spec((D_MODEL, 3 * D_MODEL)),
                  pl.BlockSpec((tm, 128), lambda b, i: (i, 0)),
                  pl.BlockSpec((tm, 128), lambda b, i: (i, 0))],
        out_specs=[pl.BlockSpec((1, tm, 3 * D_MODEL), lambda b, i: (b, i, 0)),
                   pl.BlockSpec((1, MAX_DIL, tm // MAX_DIL, 3 * WIDTH_A), lambda b, i: (b, 0, i, 0))],
        out_shape=[jax.ShapeDtypeStruct((B, T, 3 * D_MODEL), BF16),
                   jax.ShapeDtypeStruct((B, MAX_DIL, L16, 3 * WIDTH_A), BF16)],
        compiler_params=_params(2),
        name="qkv",
    )(x, ln_g, ln_b, w_in, cos, sin)


GROUP = 8


def _attn_pipeline(segments, lo_q, lo, s_scr, p_scr, m_scr, unroll):
    bq, bk = s_scr.shape[-2:]

    def stage_a(seg, g, slot):
        for t in range(GROUP):
            q2, k2, b_lo, b_hi = seg[1](g, t)
            zero = jnp.zeros_like(q2)
            for h, (q1, bias) in enumerate(((jnp.where(lo_q, q2, zero), b_lo),
                                            (jnp.where(lo_q, zero, q2), b_hi))):
                s = lax.dot_general(q1, k2, NT_DIMS, preferred_element_type=F32) + bias
                s_scr[slot, t, h] = s
                m_scr[slot, t, h] = jnp.broadcast_to(jnp.max(s, axis=1, keepdims=True), (bq, 128))

    def stage_b(slot):
        for t in range(GROUP):
            for h in range(2):
                m = jnp.concatenate([m_scr[slot, t, h]] * (bk // 128), axis=1)
                p_scr[slot, t, h] = jnp.exp2((s_scr[slot, t, h] - m).astype(BF16))

    def stage_c(seg, g, slot):
        tiles = []
        for t in range(GROUP):
            v2 = seg[2](g, t)
            v_ones = jnp.concatenate([v2, jnp.ones_like(v2)], axis=1)
            pv0 = jnp.dot(p_scr[slot, t, 0], v_ones, preferred_element_type=F32)
            pv1 = jnp.dot(p_scr[slot, t, 1], v_ones, preferred_element_type=F32)
            tiles.append((jnp.where(lo, m_scr[slot, t, 0], m_scr[slot, t, 1]),
                          jnp.where(lo, pv0[:, 128:], pv1[:, 128:]),
                          jnp.where(lo, pv0[:, :128], pv1[:, :128])))
        seg[3](g, tiles)

    stage_a(segments[0], 0, 0)
    stage_b(0)
    stage_a(segments[0], 1, 1)
    for k, seg in enumerate(segments):
        n = seg[0]
        assert n >= 2 and n % 2 == 0

        def step(i, slot):
            stage_c(seg, i, slot)
            stage_b(1 - slot)
            stage_a(seg, i + 2, slot)

        def body(j, carry):
            step(2 * j, 0)
            step(2 * j + 1, 1)
            return carry

        if unroll:
            for j in range((n - 2) // 2):
                body(j, 0)
        else:
            lax.fori_loop(0, (n - 2) // 2, body, 0)
        nxt = segments[k + 1] if k + 1 < len(segments) else None
        stage_c(seg, n - 2, 0)
        stage_b(1)
        if nxt:
            stage_a(nxt, 0, 0)
        stage_c(seg, n - 1, 1)
        if nxt:
            stage_b(0)
            stage_a(nxt, 1, 1)


def _pipeline_scratch(bq, bk):
    return [pltpu.VMEM((2, GROUP, 2, bq, bk), F32), pltpu.VMEM((2, GROUP, 2, bq, bk), BF16),
            pltpu.VMEM((2, GROUP, 2, bq, 128), F32)]


def _merge(m_r, l_r, a_r, m_t, l_t, a_t):
    m_n = jnp.maximum(m_r, m_t)
    e_r = jnp.exp2(m_r - m_n)
    e_t = jnp.exp2(m_t - m_n)
    return m_n, l_r * e_r + l_t * e_t, a_r * e_r + a_t * e_t


def _dilated_kernel(qn_ref, kn_ref, vn_ref, qp_ref, kp_ref, vp_ref, o_ref,
                    m_ref, l_ref, a_ref, mask_ref, s_scr, p_scr, m_scr, *, T):
    BQ, BK = 128, 256
    L16 = T // MAX_DIL
    row = lax.broadcasted_iota(jnp.int32, (BQ, BK), 0)
    col = lax.broadcasted_iota(jnp.int32, (BQ, BK), 1)
    band = row - col
    band4 = 4 * ((row % 32) - (col % 64)) + (row // 32 - col // 64)
    rows_per = BQ // MAX_DIL
    band1 = MAX_DIL * (row % rows_per) + row // rows_per - col
    for n in range(3):
        mask_ref[n] = jnp.where(jnp.abs(band + N_SIDE * n) <= N_SIDE, 0.0, NEG_INF)
        mask_ref[3 + n] = jnp.where(jnp.abs(band4 + N_SIDE * n) <= N_SIDE, 0.0, NEG_INF)
        mask_ref[6 + n] = jnp.where(jnp.abs(band1 + N_SIDE * n) <= N_SIDE, 0.0, NEG_INF)
    lane = lax.broadcasted_iota(jnp.int32, (BQ, 128), 1)
    lo = lane < HEAD_DIM
    lo_q = lane % HEAD_DIM < HEAD_DIM // 2
    n_groups = T // (BQ * GROUP)

    nib = L16 // BQ

    def rows16(g, t):
        tile = jnp.asarray(g * GROUP + t, jnp.int32)
        c = lax.div(tile, nib)
        ib = lax.rem(tile, nib)
        kst = jnp.clip(ib * BQ - N_SIDE, 0, L16 - BK)
        return (pl.multiple_of(c * L16 + ib * BQ, BQ), pl.multiple_of(c * L16 + kst, N_SIDE),
                lax.div(ib * BQ - kst, N_SIDE))

    def load_qk16(g, t):
        q0, k0, case = rows16(g, t)
        mask = mask_ref[case]
        return qp_ref[0, pl.ds(q0, BQ), :], kp_ref[0, pl.ds(k0, BK), :], mask, mask

    def load_v16(g, t):
        return vp_ref[0, pl.ds(rows16(g, t)[1], BK), :]

    def finalize16(g, tiles):
        for t, (m, l, a) in enumerate(tiles):
            q0 = rows16(g, t)[0]
            m_ref[pl.ds(q0, BQ), :] = m
            l_ref[pl.ds(q0, BQ), :] = l
            a_ref[pl.ds(q0, BQ), :] = a

    nb4 = L16 // 32

    def rows4(g, t):
        tile = jnp.asarray(g * GROUP + t, jnp.int32)
        r4 = lax.div(tile, nb4)
        i0 = lax.rem(tile, nb4) * 32
        kst = jnp.clip(i0 - 16, 0, L16 - 64)
        return ([pl.multiple_of((r4 + 4 * w) * L16 + i0, 32) for w in range(4)],
                [pl.multiple_of((r4 + 4 * w) * L16 + kst, 16) for w in range(4)],
                3 + lax.div(i0 - kst, 16))

    def load_qk4(g, t):
        qrows, krows, case = rows4(g, t)
        mask = mask_ref[case]
        return (jnp.concatenate([qp_ref[0, pl.ds(r, 32), :] for r in qrows], axis=0),
                jnp.concatenate([kp_ref[0, pl.ds(r, 64), :] for r in krows], axis=0), mask, mask)

    def load_v4(g, t):
        return jnp.concatenate([vp_ref[0, pl.ds(r, 64), :] for r in rows4(g, t)[1]], axis=0)

    def finalize4(g, tiles):
        merged = []
        for t, tile in enumerate(tiles):
            qrows = rows4(g, t)[0]
            state = [jnp.concatenate([ref[pl.ds(r, 32), :] for r in qrows], axis=0)
                     for ref in (m_ref, l_ref, a_ref)]
            merged.append((qrows, _merge(*state, *tile)))
        for qrows, new in merged:
            for ref, val in zip((m_ref, l_ref, a_ref), new):
                for w, r in enumerate(qrows):
                    ref[pl.ds(r, 32), :] = val[w * 32:(w + 1) * 32]

    tok = lax.broadcasted_iota(jnp.int32, (BQ, BQ), 0)
    src = lax.broadcasted_iota(jnp.int32, (BQ, BQ), 1)
    to_natural = jnp.where(src == rows_per * (tok % MAX_DIL) + tok // MAX_DIL, 1.0, 0.0).astype(BF16)
    def rows1(g, t):
        t0 = pl.multiple_of(jnp.asarray(g * GROUP + t, jnp.int32) * BQ, BQ)
        kst = pl.multiple_of(jnp.clip(t0 - N_SIDE, 0, T - BK), N_SIDE)
        return t0, kst, 6 + lax.div(t0 - kst, N_SIDE)

    def load_qk1(g, t):
        t0, kst, case = rows1(g, t)
        mask = mask_ref[case]
        return qn_ref[0, pl.ds(t0, BQ), :], kn_ref[0, pl.ds(kst, BK), :], mask, mask

    def load_v1(g, t):
        return vn_ref[0, pl.ds(rows1(g, t)[1], BK), :]

    def finalize1(g, tiles):
        for t, tile in enumerate(tiles):
            t0 = rows1(g, t)[0]
            i0 = lax.div(t0, MAX_DIL)
            state = [jnp.concatenate([ref[pl.ds(pl.multiple_of(c * L16 + i0, rows_per), rows_per), :]
                                      for c in range(MAX_DIL)], axis=0)
                     for ref in (m_ref, l_ref, a_ref)]
            _, l_n, a_n = _merge(*state, *tile)
            out = (a_n / l_n).astype(BF16)
            o_ref[0, pl.ds(t0, BQ), :] = jnp.dot(to_natural, out,
                                                 preferred_element_type=F32).astype(BF16)

    _attn_pipeline([(n_groups, load_qk16, load_v16, finalize16),
                    (n_groups, load_qk4, load_v4, finalize4),
                    (n_groups, load_qk1, load_v1, finalize1)], lo_q, lo, s_scr, p_scr, m_scr,
                   unroll=True)


def _dilated(qkv_nat, qkv_perm):
    B, T, _ = qkv_nat.shape
    assert T % (MAX_DIL * 128) == 0 and T % (128 * GROUP) == 0 and T // MAX_DIL >= 256
    seq = lambda col0: pl.BlockSpec((1, T, 128), lambda b, j: (b, 0, col0 + j))
    return pl.pallas_call(
        functools.partial(_dilated_kernel, T=T),
        grid=(B, N_PAIRS),
        in_specs=[seq(0), seq(N_PAIRS), seq(2 * N_PAIRS), seq(0), seq(N_PAIRS), seq(2 * N_PAIRS)],
        out_specs=pl.BlockSpec((1, T, 128), lambda b, j: (b, 0, j)),
        out_shape=jax.ShapeDtypeStruct((B, T, WIDTH_A), BF16),
        scratch_shapes=[pltpu.VMEM((T, 128), F32), pltpu.VMEM((T, 128), F32),
                        pltpu.VMEM((T, 128), F32),
                        pltpu.VMEM((9, 128, 256), F32)] + _pipeline_scratch(128, 256),
        compiler_params=_params(2),
        name="dilated",
    )(qkv_nat, qkv_nat, qkv_nat, qkv_perm, qkv_perm, qkv_perm)


NBR_QROWS = 2
NBR_KROWS = NA_ROWS + 2
NBR_BQ = NBR_QROWS * GRID_W
NBR_BK = NBR_KROWS * GRID_W
NBR_VARIANTS = ((0, (0, 0)), (2, (0, 0)), (4, (0, 1)), (6, (2, 2)), (8, (2, 2)))


NBR_STEP_TOKENS = 8192


def _nbr_kernel(q_ref, k_ref, v_ref, bias_ref, o_ref, s_scr, p_scr, m_scr, *, T):
    rows = T // GRID_W
    tiles_per_seq = rows // NBR_QROWS
    lo = lax.broadcasted_iota(jnp.int32, (NBR_BQ, 128), 1) < HEAD_DIM

    def where(g, t):
        tile = jnp.asarray(g * GROUP + t, jnp.int32)
        seq = lax.div(tile, tiles_per_seq)
        r = NBR_QROWS * lax.rem(tile, tiles_per_seq)
        first_key_row = jnp.clip(r - NA_ROWS // 2, 0, rows - NBR_KROWS)
        variant = lax.div(r - first_key_row, 2)
        return (seq, pl.multiple_of(r * GRID_W, NBR_BQ),
                pl.multiple_of(first_key_row * GRID_W, GRID_W), variant)

    def load_qk(g, t):
        seq, q0, k0, variant = where(g, t)
        return (q_ref[seq, pl.ds(q0, NBR_BQ), :], k_ref[seq, pl.ds(k0, NBR_BK), :],
                bias_ref[0, variant], bias_ref[1, variant])

    def load_v(g, t):
        seq, _, k0, _ = where(g, t)
        return v_ref[seq, pl.ds(k0, NBR_BK), :]

    def finalize(g, tiles):
        for t, (_, l, a) in enumerate(tiles):
            seq, q0, _, _ = where(g, t)
            o_ref[seq, pl.ds(q0, NBR_BQ), :] = (a / l).astype(BF16)

    n_seq = q_ref.shape[0]
    _attn_pipeline([(n_seq * tiles_per_seq // GROUP, load_qk, load_v, finalize)],
                   lo, lo, s_scr, p_scr, m_scr, unroll=False)


def _nbr(qkv_nat, bias):
    B, T, _ = qkv_nat.shape
    nb = max(1, NBR_STEP_TOKENS // T)
    assert T % (NBR_BQ * GROUP) == 0 and T // GRID_W >= 2 * NBR_KROWS and B % nb == 0
    seq = lambda col0: pl.BlockSpec((nb, T, 128), lambda b, j: (b, 0, col0 + j))
    return pl.pallas_call(
        functools.partial(_nbr_kernel, T=T),
        grid=(B // nb, N_PAIRS),
        in_specs=[seq(3 * N_PAIRS), seq(4 * N_PAIRS), seq(5 * N_PAIRS),
                  pl.BlockSpec((2, len(NBR_VARIANTS), NBR_BQ, NBR_BK), lambda b, j: (j, 0, 0, 0))],
        out_specs=pl.BlockSpec((nb, T, 128), lambda b, j: (b, 0, j)),
        out_shape=jax.ShapeDtypeStruct((B, T, WIDTH_B), BF16),
        scratch_shapes=_pipeline_scratch(NBR_BQ, NBR_BK),
        compiler_params=_params(2),
        name="nbr",
    )(qkv_nat, qkv_nat, qkv_nat, bias)


def _nbr_bias(rpb):
    qc = jnp.arange(GRID_W)[:, None]
    kc = jnp.arange(GRID_W)[None, :]
    cidx = jnp.clip(kc - qc, -(NA_COLS - 1), NA_COLS - 1) + NA_COLS - 1
    cs = jnp.clip(qc - NA_COLS // 2, 0, GRID_W - NA_COLS)
    col_ok = (kc >= cs) & (kc < cs + NA_COLS)
    scaled = (LOG2E * rpb.astype(F32))[:, None, :, None, :]
    base = sum(jnp.where((cidx == d)[None, :, None, :], scaled[..., d], 0.0)
               for d in range(2 * NA_COLS - 1))
    base = jnp.where(col_ok[None, :, None, :], base, NEG_INF)
    n_heads = rpb.shape[0]
    blocks = []
    for off, first in NBR_VARIANTS:
        for q in range(NBR_QROWS):
            start = first[q] - off - q + NA_ROWS - 1
            pads = [jnp.full((n_heads, GRID_W, n, GRID_W), NEG_INF, F32)
                    for n in (first[q], NBR_KROWS - NA_ROWS - first[q])]
            rows = jnp.concatenate([pads[0], base[:, :, start:start + NA_ROWS], pads[1]], axis=2)
            blocks.append(rows.reshape(n_heads, GRID_W, NBR_BK))
    return jnp.stack(blocks, axis=1).reshape(n_heads, len(NBR_VARIANTS), NBR_BQ, NBR_BK)


UP_CHUNKS = 4


def _tail_kernel(oa_ref, ob_ref, x_ref, kv_ref, ga_ref, gb_ref, g0_ref, b0_ref, wo_ref, g1_ref, b1_ref,
                 wq_ref, wxo_ref, g2_ref, b2_ref, wu_ref, wd_ref, g3_ref, b3_ref, o_ref, x2_scr, r_scr):
    s = pl.program_id(0)
    cur, prev = lax.rem(s, 2), lax.rem(s + 1, 2)

    @pl.when(s == 0)
    def _():
        x2_scr[1] = jnp.zeros(x2_scr.shape[1:], F32)
        r_scr[...] = jnp.zeros(r_scr.shape, F32)

    x2_prev = x2_scr[prev]
    xb_prev = x2_prev.astype(BF16)
    cw = D_FF // UP_CHUNKS
    hidden = []

    def up_chunk():
        c = len(hidden)
        h = jnp.maximum(jnp.dot(xb_prev, wu_ref[:, c * cw:(c + 1) * cw],
                                preferred_element_type=F32), 0.0)
        hidden.append((h * h).astype(BF16))

    up_chunk()
    y = jnp.concatenate([_rms(oa_ref[0].astype(F32), ga_ref[...]),
                         _rms(ob_ref[0].astype(F32), gb_ref[...])], axis=1).astype(BF16)
    z = jnp.dot(y, wo_ref[...], preferred_element_type=F32)
    x1 = _ln(ALPHA * _ln(x_ref[0], g0_ref[...], b0_ref[...]) + z, g1_ref[...], b1_ref[...])
    up_chunk()
    q = jnp.dot(x1.astype(BF16), wq_ref[...], preferred_element_type=F32)
    q = (q * (HEAD_DIM_X ** -0.5)).astype(BF16)
    outs = []
    for h in range(N_HEADS_X):
        k = kv_ref[0, :, h * HEAD_DIM_X:(h + 1) * HEAD_DIM_X]
        v = kv_ref[0, :, D_MODEL + h * HEAD_DIM_X:D_MODEL + (h + 1) * HEAD_DIM_X]
        sc = lax.dot_general(q[:, h * HEAD_DIM_X:(h + 1) * HEAD_DIM_X], k, NT_DIMS,
                             preferred_element_type=F32)
        pr = jnp.exp(sc - jnp.max(sc, axis=1, keepdims=True))
        l = jnp.sum(pr, axis=1, keepdims=True)
        outs.append(jnp.dot(pr.astype(BF16), v, preferred_element_type=F32) / l)
    up_chunk()
    z = jnp.dot(jnp.concatenate(outs, axis=1).astype(BF16), wxo_ref[...], preferred_element_type=F32)
    x2_scr[cur] = _ln(ALPHA * x1 + z, g2_ref[...], b2_ref[...])
    while len(hidden) < UP_CHUNKS:
        up_chunk()
    o_ref[0] = _ln(r_scr[prev], g3_ref[...], b3_ref[...])
    z = jnp.dot(jnp.concatenate(hidden, axis=1), wd_ref[...], preferred_element_type=F32)
    r_scr[cur] = ALPHA * x2_prev + z


def _tail(oa, ob, x, kv, p, tm):
    B, T, _ = x.shape
    M = kv.shape[1]
    assert T % tm == 0
    n = T // tm
    last = B * n - 1
    vec = lambda width: _const_spec((1, width))
    mat = lambda rows, cols: _const_spec((rows, cols))

    def cur(width):
        return pl.BlockSpec((1, tm, width),
                            lambda s: (jnp.minimum(s, last) // n, jnp.minimum(s, last) % n, 0))

    return pl.pallas_call(
        _tail_kernel,
        grid=(B * n + 2,),
        in_specs=[cur(WIDTH_A), cur(WIDTH_B), cur(D_MODEL),
                  pl.BlockSpec((1, M, 2 * D_MODEL), lambda s: (jnp.minimum(s, last) // n, 0, 0)),
                  vec(WIDTH_A), vec(WIDTH_B), vec(D_MODEL), vec(D_MODEL),
                  mat(D_MODEL, D_MODEL), vec(D_MODEL), vec(D_MODEL),
                  mat(D_MODEL, D_MODEL), mat(D_MODEL, D_MODEL), vec(D_MODEL), vec(D_MODEL),
                  mat(D_MODEL, D_FF), mat(D_FF, D_MODEL), vec(D_MODEL), vec(D_MODEL)],
        out_specs=pl.BlockSpec((1, tm, D_MODEL),
                               lambda s: (jnp.maximum(s - 2, 0) // n, jnp.maximum(s - 2, 0) % n, 0)),
        out_shape=jax.ShapeDtypeStruct((B, T, D_MODEL), F32),
        scratch_shapes=[pltpu.VMEM((2, tm, D_MODEL), F32), pltpu.VMEM((2, tm, D_MODEL), F32)],
        compiler_params=_params(1),
        name="tail",
    )(oa, ob, x, kv, p["g_mix_a"], p["g_mix_b"], p["ln_in_g"], p["ln_in_b"], p["w_out"],
      p["ln1_g"], p["ln1_b"], p["w_xq"], p["w_xo"], p["ln2_g"], p["ln2_b"],
      p["w_up"], p["w_down"], p["ln3_g"], p["ln3_b"])


def _rope_tables(T):
    half = HEAD_DIM // 2
    inv = ROPE_THETA ** (-jnp.arange(half, dtype=F32) / half)
    ang = jnp.arange(T, dtype=F32)[:, None] * inv[None, :]
    cos, sin = jnp.cos(ang), jnp.sin(ang)
    return (jnp.concatenate([cos, cos, cos, cos], axis=1),
            jnp.concatenate([-sin, -sin, sin, sin], axis=1))


def _pair_rotary_layout(w_in):
    half = HEAD_DIM // 2
    idx = jnp.arange(128).reshape(4, half)[jnp.array([0, 2, 1, 3])].reshape(128)
    cols = (jnp.arange(2 * WIDTH_A // 128)[:, None] * 128 + idx[None, :]).reshape(-1)
    return jnp.concatenate([w_in[:, cols], w_in[:, 2 * WIDTH_A:]], axis=1)


QKV_TILE = 1024
TAIL_TILE = 512


def _trunk(x, mem, p):
    B, T, _ = x.shape
    kv = _mem_kv(mem, p["w_xkv"])
    qkv_nat, qkv_perm = _qkv(x, p["ln_in_g"], p["ln_in_b"], p["w_in"], p["cos"], p["sin"], QKV_TILE)
    oa = _dilated(qkv_nat, qkv_perm.reshape(B, T, 3 * WIDTH_A))
    ob = _nbr(qkv_nat, p["bias"])
    return _tail(oa, ob, x, kv, p, TAIL_TILE)


def kernel(x_prompt, x_sample, mem_prompt, mem_sample, ln_in_g, ln_in_b, w_in, rpb, g_mix_a, g_mix_b,
           w_out, ln1_g, ln1_b, w_xq, w_xkv, w_xo, ln2_g, ln2_b, w_up, w_down, ln3_g, ln3_b):
    assert w_in.shape[0] == 1, "single-layer trunk"
    row = lambda v: v.reshape(1, -1).astype(F32)
    cos, sin = _rope_tables(max(x_prompt.shape[1], x_sample.shape[1]))
    p = dict(
        cos=cos, sin=sin, ln_in_g=row(ln_in_g), ln_in_b=row(ln_in_b),
        w_in=_pair_rotary_layout(w_in[0].astype(BF16)), bias=_nbr_bias(rpb[0]),
        g_mix_a=row(g_mix_a[0]), g_mix_b=row(g_mix_b[0]), w_out=w_out[0].astype(BF16),
        ln1_g=row(ln1_g[0]), ln1_b=row(ln1_b[0]),
        w_xq=w_xq[0].astype(BF16), w_xkv=w_xkv[0].astype(BF16), w_xo=w_xo[0].astype(BF16),
        ln2_g=row(ln2_g[0]), ln2_b=row(ln2_b[0]),
        w_up=w_up[0].astype(BF16), w_down=w_down[0].astype(BF16),
        ln3_g=row(ln3_g[0]), ln3_b=row(ln3_b[0]),
    )
    return _trunk(x_prompt, mem_prompt, p), _trunk(x_sample, mem_sample, p)
```

```python
import functools

import jax
import jax.numpy as jnp
from jax import lax
from jax.experimental import pallas as pl
from jax.experimental.pallas import tpu as pltpu

F32 = jnp.float32
BF16 = jnp.bfloat16

D_MODEL = 1024
HEAD_DIM = 64
WIDTH_A = 512
WIDTH_B = 512
N_PAIRS = WIDTH_A // 128
GRID_W = 64
NA_ROWS = 8
NA_COLS = 16
N_HEADS_X = 4
HEAD_DIM_X = 256
D_FF = 4096
ROPE_THETA = 10000.0
LN_EPS = 1e-5
ALPHA = 2.0 ** 0.25
NEG_INF = -1e30
LOG2E = 1.4426950408889634
N_SIDE = 64
MAX_DIL = 16

VMEM_LIMIT = 56 * 1024 * 1024

NT_DIMS = (((1,), (1,)), ((), ()))


def _ln(x, g, b):
    mu = jnp.mean(x, axis=-1, keepdims=True)
    xc = x - mu
    var = jnp.mean(xc * xc, axis=-1, keepdims=True)
    return xc * lax.rsqrt(var + LN_EPS) * g + b


def _rms(x, g):
    return x * lax.rsqrt(jnp.mean(x * x, axis=-1, keepdims=True) + LN_EPS) * g


def _params(n_axes):
    return pltpu.CompilerParams(dimension_semantics=("arbitrary",) * n_axes,
                                vmem_limit_bytes=VMEM_LIMIT)


def _const_spec(shape):
    nd = len(shape)
    return pl.BlockSpec(shape, lambda *_: (0,) * nd, pipeline_mode=pl.Buffered(1))


def _mem_kv_kernel(mem_ref, w_ref, o_ref):
    o_ref[0] = jnp.dot(mem_ref[0].astype(BF16), w_ref[...],
                       preferred_element_type=F32).astype(BF16)


def _mem_kv(mem, w_xkv):
    B, M, _ = mem.shape
    return pl.pallas_call(
        _mem_kv_kernel,
        grid=(B,),
        in_specs=[pl.BlockSpec((1, M, D_MODEL), lambda b: (b, 0, 0)),
                  _const_spec((D_MODEL, 2 * D_MODEL))],
        out_specs=pl.BlockSpec((1, M, 2 * D_MODEL), lambda b: (b, 0, 0)),
        out_shape=jax.ShapeDtypeStruct((B, M, 2 * D_MODEL), BF16),
        compiler_params=_params(1),
        name="mem_kv",
    )(mem, w_xkv)


PERM_SUB = 256


def _qkv_kernel(x_ref, g_ref, b_ref, w_ref, cos_ref, sin_ref, nat_ref, perm_ref, *, tm):
    xb = _ln(x_ref[0], g_ref[...], b_ref[...]).astype(BF16)
    cos = jnp.concatenate([cos_ref[...]] * 4, axis=1)
    sin = jnp.concatenate([sin_ref[...]] * 4, axis=1)
    row = lax.broadcasted_iota(jnp.int32, (PERM_SUB, PERM_SUB), 0)
    col = lax.broadcasted_iota(jnp.int32, (PERM_SUB, PERM_SUB), 1)
    rows_per = PERM_SUB // MAX_DIL
    perm_mat = jnp.where(col == MAX_DIL * (row % rows_per) + row // rows_per, 1.0, 0.0).astype(BF16)
    for part in (0, 3, 1, 4, 2, 5):
        cols = slice(part * 512, (part + 1) * 512)
        r = jnp.dot(xb, w_ref[:, cols], preferred_element_type=F32)
        if part in (0, 1):
            swapped = jnp.concatenate([pltpu.roll(r[:, c:c + 128], 64, 1)
                                       for c in range(0, WIDTH_A, 128)], axis=1)
            r = r * cos + swapped * sin
        if part in (0, 3):
            r = r * (HEAD_DIM ** -0.5 * LOG2E)
        rb = r.astype(BF16)
        if part > 0:
            nat_ref[0, :, cols] = rb
        if part < 3:
            for g in range(tm // PERM_SUB):
                moved = jnp.dot(perm_mat, rb[g * PERM_SUB:(g + 1) * PERM_SUB],
                                preferred_element_type=F32)
                for c in range(MAX_DIL):
                    perm_ref[0, c, g * rows_per:(g + 1) * rows_per, cols] = (
                        moved[c * rows_per:(c + 1) * rows_per].astype(BF16))
                if part == 0:
                    half = rows_per // 2
                    for h in range(2):
                        tile = jnp.concatenate(
                            [moved[c * rows_per + h * half:c * rows_per + (h + 1) * half]
                             for c in range(MAX_DIL)], axis=0)
                        row0 = g * PERM_SUB + h * (PERM_SUB // 2)
                        nat_ref[0, row0:row0 + PERM_SUB // 2, cols] = tile.astype(BF16)


def _qkv(x, ln_g, ln_b, w_in, cos, sin, tm):
    B, T, _ = x.shape
    L16 = T // MAX_DIL
    assert tm % PERM_SUB == 0 and T % tm == 0
    return pl.pallas_call(
        functools.partial(_qkv_kernel, tm=tm),
        grid=(B, T // tm),
        in_specs=[pl.BlockSpec((1, tm, D_MODEL), lambda b, i: (b, i, 0)),
                  _const_spec((1, D_MODEL)), _const_spec((1, D_MODEL)),
                  _const_spec((D_MODEL, 3 * D_MODEL)),
                  pl.BlockSpec((tm, 128), lambda b, i: (i, 0)),
                  pl.BlockSpec((tm, 128), lambda b, i: (i, 0))],
        out_specs=[pl.BlockSpec((1, tm, 3 * D_MODEL), lambda b, i: (b, i, 0)),
                   pl.BlockSpec((1, MAX_DIL, tm // MAX_DIL, 3 * WIDTH_A), lambda b, i: (b, 0, i, 0))],
        out_shape=[jax.ShapeDtypeStruct((B, T, 3 * D_MODEL), BF16),
                   jax.ShapeDtypeStruct((B, MAX_DIL, L16, 3 * WIDTH_A), BF16)],
        compiler_params=_params(2),
        name="qkv",
    )(x, ln_g, ln_b, w_in, cos, sin)


GROUP = 8
STEP_TOKENS = 8192


def _attn_pipeline(segments, lo_q, lo, s_scr, p_scr, m_scr, unroll):
    bq, bk = s_scr.shape[-2:]

    def stage_a(seg, g, slot):
        for t in range(GROUP):
            q2, k2, b_lo, b_hi = seg[1](g, t)
            zero = jnp.zeros_like(q2)
            for h, (q1, bias) in enumerate(((jnp.where(lo_q, q2, zero), b_lo),
                                            (jnp.where(lo_q, zero, q2), b_hi))):
                s = lax.dot_general(q1, k2, NT_DIMS, preferred_element_type=F32) + bias
                s_scr[slot, t, h] = s
                m_scr[slot, t, h] = jnp.broadcast_to(jnp.max(s, axis=1, keepdims=True), (bq, 128))

    def stage_b(slot):
        for t in range(GROUP):
            for h in range(2):
                m = jnp.concatenate([m_scr[slot, t, h]] * (bk // 128), axis=1)
                p_scr[slot, t, h] = jnp.exp2((s_scr[slot, t, h] - m).astype(BF16))

    def stage_c(seg, g, slot):
        tiles = []
        for t in range(GROUP):
            v2 = seg[2](g, t)
            v_ones = jnp.concatenate([v2, jnp.ones_like(v2)], axis=1)
            pv0 = jnp.dot(p_scr[slot, t, 0], v_ones, preferred_element_type=F32)
            pv1 = jnp.dot(p_scr[slot, t, 1], v_ones, preferred_element_type=F32)
            tiles.append((jnp.where(lo, m_scr[slot, t, 0], m_scr[slot, t, 1]),
                          jnp.where(lo, pv0[:, 128:], pv1[:, 128:]),
                          jnp.where(lo, pv0[:, :128], pv1[:, :128])))
        seg[3](g, tiles)

    stage_a(segments[0], 0, 0)
    stage_b(0)
    stage_a(segments[0], 1, 1)
    for k, seg in enumerate(segments):
        n = seg[0]
        assert n >= 2 and n % 2 == 0

        def step(i, slot):
            stage_c(seg, i, slot)
            stage_b(1 - slot)
            stage_a(seg, i + 2, slot)

        def body(j, carry):
            step(2 * j, 0)
            step(2 * j + 1, 1)
            return carry

        if unroll:
            for j in range((n - 2) // 2):
                body(j, 0)
        else:
            lax.fori_loop(0, (n - 2) // 2, body, 0)
        nxt = segments[k + 1] if k + 1 < len(segments) else None
        stage_c(seg, n - 2, 0)
        stage_b(1)
        if nxt:
            stage_a(nxt, 0, 0)
        stage_c(seg, n - 1, 1)
        if nxt:
            stage_b(0)
            stage_a(nxt, 1, 1)


def _pipeline_scratch(bq, bk):
    return [pltpu.VMEM((2, GROUP, 2, bq, bk), F32), pltpu.VMEM((2, GROUP, 2, bq, bk), BF16),
            pltpu.VMEM((2, GROUP, 2, bq, 128), F32)]


def _merge(m_r, l_r, a_r, m_t, l_t, a_t):
    m_n = jnp.maximum(m_r, m_t)
    e_r = jnp.exp2(m_r - m_n)
    e_t = jnp.exp2(m_t - m_n)
    return m_n, l_r * e_r + l_t * e_t, a_r * e_r + a_t * e_t


def _dilated_kernel(qn_ref, kn_ref, vn_ref, qp_ref, kp_ref, vp_ref, o_ref,
                    m_ref, l_ref, a_ref, tmp_ref, mask_ref, s_scr, p_scr, m_scr, *, T):
    BQ, BK = 128, 256
    L16 = T // MAX_DIL
    row = lax.broadcasted_iota(jnp.int32, (BQ, BK), 0)
    col = lax.broadcasted_iota(jnp.int32, (BQ, BK), 1)
    band = row - col
    band4 = 4 * ((row % 32) - (col % 64)) + (row // 32 - col // 64)
    rows_per = BQ // MAX_DIL
    band1 = MAX_DIL * (row % rows_per) + row // rows_per - col
    for n in range(3):
        mask_ref[n] = jnp.where(jnp.abs(band + N_SIDE * n) <= N_SIDE, 0.0, NEG_INF)
        mask_ref[3 + n] = jnp.where(jnp.abs(band4 + N_SIDE * n) <= N_SIDE, 0.0, NEG_INF)
        mask_ref[6 + n] = jnp.where(jnp.abs(band1 + N_SIDE * n) <= N_SIDE, 0.0, NEG_INF)
    lane = lax.broadcasted_iota(jnp.int32, (BQ, 128), 1)
    lo = lane < HEAD_DIM
    lo_q = lane % HEAD_DIM < HEAD_DIM // 2
    n_groups = T // (BQ * GROUP)

    nib = L16 // BQ

    def rows16(g, t):
        tile = jnp.asarray(g * GROUP + t, jnp.int32)
        c = lax.div(tile, nib)
        ib = lax.rem(tile, nib)
        kst = jnp.clip(ib * BQ - N_SIDE, 0, L16 - BK)
        return (pl.multiple_of(c * L16 + ib * BQ, BQ), pl.multiple_of(c * L16 + kst, N_SIDE),
                lax.div(ib * BQ - kst, N_SIDE))

    def load_qk16(seq, g, t):
        q0, k0, case = rows16(g, t)
        mask = mask_ref[case]
        return qp_ref[seq, pl.ds(q0, BQ), :], kp_ref[seq, pl.ds(k0, BK), :], mask, mask

    def load_v16(seq, g, t):
        return vp_ref[seq, pl.ds(rows16(g, t)[1], BK), :]

    def finalize16(seq, g, tiles):
        for t, (m, l, a) in enumerate(tiles):
            q0 = rows16(g, t)[0]
            m_ref[pl.ds(q0, BQ), :] = m
            l_ref[pl.ds(q0, BQ), :] = l
            a_ref[pl.ds(q0, BQ), :] = a

    nb4 = L16 // 32

    def rows4(g, t):
        tile = jnp.asarray(g * GROUP + t, jnp.int32)
        r4 = lax.div(tile, nb4)
        i0 = lax.rem(tile, nb4) * 32
        kst = jnp.clip(i0 - 16, 0, L16 - 64)
        return ([pl.multiple_of((r4 + 4 * w) * L16 + i0, 32) for w in range(4)],
                [pl.multiple_of((r4 + 4 * w) * L16 + kst, 16) for w in range(4)],
                3 + lax.div(i0 - kst, 16))

    def load_qk4(seq, g, t):
        qrows, krows, case = rows4(g, t)
        mask = mask_ref[case]
        return (jnp.concatenate([qp_ref[seq, pl.ds(r, 32), :] for r in qrows], axis=0),
                jnp.concatenate([kp_ref[seq, pl.ds(r, 64), :] for r in krows], axis=0), mask, mask)

    def load_v4(seq, g, t):
        return jnp.concatenate([vp_ref[seq, pl.ds(r, 64), :] for r in rows4(g, t)[1]], axis=0)

    def finalize4(seq, g, tiles):
        merged = []
        for t, tile in enumerate(tiles):
            qrows = rows4(g, t)[0]
            state = [jnp.concatenate([ref[pl.ds(r, 32), :] for r in qrows], axis=0)
                     for ref in (m_ref, l_ref, a_ref)]
            merged.append((qrows, _merge(*state, *tile)))
        for qrows, new in merged:
            for ref, val in zip((m_ref, l_ref, a_ref), new):
                for w, r in enumerate(qrows):
                    ref[pl.ds(r, 32), :] = val[w * 32:(w + 1) * 32]

    def rows1(g, t):
        t0 = pl.multiple_of(jnp.asarray(g * GROUP + t, jnp.int32) * BQ, BQ)
        kst = pl.multiple_of(jnp.clip(t0 - N_SIDE, 0, T - BK), N_SIDE)
        return t0, kst, 6 + lax.div(t0 - kst, N_SIDE)

    def load_qk1(seq, g, t):
        t0, kst, case = rows1(g, t)
        mask = mask_ref[case]
        return qn_ref[seq, pl.ds(t0, BQ), :], kn_ref[seq, pl.ds(kst, BK), :], mask, mask

    def load_v1(seq, g, t):
        return vn_ref[seq, pl.ds(rows1(g, t)[1], BK), :]

    def finalize1(seq, g, tiles):
        for t, tile in enumerate(tiles):
            t0 = rows1(g, t)[0]
            i0 = lax.div(t0, MAX_DIL)
            state = [jnp.concatenate([ref[pl.ds(pl.multiple_of(c * L16 + i0, rows_per), rows_per), :]
                                      for c in range(MAX_DIL)], axis=0)
                     for ref in (m_ref, l_ref, a_ref)]
            _, l_n, a_n = _merge(*state, *tile)
            out = a_n / l_n
            for c in range(MAX_DIL):
                tmp_ref[t, pl.ds(c, rows_per, stride=MAX_DIL), :] = out[c * rows_per:(c + 1) * rows_per]
            o_ref[seq, pl.ds(t0, BQ), :] = tmp_ref[t].astype(BF16)

    segments = [(n_groups,) + tuple(functools.partial(f, seq) for f in fns)
                for seq in range(o_ref.shape[0])
                for fns in ((load_qk16, load_v16, finalize16), (load_qk4, load_v4, finalize4),
                            (load_qk1, load_v1, finalize1))]
    _attn_pipeline(segments, lo_q, lo, s_scr, p_scr, m_scr, unroll=True)


def _dilated(qkv_nat, qkv_perm):
    B, T, _ = qkv_nat.shape
    nb = max(1, STEP_TOKENS // T)
    assert T % (MAX_DIL * 128) == 0 and T % (128 * GROUP) == 0 and T // MAX_DIL >= 256 and B % nb == 0
    seq = lambda col0: pl.BlockSpec((nb, T, 128), lambda b, j: (b, 0, col0 + j))
    return pl.pallas_call(
        functools.partial(_dilated_kernel, T=T),
        grid=(B // nb, N_PAIRS),
        in_specs=[seq(0), seq(N_PAIRS), seq(2 * N_PAIRS), seq(0), seq(N_PAIRS), seq(2 * N_PAIRS)],
        out_specs=pl.BlockSpec((nb, T, 128), lambda b, j: (b, 0, j)),
        out_shape=jax.ShapeDtypeStruct((B, T, WIDTH_A), BF16),
        scratch_shapes=[pltpu.VMEM((T, 128), F32), pltpu.VMEM((T, 128), F32),
                        pltpu.VMEM((T, 128), F32), pltpu.VMEM((GROUP, 128, 128), F32),
                        pltpu.VMEM((9, 128, 256), F32)] + _pipeline_scratch(128, 256),
        compiler_params=_params(2),
        name="dilated",
    )(qkv_nat, qkv_nat, qkv_nat, qkv_perm, qkv_perm, qkv_perm)


NBR_QROWS = 2
NBR_KROWS = NA_ROWS + 2
NBR_BQ = NBR_QROWS * GRID_W
NBR_BK = NBR_KROWS * GRID_W
NBR_VARIANTS = ((0, (0, 0)), (2, (0, 0)), (4, (0, 1)), (6, (2, 2)), (8, (2, 2)))


def _nbr_kernel(q_ref, k_ref, v_ref, bias_ref, o_ref, s_scr, p_scr, m_scr, *, T):
    rows = T // GRID_W
    tiles_per_seq = rows // NBR_QROWS
    lo = lax.broadcasted_iota(jnp.int32, (NBR_BQ, 128), 1) < HEAD_DIM

    def where(g, t):
        tile = jnp.asarray(g * GROUP + t, jnp.int32)
        seq = lax.div(tile, tiles_per_seq)
        r = NBR_QROWS * lax.rem(tile, tiles_per_seq)
        first_key_row = jnp.clip(r - NA_ROWS // 2, 0, rows - NBR_KROWS)
        variant = lax.div(r - first_key_row, 2)
        return (seq, pl.multiple_of(r * GRID_W, NBR_BQ),
                pl.multiple_of(first_key_row * GRID_W, GRID_W), variant)

    def load_qk(g, t):
        seq, q0, k0, variant = where(g, t)
        return (q_ref[seq, pl.ds(q0, NBR_BQ), :], k_ref[seq, pl.ds(k0, NBR_BK), :],
                bias_ref[0, variant], bias_ref[1, variant])

    def load_v(g, t):
        seq, _, k0, _ = where(g, t)
        return v_ref[seq, pl.ds(k0, NBR_BK), :]

    def finalize(g, tiles):
        for t, (_, l, a) in enumerate(tiles):
            seq, q0, _, _ = where(g, t)
            o_ref[seq, pl.ds(q0, NBR_BQ), :] = (a / l).astype(BF16)

    n_seq = q_ref.shape[0]
    _attn_pipeline([(n_seq * tiles_per_seq // GROUP, load_qk, load_v, finalize)],
                   lo, lo, s_scr, p_scr, m_scr, unroll=False)


def _nbr(qkv_nat, bias):
    B, T, _ = qkv_nat.shape
    nb = max(1, STEP_TOKENS // T)
    assert T % (NBR_BQ * GROUP) == 0 and T // GRID_W >= 2 * NBR_KROWS and B % nb == 0
    seq = lambda col0: pl.BlockSpec((nb, T, 128), lambda b, j: (b, 0, col0 + j))
    return pl.pallas_call(
        functools.partial(_nbr_kernel, T=T),
        grid=(B // nb, N_PAIRS),
        in_specs=[seq(3 * N_PAIRS), seq(4 * N_PAIRS), seq(5 * N_PAIRS),
                  pl.BlockSpec((2, len(NBR_VARIANTS), NBR_BQ, NBR_BK), lambda b, j: (j, 0, 0, 0))],
        out_specs=pl.BlockSpec((nb, T, 128), lambda b, j: (b, 0, j)),
        out_shape=jax.ShapeDtypeStruct((B, T, WIDTH_B), BF16),
        scratch_shapes=_pipeline_scratch(NBR_BQ, NBR_BK),
        compiler_params=_params(2),
        name="nbr",
    )(qkv_nat, qkv_nat, qkv_nat, bias)


def _nbr_bias(rpb):
    qc = jnp.arange(GRID_W)[:, None]
    kc = jnp.arange(GRID_W)[None, :]
    cidx = jnp.clip(kc - qc, -(NA_COLS - 1), NA_COLS - 1) + NA_COLS - 1
    cs = jnp.clip(qc - NA_COLS // 2, 0, GRID_W - NA_COLS)
    col_ok = (kc >= cs) & (kc < cs + NA_COLS)
    scaled = (LOG2E * rpb.astype(F32))[:, None, :, None, :]
    base = sum(jnp.where((cidx == d)[None, :, None, :], scaled[..., d], 0.0)
               for d in range(2 * NA_COLS - 1))
    base = jnp.where(col_ok[None, :, None, :], base, NEG_INF)
    n_heads = rpb.shape[0]
    blocks = []
    for off, first in NBR_VARIANTS:
        for q in range(NBR_QROWS):
            start = first[q] - off - q + NA_ROWS - 1
            pads = [jnp.full((n_heads, GRID_W, n, GRID_W), NEG_INF, F32)
                    for n in (first[q], NBR_KROWS - NA_ROWS - first[q])]
            rows = jnp.concatenate([pads[0], base[:, :, start:start + NA_ROWS], pads[1]], axis=2)
            blocks.append(rows.reshape(n_heads, GRID_W, NBR_BK))
    return jnp.stack(blocks, axis=1).reshape(n_heads, len(NBR_VARIANTS), NBR_BQ, NBR_BK)


UP_CHUNKS = 4


def _tail_kernel(oa_ref, ob_ref, x_ref, kv_ref, ga_ref, gb_ref, g0_ref, b0_ref, wo_ref, g1_ref, b1_ref,
                 wq_ref, wxo_ref, g2_ref, b2_ref, wu_ref, wd_ref, g3_ref, b3_ref, o_ref, x2_scr, r_scr):
    s = pl.program_id(0)
    cur, prev = lax.rem(s, 2), lax.rem(s + 1, 2)

    @pl.when(s == 0)
    def _():
        x2_scr[1] = jnp.zeros(x2_scr.shape[1:], F32)
        r_scr[...] = jnp.zeros(r_scr.shape, F32)

    x2_prev = x2_scr[prev]
    xb_prev = x2_prev.astype(BF16)
    cw = D_FF // UP_CHUNKS
    hidden = []

    def up_chunk():
        c = len(hidden)
        h = jnp.maximum(jnp.dot(xb_prev, wu_ref[:, c * cw:(c + 1) * cw],
                                preferred_element_type=F32), 0.0)
        hidden.append((h * h).astype(BF16))

    up_chunk()
    y = jnp.concatenate([_rms(oa_ref[0].astype(F32), ga_ref[...]),
                         _rms(ob_ref[0].astype(F32), gb_ref[...])], axis=1).astype(BF16)
    z = jnp.dot(y, wo_ref[...], preferred_element_type=F32)
    x1 = _ln(ALPHA * _ln(x_ref[0], g0_ref[...], b0_ref[...]) + z, g1_ref[...], b1_ref[...])
    up_chunk()
    q = jnp.dot(x1.astype(BF16), wq_ref[...], preferred_element_type=F32)
    q = (q * (HEAD_DIM_X ** -0.5)).astype(BF16)
    outs = []
    for h in range(N_HEADS_X):
        k = kv_ref[0, :, h * HEAD_DIM_X:(h + 1) * HEAD_DIM_X]
        v = kv_ref[0, :, D_MODEL + h * HEAD_DIM_X:D_MODEL + (h + 1) * HEAD_DIM_X]
        sc = lax.dot_general(q[:, h * HEAD_DIM_X:(h + 1) * HEAD_DIM_X], k, NT_DIMS,
                             preferred_element_type=F32)
        pr = jnp.exp(sc - jnp.max(sc, axis=1, keepdims=True))
        l = jnp.sum(pr, axis=1, keepdims=True)
        outs.append(jnp.dot(pr.astype(BF16), v, preferred_element_type=F32) / l)
    up_chunk()
    z = jnp.dot(jnp.concatenate(outs, axis=1).astype(BF16), wxo_ref[...], preferred_element_type=F32)
    x2_scr[cur] = _ln(ALPHA * x1 + z, g2_ref[...], b2_ref[...])
    while len(hidden) < UP_CHUNKS:
        up_chunk()
    o_ref[0] = _ln(r_scr[prev], g3_ref[...], b3_ref[...])
    z = jnp.dot(jnp.concatenate(hidden, axis=1), wd_ref[...], preferred_element_type=F32)
    r_scr[cur] = ALPHA * x2_prev + z


def _tail(oa, ob, x, kv, p, tm):
    B, T, _ = x.shape
    M = kv.shape[1]
    assert T % tm == 0
    n = T // tm
    last = B * n - 1
    vec = lambda width: _const_spec((1, width))
    mat = lambda rows, cols: _const_spec((rows, cols))

    def cur(width):
        return pl.BlockSpec((1, tm, width),
                            lambda s: (jnp.minimum(s, last) // n, jnp.minimum(s, last) % n, 0))

    return pl.pallas_call(
        _tail_kernel,
        grid=(B * n + 2,),
        in_specs=[cur(WIDTH_A), cur(WIDTH_B), cur(D_MODEL),
                  pl.BlockSpec((1, M, 2 * D_MODEL), lambda s: (jnp.minimum(s, last) // n, 0, 0)),
                  vec(WIDTH_A), vec(WIDTH_B), vec(D_MODEL), vec(D_MODEL),
                  mat(D_MODEL, D_MODEL), vec(D_MODEL), vec(D_MODEL),
                  mat(D_MODEL, D_MODEL), mat(D_MODEL, D_MODEL), vec(D_MODEL), vec(D_MODEL),
                  mat(D_MODEL, D_FF), mat(D_FF, D_MODEL), vec(D_MODEL), vec(D_MODEL)],
        out_specs=pl.BlockSpec((1, tm, D_MODEL),
                               lambda s: (jnp.maximum(s - 2, 0) // n, jnp.maximum(s - 2, 0) % n, 0)),
        out_shape=jax.ShapeDtypeStruct((B, T, D_MODEL), F32),
        scratch_shapes=[pltpu.VMEM((2, tm, D_MODEL), F32), pltpu.VMEM((2, tm, D_MODEL), F32)],
        compiler_params=_params(1),
        name="tail",
    )(oa, ob, x, kv, p["g_mix_a"], p["g_mix_b"], p["ln_in_g"], p["ln_in_b"], p["w_out"],
      p["ln1_g"], p["ln1_b"], p["w_xq"], p["w_xo"], p["ln2_g"], p["ln2_b"],
      p["w_up"], p["w_down"], p["ln3_g"], p["ln3_b"])


def _rope_tables(T):
    half = HEAD_DIM // 2
    inv = ROPE_THETA ** (-jnp.arange(half, dtype=F32) / half)
    ang = jnp.arange(T, dtype=F32)[:, None] * inv[None, :]
    cos, sin = jnp.cos(ang), jnp.sin(ang)
    return (jnp.concatenate([cos, cos, cos, cos], axis=1),
            jnp.concatenate([-sin, -sin, sin, sin], axis=1))


def _pair_rotary_layout(w_in):
    half = HEAD_DIM // 2
    idx = jnp.arange(128).reshape(4, half)[jnp.array([0, 2, 1, 3])].reshape(128)
    cols = (jnp.arange(2 * WIDTH_A // 128)[:, None] * 128 + idx[None, :]).reshape(-1)
    return jnp.concatenate([w_in[:, cols], w_in[:, 2 * WIDTH_A:]], axis=1)


QKV_TILE = 1024
TAIL_TILE = 512


def _trunk(x, mem, p):
    B, T, _ = x.shape
    kv = _mem_kv(mem, p["w_xkv"])
    qkv_nat, qkv_perm = _qkv(x, p["ln_in_g"], p["ln_in_b"], p["w_in"], p["cos"], p["sin"], QKV_TILE)
    oa = _dilated(qkv_nat, qkv_perm.reshape(B, T, 3 * WIDTH_A))
    ob = _nbr(qkv_nat, p["bias"])
    return _tail(oa, ob, x, kv, p, TAIL_TILE)


def kernel(x_prompt, x_sample, mem_prompt, mem_sample, ln_in_g, ln_in_b, w_in, rpb, g_mix_a, g_mix_b,
           w_out, ln1_g, ln1_b, w_xq, w_xkv, w_xo, ln2_g, ln2_b, w_up, w_down, ln3_g, ln3_b):
    assert w_in.shape[0] == 1, "single-layer trunk"
    row = lambda v: v.reshape(1, -1).astype(F32)
    cos, sin = _rope_tables(max(x_prompt.shape[1], x_sample.shape[1]))
    p = dict(
        cos=cos, sin=sin, ln_in_g=row(ln_in_g), ln_in_b=row(ln_in_b),
        w_in=_pair_rotary_layout(w_in[0].astype(BF16)), bias=_nbr_bias(rpb[0]),
        g_mix_a=row(g_mix_a[0]), g_mix_b=row(g_mix_b[0]), w_out=w_out[0].astype(BF16),
        ln1_g=row(ln1_g[0]), ln1_b=row(ln1_b[0]),
        w_xq=w_xq[0].astype(BF16), w_xkv=w_xkv[0].astype(BF16), w_xo=w_xo[0].astype(BF16),
        ln2_g=row(ln2_g[0]), ln2_b=row(ln2_b[0]),
        w_up=w_up[0].astype(BF16), w_down=w_down[0].astype(BF16),
        ln3_g=row(ln3_g[0]), ln3_b=row(ln3_b[0]),
    )
    return _trunk(x_prompt, mem_prompt, p), _trunk(x_sample, mem_sample, p)
```

```python
import functools

import jax
import jax.numpy as jnp
from jax import lax
from jax.experimental import pallas as pl
from jax.experimental.pallas import tpu as pltpu

F32 = jnp.float32
BF16 = jnp.bfloat16

D_MODEL = 1024
HEAD_DIM = 64
WIDTH_A = 512
WIDTH_B = 512
N_PAIRS = WIDTH_A // 128
GRID_W = 64
NA_ROWS = 8
NA_COLS = 16
N_HEADS_X = 4
HEAD_DIM_X = 256
D_FF = 4096
ROPE_THETA = 10000.0
LN_EPS = 1e-5
ALPHA = 2.0 ** 0.25
NEG_INF = -1e30
LOG2E = 1.4426950408889634
N_SIDE = 64
MAX_DIL = 16

VMEM_LIMIT = 56 * 1024 * 1024

NT_DIMS = (((1,), (1,)), ((), ()))


def _ln(x, g, b):
    mu = jnp.mean(x, axis=-1, keepdims=True)
    xc = x - mu
    var = jnp.mean(xc * xc, axis=-1, keepdims=True)
    return xc * lax.rsqrt(var + LN_EPS) * g + b


def _rms(x, g):
    return x * lax.rsqrt(jnp.mean(x * x, axis=-1, keepdims=True) + LN_EPS) * g


def _params(n_axes):
    return pltpu.CompilerParams(dimension_semantics=("arbitrary",) * n_axes,
                                vmem_limit_bytes=VMEM_LIMIT)


def _const_spec(shape):
    nd = len(shape)
    return pl.BlockSpec(shape, lambda *_: (0,) * nd, pipeline_mode=pl.Buffered(1))


def _mem_kv_kernel(mem_ref, w_ref, o_ref):
    o_ref[0] = jnp.dot(mem_ref[0].astype(BF16), w_ref[...],
                       preferred_element_type=F32).astype(BF16)


def _mem_kv(mem, w_xkv):
    B, M, _ = mem.shape
    return pl.pallas_call(
        _mem_kv_kernel,
        grid=(B,),
        in_specs=[pl.BlockSpec((1, M, D_MODEL), lambda b: (b, 0, 0)),
                  _const_spec((D_MODEL, 2 * D_MODEL))],
        out_specs=pl.BlockSpec((1, M, 2 * D_MODEL), lambda b: (b, 0, 0)),
        out_shape=jax.ShapeDtypeStruct((B, M, 2 * D_MODEL), BF16),
        compiler_params=_params(1),
        name="mem_kv",
    )(mem, w_xkv)


PERM_SUB = 256


def _qkv_kernel(x_ref, g_ref, b_ref, w_ref, cos_ref, sin_ref, nat_ref, perm_ref, res_ref, *, tm):
    x0 = _ln(x_ref[0], g_ref[...], b_ref[...])
    res_ref[0] = ALPHA * x0
    xb = x0.astype(BF16)
    cos = jnp.concatenate([cos_ref[...]] * 4, axis=1)
    sin = jnp.concatenate([sin_ref[...]] * 4, axis=1)
    row = lax.broadcasted_iota(jnp.int32, (PERM_SUB, PERM_SUB), 0)
    col = lax.broadcasted_iota(jnp.int32, (PERM_SUB, PERM_SUB), 1)
    rows_per = PERM_SUB // MAX_DIL
    perm_mat = jnp.where(col == MAX_DIL * (row % rows_per) + row // rows_per, 1.0, 0.0).astype(BF16)
    for part in (0, 3, 1, 4, 2, 5):
        cols = slice(part * 512, (part + 1) * 512)
        r = jnp.dot(xb, w_ref[:, cols], preferred_element_type=F32)
        if part in (0, 1):
            swapped = jnp.concatenate([pltpu.roll(r[:, c:c + 128], 64, 1)
                                       for c in range(0, WIDTH_A, 128)], axis=1)
            r = r * cos + swapped * sin
        if part in (0, 3):
            r = r * (HEAD_DIM ** -0.5 * LOG2E)
        rb = r.astype(BF16)
        if part > 0:
            nat_ref[0, :, cols] = rb
        if part < 3:
            for g in range(tm // PERM_SUB):
                moved = jnp.dot(perm_mat, rb[g * PERM_SUB:(g + 1) * PERM_SUB],
                                preferred_element_type=F32)
                for c in range(MAX_DIL):
                    perm_ref[0, c, g * rows_per:(g + 1) * rows_per, cols] = (
                        moved[c * rows_per:(c + 1) * rows_per].astype(BF16))
                if part == 0:
                    half = rows_per // 2
                    for h in range(2):
                        tile = jnp.concatenate(
                            [moved[c * rows_per + h * half:c * rows_per + (h + 1) * half]
                             for c in range(MAX_DIL)], axis=0)
                        row0 = g * PERM_SUB + h * (PERM_SUB // 2)
                        nat_ref[0, row0:row0 + PERM_SUB // 2, cols] = tile.astype(BF16)


def _qkv(x, ln_g, ln_b, w_in, cos, sin, tm):
    B, T, _ = x.shape
    L16 = T // MAX_DIL
    assert tm % PERM_SUB == 0 and T % tm == 0
    return pl.pallas_call(
        functools.partial(_qkv_kernel, tm=tm),
        grid=(B, T // tm),
        in_specs=[pl.BlockSpec((1, tm, D_MODEL), lambda b, i: (b, i, 0)),
                  _const_spec((1, D_MODEL)), _const_spec((1, D_MODEL)),
                  _const_spec((D_MODEL, 3 * D_MODEL)),
                  pl.BlockSpec((tm, 128), lambda b, i: (i, 0)),
                  pl.BlockSpec((tm, 128), lambda b, i: (i, 0))],
        out_specs=[pl.BlockSpec((1, tm, 3 * D_MODEL), lambda b, i: (b, i, 0)),
                   pl.BlockSpec((1, MAX_DIL, tm // MAX_DIL, 3 * WIDTH_A), lambda b, i: (b, 0, i, 0)),
                   pl.BlockSpec((1, tm, D_MODEL), lambda b, i: (b, i, 0))],
        out_shape=[jax.ShapeDtypeStruct((B, T, 3 * D_MODEL), BF16),
                   jax.ShapeDtypeStruct((B, MAX_DIL, L16, 3 * WIDTH_A), BF16),
                   jax.ShapeDtypeStruct((B, T, D_MODEL), F32)],
        compiler_params=_params(2),
        name="qkv",
    )(x, ln_g, ln_b, w_in, cos, sin)


GROUP = 8
STEP_TOKENS = 8192


def _attn_pipeline(segments, lo_q, lo, s_scr, p_scr, m_scr, unroll):
    bq, bk = s_scr.shape[-2:]

    def stage_a(seg, g, slot):
        for t in range(GROUP):
            q2, k2, b_lo, b_hi = seg[1](g, t)
            zero = jnp.zeros_like(q2)
            for h, (q1, bias) in enumerate(((jnp.where(lo_q, q2, zero), b_lo),
                                            (jnp.where(lo_q, zero, q2), b_hi))):
                s = lax.dot_general(q1, k2, NT_DIMS, preferred_element_type=F32) + bias
                s_scr[slot, t, h] = s
                m_scr[slot, t, h] = jnp.broadcast_to(jnp.max(s, axis=1, keepdims=True), (bq, 128))

    def stage_b(slot):
        for t in range(GROUP):
            for h in range(2):
                m = jnp.concatenate([m_scr[slot, t, h]] * (bk // 128), axis=1)
                p_scr[slot, t, h] = jnp.exp2((s_scr[slot, t, h] - m).astype(BF16))

    def stage_c(seg, g, slot):
        tiles = []
        for t in range(GROUP):
            v2 = seg[2](g, t)
            v_ones = jnp.concatenate([v2, jnp.ones_like(v2)], axis=1)
            pv0 = jnp.dot(p_scr[slot, t, 0], v_ones, preferred_element_type=F32)
            pv1 = jnp.dot(p_scr[slot, t, 1], v_ones, preferred_element_type=F32)
            tiles.append((jnp.where(lo, m_scr[slot, t, 0], m_scr[slot, t, 1]),
                          jnp.where(lo, pv0[:, 128:], pv1[:, 128:]),
                          jnp.where(lo, pv0[:, :128], pv1[:, :128])))
        seg[3](g, tiles)

    stage_a(segments[0], 0, 0)
    stage_b(0)
    stage_a(segments[0], 1, 1)
    for k, seg in enumerate(segments):
        n = seg[0]
        assert n >= 2 and n % 2 == 0

        def step(i, slot):
            stage_c(seg, i, slot)
            stage_b(1 - slot)
            stage_a(seg, i + 2, slot)

        def body(j, carry):
            step(2 * j, 0)
            step(2 * j + 1, 1)
            return carry

        if unroll:
            for j in range((n - 2) // 2):
                body(j, 0)
        else:
            lax.fori_loop(0, (n - 2) // 2, body, 0)
        nxt = segments[k + 1] if k + 1 < len(segments) else None
        stage_c(seg, n - 2, 0)
        stage_b(1)
        if nxt:
            stage_a(nxt, 0, 0)
        stage_c(seg, n - 1, 1)
        if nxt:
            stage_b(0)
            stage_a(nxt, 1, 1)


def _pipeline_scratch(bq, bk):
    return [pltpu.VMEM((2, GROUP, 2, bq, bk), F32), pltpu.VMEM((2, GROUP, 2, bq, bk), BF16),
            pltpu.VMEM((2, GROUP, 2, bq, 128), F32)]


def _merge(m_r, l_r, a_r, m_t, l_t, a_t):
    m_n = jnp.maximum(m_r, m_t)
    e_r = jnp.exp2(m_r - m_n)
    e_t = jnp.exp2(m_t - m_n)
    return m_n, l_r * e_r + l_t * e_t, a_r * e_r + a_t * e_t


def _dilated_kernel(qn_ref, kn_ref, vn_ref, qp_ref, kp_ref, vp_ref, o_ref,
                    m_ref, l_ref, a_ref, tmp_ref, mask_ref, s_scr, p_scr, m_scr, *, T):
    BQ, BK = 128, 256
    L16 = T // MAX_DIL
    row = lax.broadcasted_iota(jnp.int32, (BQ, BK), 0)
    col = lax.broadcasted_iota(jnp.int32, (BQ, BK), 1)
    band = row - col
    band4 = 4 * ((row % 32) - (col % 64)) + (row // 32 - col // 64)
    rows_per = BQ // MAX_DIL
    band1 = MAX_DIL * (row % rows_per) + row // rows_per - col
    for n in range(3):
        mask_ref[n] = jnp.where(jnp.abs(band + N_SIDE * n) <= N_SIDE, 0.0, NEG_INF)
        mask_ref[3 + n] = jnp.where(jnp.abs(band4 + N_SIDE * n) <= N_SIDE, 0.0, NEG_INF)
        mask_ref[6 + n] = jnp.where(jnp.abs(band1 + N_SIDE * n) <= N_SIDE, 0.0, NEG_INF)
    lane = lax.broadcasted_iota(jnp.int32, (BQ, 128), 1)
    lo = lane < HEAD_DIM
    lo_q = lane % HEAD_DIM < HEAD_DIM // 2
    n_groups = T // (BQ * GROUP)

    nib = L16 // BQ

    def rows16(g, t):
        tile = jnp.asarray(g * GROUP + t, jnp.int32)
        c = lax.div(tile, nib)
        ib = lax.rem(tile, nib)
        kst = jnp.clip(ib * BQ - N_SIDE, 0, L16 - BK)
        return (pl.multiple_of(c * L16 + ib * BQ, BQ), pl.multiple_of(c * L16 + kst, N_SIDE),
                lax.div(ib * BQ - kst, N_SIDE))

    def load_qk16(seq, g, t):
        q0, k0, case = rows16(g, t)
        mask = mask_ref[case]
        return qp_ref[seq, pl.ds(q0, BQ), :], kp_ref[seq, pl.ds(k0, BK), :], mask, mask

    def load_v16(seq, g, t):
        return vp_ref[seq, pl.ds(rows16(g, t)[1], BK), :]

    def finalize16(seq, g, tiles):
        for t, (m, l, a) in enumerate(tiles):
            q0 = rows16(g, t)[0]
            m_ref[pl.ds(q0, BQ), :] = m
            l_ref[pl.ds(q0, BQ), :] = l
            a_ref[pl.ds(q0, BQ), :] = a

    nb4 = L16 // 32

    def rows4(g, t):
        tile = jnp.asarray(g * GROUP + t, jnp.int32)
        r4 = lax.div(tile, nb4)
        i0 = lax.rem(tile, nb4) * 32
        kst = jnp.clip(i0 - 16, 0, L16 - 64)
        return ([pl.multiple_of((r4 + 4 * w) * L16 + i0, 32) for w in range(4)],
                [pl.multiple_of((r4 + 4 * w) * L16 + kst, 16) for w in range(4)],
                3 + lax.div(i0 - kst, 16))

    def load_qk4(seq, g, t):
        qrows, krows, case = rows4(g, t)
        mask = mask_ref[case]
        return (jnp.concatenate([qp_ref[seq, pl.ds(r, 32), :] for r in qrows], axis=0),
                jnp.concatenate([kp_ref[seq, pl.ds(r, 64), :] for r in krows], axis=0), mask, mask)

    def load_v4(seq, g, t):
        return jnp.concatenate([vp_ref[seq, pl.ds(r, 64), :] for r in rows4(g, t)[1]], axis=0)

    def finalize4(seq, g, tiles):
        merged = []
        for t, tile in enumerate(tiles):
            qrows = rows4(g, t)[0]
            state = [jnp.concatenate([ref[pl.ds(r, 32), :] for r in qrows], axis=0)
                     for ref in (m_ref, l_ref, a_ref)]
            merged.append((qrows, _merge(*state, *tile)))
        for qrows, new in merged:
            for ref, val in zip((m_ref, l_ref, a_ref), new):
                for w, r in enumerate(qrows):
                    ref[pl.ds(r, 32), :] = val[w * 32:(w + 1) * 32]

    def rows1(g, t):
        t0 = pl.multiple_of(jnp.asarray(g * GROUP + t, jnp.int32) * BQ, BQ)
        kst = pl.multiple_of(jnp.clip(t0 - N_SIDE, 0, T - BK), N_SIDE)
        return t0, kst, 6 + lax.div(t0 - kst, N_SIDE)

    def load_qk1(seq, g, t):
        t0, kst, case = rows1(g, t)
        mask = mask_ref[case]
        return qn_ref[seq, pl.ds(t0, BQ), :], kn_ref[seq, pl.ds(kst, BK), :], mask, mask

    def load_v1(seq, g, t):
        return vn_ref[seq, pl.ds(rows1(g, t)[1], BK), :]

    def finalize1(seq, g, tiles):
        for t, tile in enumerate(tiles):
            t0 = rows1(g, t)[0]
            i0 = lax.div(t0, MAX_DIL)
            state = [jnp.concatenate([ref[pl.ds(pl.multiple_of(c * L16 + i0, rows_per), rows_per), :]
                                      for c in range(MAX_DIL)], axis=0)
                     for ref in (m_ref, l_ref, a_ref)]
            _, l_n, a_n = _merge(*state, *tile)
            out = a_n / l_n
            for c in range(MAX_DIL):
                tmp_ref[t, pl.ds(c, rows_per, stride=MAX_DIL), :] = out[c * rows_per:(c + 1) * rows_per]
            o_ref[seq, pl.ds(t0, BQ), :] = tmp_ref[t].astype(BF16)

    segments = [(n_groups,) + tuple(functools.partial(f, seq) for f in fns)
                for seq in range(o_ref.shape[0])
                for fns in ((load_qk16, load_v16, finalize16), (load_qk4, load_v4, finalize4),
                            (load_qk1, load_v1, finalize1))]
    _attn_pipeline(segments, lo_q, lo, s_scr, p_scr, m_scr, unroll=True)


def _dilated(qkv_nat, qkv_perm):
    B, T, _ = qkv_nat.shape
    nb = max(1, STEP_TOKENS // T)
    assert T % (MAX_DIL * 128) == 0 and T % (128 * GROUP) == 0 and T // MAX_DIL >= 256 and B % nb == 0
    seq = lambda col0: pl.BlockSpec((nb, T, 128), lambda b, j: (b, 0, col0 + j))
    return pl.pallas_call(
        functools.partial(_dilated_kernel, T=T),
        grid=(B // nb, N_PAIRS),
        in_specs=[seq(0), seq(N_PAIRS), seq(2 * N_PAIRS), seq(0), seq(N_PAIRS), seq(2 * N_PAIRS)],
        out_specs=pl.BlockSpec((nb, T, 128), lambda b, j: (b, 0, j)),
        out_shape=jax.ShapeDtypeStruct((B, T, WIDTH_A), BF16),
        scratch_shapes=[pltpu.VMEM((T, 128), F32), pltpu.VMEM((T, 128), F32),
                        pltpu.VMEM((T, 128), F32), pltpu.VMEM((GROUP, 128, 128), F32),
                        pltpu.VMEM((9, 128, 256), F32)] + _pipeline_scratch(128, 256),
        compiler_params=_params(2),
        name="dilated",
    )(qkv_nat, qkv_nat, qkv_nat, qkv_perm, qkv_perm, qkv_perm)


NBR_QROWS = 2
NBR_KROWS = NA_ROWS + 2
NBR_BQ = NBR_QROWS * GRID_W
NBR_BK = NBR_KROWS * GRID_W
NBR_VARIANTS = ((0, (0, 0)), (2, (0, 0)), (4, (0, 1)), (6, (2, 2)), (8, (2, 2)))


def _nbr_kernel(q_ref, k_ref, v_ref, bias_ref, o_ref, s_scr, p_scr, m_scr, *, T):
    rows = T // GRID_W
    tiles_per_seq = rows // NBR_QROWS
    lo = lax.broadcasted_iota(jnp.int32, (NBR_BQ, 128), 1) < HEAD_DIM

    def where(g, t):
        tile = jnp.asarray(g * GROUP + t, jnp.int32)
        seq = lax.div(tile, tiles_per_seq)
        r = NBR_QROWS * lax.rem(tile, tiles_per_seq)
        first_key_row = jnp.clip(r - NA_ROWS // 2, 0, rows - NBR_KROWS)
        variant = lax.div(r - first_key_row, 2)
        return (seq, pl.multiple_of(r * GRID_W, NBR_BQ),
                pl.multiple_of(first_key_row * GRID_W, GRID_W), variant)

    def load_qk(g, t):
        seq, q0, k0, variant = where(g, t)
        return (q_ref[seq, pl.ds(q0, NBR_BQ), :], k_ref[seq, pl.ds(k0, NBR_BK), :],
                bias_ref[0, variant], bias_ref[1, variant])

    def load_v(g, t):
        seq, _, k0, _ = where(g, t)
        return v_ref[seq, pl.ds(k0, NBR_BK), :]

    def finalize(g, tiles):
        for t, (_, l, a) in enumerate(tiles):
            seq, q0, _, _ = where(g, t)
            o_ref[seq, pl.ds(q0, NBR_BQ), :] = (a / l).astype(BF16)

    n_seq = q_ref.shape[0]
    _attn_pipeline([(n_seq * tiles_per_seq // GROUP, load_qk, load_v, finalize)],
                   lo, lo, s_scr, p_scr, m_scr, unroll=False)


def _nbr(qkv_nat, bias):
    B, T, _ = qkv_nat.shape
    nb = max(1, STEP_TOKENS // T)
    assert T % (NBR_BQ * GROUP) == 0 and T // GRID_W >= 2 * NBR_KROWS and B % nb == 0
    seq = lambda col0: pl.BlockSpec((nb, T, 128), lambda b, j: (b, 0, col0 + j))
    return pl.pallas_call(
        functools.partial(_nbr_kernel, T=T),
        grid=(B // nb, N_PAIRS),
        in_specs=[seq(3 * N_PAIRS), seq(4 * N_PAIRS), seq(5 * N_PAIRS),
                  pl.BlockSpec((2, len(NBR_VARIANTS), NBR_BQ, NBR_BK), lambda b, j: (j, 0, 0, 0))],
        out_specs=pl.BlockSpec((nb, T, 128), lambda b, j: (b, 0, j)),
        out_shape=jax.ShapeDtypeStruct((B, T, WIDTH_B), BF16),
        scratch_shapes=_pipeline_scratch(NBR_BQ, NBR_BK),
        compiler_params=_params(2),
        name="nbr",
    )(qkv_nat, qkv_nat, qkv_nat, bias)


def _nbr_bias(rpb):
    qc = jnp.arange(GRID_W)[:, None]
    kc = jnp.arange(GRID_W)[None, :]
    cidx = jnp.clip(kc - qc, -(NA_COLS - 1), NA_COLS - 1) + NA_COLS - 1
    cs = jnp.clip(qc - NA_COLS // 2, 0, GRID_W - NA_COLS)
    col_ok = (kc >= cs) & (kc < cs + NA_COLS)
    scaled = (LOG2E * rpb.astype(F32))[:, None, :, None, :]
    base = sum(jnp.where((cidx == d)[None, :, None, :], scaled[..., d], 0.0)
               for d in range(2 * NA_COLS - 1))
    base = jnp.where(col_ok[None, :, None, :], base, NEG_INF)
    n_heads = rpb.shape[0]
    blocks = []
    for off, first in NBR_VARIANTS:
        for q in range(NBR_QROWS):
            start = first[q] - off - q + NA_ROWS - 1
            pads = [jnp.full((n_heads, GRID_W, n, GRID_W), NEG_INF, F32)
                    for n in (first[q], NBR_KROWS - NA_ROWS - first[q])]
            rows = jnp.concatenate([pads[0], base[:, :, start:start + NA_ROWS], pads[1]], axis=2)
            blocks.append(rows.reshape(n_heads, GRID_W, NBR_BK))
    return jnp.stack(blocks, axis=1).reshape(n_heads, len(NBR_VARIANTS), NBR_BQ, NBR_BK)


UP_CHUNKS = 4


def _tail_kernel(oa_ref, ob_ref, res_ref, kv_ref, ga_ref, gb_ref, wo_ref, g1_ref, b1_ref,
                 wq_ref, wxo_ref, g2_ref, b2_ref, wu_ref, wd_ref, g3_ref, b3_ref, o_ref, x2_scr, r_scr):
    s = pl.program_id(0)
    cur, prev = lax.rem(s, 2), lax.rem(s + 1, 2)

    @pl.when(s == 0)
    def _():
        x2_scr[1] = jnp.zeros(x2_scr.shape[1:], F32)
        r_scr[...] = jnp.zeros(r_scr.shape, F32)

    x2_prev = x2_scr[prev]
    xb_prev = x2_prev.astype(BF16)
    cw = D_FF // UP_CHUNKS
    hidden = []

    def up_chunk():
        c = len(hidden)
        h = jnp.maximum(jnp.dot(xb_prev, wu_ref[:, c * cw:(c + 1) * cw],
                                preferred_element_type=F32), 0.0)
        hidden.append((h * h).astype(BF16))

    up_chunk()
    y = jnp.concatenate([_rms(oa_ref[0].astype(F32), ga_ref[...]),
                         _rms(ob_ref[0].astype(F32), gb_ref[...])], axis=1).astype(BF16)
    z = jnp.dot(y, wo_ref[...], preferred_element_type=F32)
    x1 = _ln(res_ref[0] + z, g1_ref[...], b1_ref[...])
    up_chunk()
    q = jnp.dot(x1.astype(BF16), wq_ref[...], preferred_element_type=F32)
    q = (q * (HEAD_DIM_X ** -0.5)).astype(BF16)
    outs = []
    for h in range(N_HEADS_X):
        k = kv_ref[0, :, h * HEAD_DIM_X:(h + 1) * HEAD_DIM_X]
        v = kv_ref[0, :, D_MODEL + h * HEAD_DIM_X:D_MODEL + (h + 1) * HEAD_DIM_X]
        sc = lax.dot_general(q[:, h * HEAD_DIM_X:(h + 1) * HEAD_DIM_X], k, NT_DIMS,
                             preferred_element_type=F32)
        pr = jnp.exp(sc - jnp.max(sc, axis=1, keepdims=True))
        l = jnp.sum(pr, axis=1, keepdims=True)
        outs.append(jnp.dot(pr.astype(BF16), v, preferred_element_type=F32) / l)
    up_chunk()
    z = jnp.dot(jnp.concatenate(outs, axis=1).astype(BF16), wxo_ref[...], preferred_element_type=F32)
    x2_scr[cur] = _ln(ALPHA * x1 + z, g2_ref[...], b2_ref[...])
    while len(hidden) < UP_CHUNKS:
        up_chunk()
    o_ref[0] = _ln(r_scr[prev], g3_ref[...], b3_ref[...])
    z = jnp.dot(jnp.concatenate(hidden, axis=1), wd_ref[...], preferred_element_type=F32)
    r_scr[cur] = ALPHA * x2_prev + z


def _tail(oa, ob, res, kv, p, tm):
    B, T, _ = res.shape
    M = kv.shape[1]
    assert T % tm == 0
    n = T // tm
    last = B * n - 1
    vec = lambda width: _const_spec((1, width))
    mat = lambda rows, cols: _const_spec((rows, cols))

    def cur(width):
        return pl.BlockSpec((1, tm, width),
                            lambda s: (jnp.minimum(s, last) // n, jnp.minimum(s, last) % n, 0))

    return pl.pallas_call(
        _tail_kernel,
        grid=(B * n + 2,),
        in_specs=[cur(WIDTH_A), cur(WIDTH_B), cur(D_MODEL),
                  pl.BlockSpec((1, M, 2 * D_MODEL), lambda s: (jnp.minimum(s, last) // n, 0, 0)),
                  vec(WIDTH_A), vec(WIDTH_B),
                  mat(D_MODEL, D_MODEL), vec(D_MODEL), vec(D_MODEL),
                  mat(D_MODEL, D_MODEL), mat(D_MODEL, D_MODEL), vec(D_MODEL), vec(D_MODEL),
                  mat(D_MODEL, D_FF), mat(D_FF, D_MODEL), vec(D_MODEL), vec(D_MODEL)],
        out_specs=pl.BlockSpec((1, tm, D_MODEL),
                               lambda s: (jnp.maximum(s - 2, 0) // n, jnp.maximum(s - 2, 0) % n, 0)),
        out_shape=jax.ShapeDtypeStruct((B, T, D_MODEL), F32),
        scratch_shapes=[pltpu.VMEM((2, tm, D_MODEL), F32), pltpu.VMEM((2, tm, D_MODEL), F32)],
        compiler_params=_params(1),
        name="tail",
    )(oa, ob, res, kv, p["g_mix_a"], p["g_mix_b"], p["w_out"],
      p["ln1_g"], p["ln1_b"], p["w_xq"], p["w_xo"], p["ln2_g"], p["ln2_b"],
      p["w_up"], p["w_down"], p["ln3_g"], p["ln3_b"])


def _rope_tables(T):
    half = HEAD_DIM // 2
    inv = ROPE_THETA ** (-jnp.arange(half, dtype=F32) / half)
    ang = jnp.arange(T, dtype=F32)[:, None] * inv[None, :]
    cos, sin = jnp.cos(ang), jnp.sin(ang)
    return (jnp.concatenate([cos, cos, cos, cos], axis=1),
            jnp.concatenate([-sin, -sin, sin, sin], axis=1))


def _pair_rotary_layout(w_in):
    half = HEAD_DIM // 2
    idx = jnp.arange(128).reshape(4, half)[jnp.array([0, 2, 1, 3])].reshape(128)
    cols = (jnp.arange(2 * WIDTH_A // 128)[:, None] * 128 + idx[None, :]).reshape(-1)
    return jnp.concatenate([w_in[:, cols], w_in[:, 2 * WIDTH_A:]], axis=1)


QKV_TILE = 1024
TAIL_TILE = 512


def _trunk(x, mem, p):
    B, T, _ = x.shape
    kv = _mem_kv(mem, p["w_xkv"])
    qkv_nat, qkv_perm, res = _qkv(x, p["ln_in_g"], p["ln_in_b"], p["w_in"], p["cos"], p["sin"], QKV_TILE)
    oa = _dilated(qkv_nat, qkv_perm.reshape(B, T, 3 * WIDTH_A))
    ob = _nbr(qkv_nat, p["bias"])
    return _tail(oa, ob, res, kv, p, TAIL_TILE)


def kernel(x_prompt, x_sample, mem_prompt, mem_sample, ln_in_g, ln_in_b, w_in, rpb, g_mix_a, g_mix_b,
           w_out, ln1_g, ln1_b, w_xq, w_xkv, w_xo, ln2_g, ln2_b, w_up, w_down, ln3_g, ln3_b):
    assert w_in.shape[0] == 1, "single-layer trunk"
    row = lambda v: v.reshape(1, -1).astype(F32)
    cos, sin = _rope_tables(max(x_prompt.shape[1], x_sample.shape[1]))
    p = dict(
        cos=cos, sin=sin, ln_in_g=row(ln_in_g), ln_in_b=row(ln_in_b),
        w_in=_pair_rotary_layout(w_in[0].astype(BF16)), bias=_nbr_bias(rpb[0]),
        g_mix_a=row(g_mix_a[0]), g_mix_b=row(g_mix_b[0]), w_out=w_out[0].astype(BF16),
        ln1_g=row(ln1_g[0]), ln1_b=row(ln1_b[0]),
        w_xq=w_xq[0].astype(BF16), w_xkv=w_xkv[0].astype(BF16), w_xo=w_xo[0].astype(BF16),
        ln2_g=row(ln2_g[0]), ln2_b=row(ln2_b[0]),
        w_up=w_up[0].astype(BF16), w_down=w_down[0].astype(BF16),
        ln3_g=row(ln3_g[0]), ln3_b=row(ln3_b[0]),
    )
    return _trunk(x_prompt, mem_prompt, p), _trunk(x_sample, mem_sample, p)
```

```python
import functools

import jax
import jax.numpy as jnp
from jax import lax
from jax.experimental import pallas as pl
from jax.experimental.pallas import tpu as pltpu

F32 = jnp.float32
BF16 = jnp.bfloat16

D_MODEL = 1024
HEAD_DIM = 64
WIDTH_A = 512
WIDTH_B = 512
N_PAIRS = WIDTH_A // 128
GRID_W = 64
NA_ROWS = 8
NA_COLS = 16
N_HEADS_X = 4
HEAD_DIM_X = 256
D_FF = 4096
ROPE_THETA = 10000.0
LN_EPS = 1e-5
ALPHA = 2.0 ** 0.25
NEG_INF = -1e30
LOG2E = 1.4426950408889634
N_SIDE = 64
MAX_DIL = 16

VMEM_LIMIT = 56 * 1024 * 1024

NT_DIMS = (((1,), (1,)), ((), ()))


def _ln(x, g, b):
    mu = jnp.mean(x, axis=-1, keepdims=True)
    xc = x - mu
    var = jnp.mean(xc * xc, axis=-1, keepdims=True)
    return xc * lax.rsqrt(var + LN_EPS) * g + b


def _rms(x, g):
    return x * lax.rsqrt(jnp.mean(x * x, axis=-1, keepdims=True) + LN_EPS) * g


def _params(n_axes):
    return pltpu.CompilerParams(dimension_semantics=("arbitrary",) * n_axes,
                                vmem_limit_bytes=VMEM_LIMIT)


def _const_spec(shape):
    nd = len(shape)
    return pl.BlockSpec(shape, lambda *_: (0,) * nd, pipeline_mode=pl.Buffered(1))


def _mem_kv_kernel(mem_ref, w_ref, o_ref):
    o_ref[0] = jnp.dot(mem_ref[0].astype(BF16), w_ref[...],
                       preferred_element_type=F32).astype(BF16)


def _mem_kv(mem, w_xkv):
    B, M, _ = mem.shape
    return pl.pallas_call(
        _mem_kv_kernel,
        grid=(B,),
        in_specs=[pl.BlockSpec((1, M, D_MODEL), lambda b: (b, 0, 0)),
                  _const_spec((D_MODEL, 2 * D_MODEL))],
        out_specs=pl.BlockSpec((1, M, 2 * D_MODEL), lambda b: (b, 0, 0)),
        out_shape=jax.ShapeDtypeStruct((B, M, 2 * D_MODEL), BF16),
        compiler_params=_params(1),
        name="mem_kv",
    )(mem, w_xkv)


PERM_SUB = 256


def _qkv_kernel(x_ref, g_ref, b_ref, w_ref, cos_ref, sin_ref, nat_ref, perm_ref, res_ref, *, tm):
    x0 = _ln(x_ref[0], g_ref[...], b_ref[...])
    res_ref[0] = ALPHA * x0
    xb = x0.astype(BF16)
    cos = jnp.concatenate([cos_ref[...]] * 4, axis=1)
    sin = jnp.concatenate([sin_ref[...]] * 4, axis=1)
    row = lax.broadcasted_iota(jnp.int32, (PERM_SUB, PERM_SUB), 0)
    col = lax.broadcasted_iota(jnp.int32, (PERM_SUB, PERM_SUB), 1)
    rows_per = PERM_SUB // MAX_DIL
    perm_mat = jnp.where(col == MAX_DIL * (row % rows_per) + row // rows_per, 1.0, 0.0).astype(BF16)
    for part in (0, 3, 1, 4, 2, 5):
        cols = slice(part * 512, (part + 1) * 512)
        r = jnp.dot(xb, w_ref[:, cols], preferred_element_type=F32)
        if part in (0, 1):
            swapped = jnp.concatenate([pltpu.roll(r[:, c:c + 128], 64, 1)
                                       for c in range(0, WIDTH_A, 128)], axis=1)
            r = r * cos + swapped * sin
        if part in (0, 3):
            r = r * (HEAD_DIM ** -0.5 * LOG2E)
        rb = r.astype(BF16)
        if part > 0:
            nat_ref[0, :, cols] = rb
        if part < 3:
            for g in range(tm // PERM_SUB):
                moved = jnp.dot(perm_mat, rb[g * PERM_SUB:(g + 1) * PERM_SUB],
                                preferred_element_type=F32)
                for c in range(MAX_DIL):
                    perm_ref[0, c, g * rows_per:(g + 1) * rows_per, cols] = (
                        moved[c * rows_per:(c + 1) * rows_per].astype(BF16))
                if part == 0:
                    half = rows_per // 2
                    for h in range(2):
                        tile = jnp.concatenate(
                            [moved[c * rows_per + h * half:c * rows_per + (h + 1) * half]
                             for c in range(MAX_DIL)], axis=0)
                        row0 = g * PERM_SUB + h * (PERM_SUB // 2)
                        nat_ref[0, row0:row0 + PERM_SUB // 2, cols] = tile.astype(BF16)


def _qkv(x, ln_g, ln_b, w_in, cos, sin, tm):
    B, T, _ = x.shape
    L16 = T // MAX_DIL
    assert tm % PERM_SUB == 0 and T % tm == 0
    return pl.pallas_call(
        functools.partial(_qkv_kernel, tm=tm),
        grid=(B, T // tm),
        in_specs=[pl.BlockSpec((1, tm, D_MODEL), lambda b, i: (b, i, 0)),
                  _const_spec((1, D_MODEL)), _const_spec((1, D_MODEL)),
                  _const_spec((D_MODEL, 3 * D_MODEL)),
                  pl.BlockSpec((tm, 128), lambda b, i: (i, 0)),
                  pl.BlockSpec((tm, 128), lambda b, i: (i, 0))],
        out_specs=[pl.BlockSpec((1, tm, 3 * D_MODEL), lambda b, i: (b, i, 0)),
                   pl.BlockSpec((1, MAX_DIL, tm // MAX_DIL, 3 * WIDTH_A), lambda b, i: (b, 0, i, 0)),
                   pl.BlockSpec((1, tm, D_MODEL), lambda b, i: (b, i, 0))],
        out_shape=[jax.ShapeDtypeStruct((B, T, 3 * D_MODEL), BF16),
                   jax.ShapeDtypeStruct((B, MAX_DIL, L16, 3 * WIDTH_A), BF16),
                   jax.ShapeDtypeStruct((B, T, D_MODEL), F32)],
        compiler_params=_params(2),
        name="qkv",
    )(x, ln_g, ln_b, w_in, cos, sin)


GROUP = 8
STEP_TOKENS = 8192


def _attn_pipeline(segments, lo_q, lo, s_scr, p_scr, m_scr, unroll):
    bq, bk = s_scr.shape[-2:]

    def stage_a(seg, g, slot):
        for t in range(GROUP):
            q2, k2, b_lo, b_hi = seg[1](g, t)
            zero = jnp.zeros_like(q2)
            for h, (q1, bias) in enumerate(((jnp.where(lo_q, q2, zero), b_lo),
                                            (jnp.where(lo_q, zero, q2), b_hi))):
                s = lax.dot_general(q1, k2, NT_DIMS, preferred_element_type=F32) + bias
                s_scr[slot, t, h] = s
                m_scr[slot, t, h] = jnp.broadcast_to(jnp.max(s, axis=1, keepdims=True), (bq, 128))

    def stage_b(slot):
        for t in range(GROUP):
            for h in range(2):
                m = jnp.concatenate([m_scr[slot, t, h]] * (bk // 128), axis=1)
                p_scr[slot, t, h] = jnp.exp2((s_scr[slot, t, h] - m).astype(BF16))

    def stage_c(seg, g, slot):
        tiles = []
        for t in range(GROUP):
            v2 = seg[2](g, t)
            v_ones = jnp.concatenate([v2, jnp.ones_like(v2)], axis=1)
            pv0 = jnp.dot(p_scr[slot, t, 0], v_ones, preferred_element_type=F32)
            pv1 = jnp.dot(p_scr[slot, t, 1], v_ones, preferred_element_type=F32)
            tiles.append((jnp.where(lo, m_scr[slot, t, 0], m_scr[slot, t, 1]),
                          jnp.where(lo, pv0[:, 128:], pv1[:, 128:]),
                          jnp.where(lo, pv0[:, :128], pv1[:, :128])))
        seg[3](g, tiles)

    stage_a(segments[0], 0, 0)
    stage_b(0)
    stage_a(segments[0], 1, 1)
    for k, seg in enumerate(segments):
        n = seg[0]
        assert n >= 2 and n % 2 == 0

        def step(i, slot):
            stage_c(seg, i, slot)
            stage_b(1 - slot)
            stage_a(seg, i + 2, slot)

        def body(j, carry):
            step(2 * j, 0)
            step(2 * j + 1, 1)
            return carry

        if unroll:
            for j in range((n - 2) // 2):
                body(j, 0)
        else:
            lax.fori_loop(0, (n - 2) // 2, body, 0)
        nxt = segments[k + 1] if k + 1 < len(segments) else None
        stage_c(seg, n - 2, 0)
        stage_b(1)
        if nxt:
            stage_a(nxt, 0, 0)
        stage_c(seg, n - 1, 1)
        if nxt:
            stage_b(0)
            stage_a(nxt, 1, 1)


def _pipeline_scratch(bq, bk):
    return [pltpu.VMEM((2, GROUP, 2, bq, bk), F32), pltpu.VMEM((2, GROUP, 2, bq, bk), BF16),
            pltpu.VMEM((2, GROUP, 2, bq, 128), F32)]


def _merge(m_r, l_r, a_r, m_t, l_t, a_t):
    m_n = jnp.maximum(m_r, m_t)
    e_r = jnp.exp2(m_r - m_n)
    e_t = jnp.exp2(m_t - m_n)
    return m_n, l_r * e_r + l_t * e_t, a_r * e_r + a_t * e_t


def _dilated_kernel(qn_ref, kn_ref, vn_ref, qp_ref, kp_ref, vp_ref, o_ref,
                    m_ref, l_ref, a_ref, tmp_ref, mask_ref, s_scr, p_scr, m_scr, *, T):
    BQ, BK = 128, 256
    L16 = T // MAX_DIL
    row = lax.broadcasted_iota(jnp.int32, (BQ, BK), 0)
    col = lax.broadcasted_iota(jnp.int32, (BQ, BK), 1)
    band = row - col
    band4 = 4 * ((row % 32) - (col % 64)) + (row // 32 - col // 64)
    rows_per = BQ // MAX_DIL
    band1 = MAX_DIL * (row % rows_per) + row // rows_per - col
    for n in range(3):
        mask_ref[n] = jnp.where(jnp.abs(band + N_SIDE * n) <= N_SIDE, 0.0, NEG_INF)
        mask_ref[3 + n] = jnp.where(jnp.abs(band4 + N_SIDE * n) <= N_SIDE, 0.0, NEG_INF)
        mask_ref[6 + n] = jnp.where(jnp.abs(band1 + N_SIDE * n) <= N_SIDE, 0.0, NEG_INF)
    lane = lax.broadcasted_iota(jnp.int32, (BQ, 128), 1)
    lo = lane < HEAD_DIM
    lo_q = lane % HEAD_DIM < HEAD_DIM // 2
    n_groups = T // (BQ * GROUP)

    nib = L16 // BQ

    def rows16(g, t):
        tile = jnp.asarray(g * GROUP + t, jnp.int32)
        c = lax.div(tile, nib)
        ib = lax.rem(tile, nib)
        kst = jnp.clip(ib * BQ - N_SIDE, 0, L16 - BK)
        return (pl.multiple_of(c * L16 + ib * BQ, BQ), pl.multiple_of(c * L16 + kst, N_SIDE),
                lax.div(ib * BQ - kst, N_SIDE))

    def load_qk16(seq, g, t):
        q0, k0, case = rows16(g, t)
        mask = mask_ref[case]
        return qp_ref[seq, pl.ds(q0, BQ), :], kp_ref[seq, pl.ds(k0, BK), :], mask, mask

    def load_v16(seq, g, t):
        return vp_ref[seq, pl.ds(rows16(g, t)[1], BK), :]

    def finalize16(seq, g, tiles):
        for t, (m, l, a) in enumerate(tiles):
            q0 = rows16(g, t)[0]
            m_ref[pl.ds(q0, BQ), :] = m
            l_ref[pl.ds(q0, BQ), :] = l
            a_ref[pl.ds(q0, BQ), :] = a

    nb4 = L16 // 32

    def rows4(g, t):
        tile = jnp.asarray(g * GROUP + t, jnp.int32)
        r4 = lax.div(tile, nb4)
        i0 = lax.rem(tile, nb4) * 32
        kst = jnp.clip(i0 - 16, 0, L16 - 64)
        return ([pl.multiple_of((r4 + 4 * w) * L16 + i0, 32) for w in range(4)],
                [pl.multiple_of((r4 + 4 * w) * L16 + kst, 16) for w in range(4)],
                3 + lax.div(i0 - kst, 16))

    def load_qk4(seq, g, t):
        qrows, krows, case = rows4(g, t)
        mask = mask_ref[case]
        return (jnp.concatenate([qp_ref[seq, pl.ds(r, 32), :] for r in qrows], axis=0),
                jnp.concatenate([kp_ref[seq, pl.ds(r, 64), :] for r in krows], axis=0), mask, mask)

    def load_v4(seq, g, t):
        return jnp.concatenate([vp_ref[seq, pl.ds(r, 64), :] for r in rows4(g, t)[1]], axis=0)

    def finalize4(seq, g, tiles):
        merged = []
        for t, tile in enumerate(tiles):
            qrows = rows4(g, t)[0]
            state = [jnp.concatenate([ref[pl.ds(r, 32), :] for r in qrows], axis=0)
                     for ref in (m_ref, l_ref, a_ref)]
            merged.append((qrows, _merge(*state, *tile)))
        for qrows, new in merged:
            for ref, val in zip((m_ref, l_ref, a_ref), new):
                for w, r in enumerate(qrows):
                    ref[pl.ds(r, 32), :] = val[w * 32:(w + 1) * 32]

    def rows1(g, t):
        t0 = pl.multiple_of(jnp.asarray(g * GROUP + t, jnp.int32) * BQ, BQ)
        kst = pl.multiple_of(jnp.clip(t0 - N_SIDE, 0, T - BK), N_SIDE)
        return t0, kst, 6 + lax.div(t0 - kst, N_SIDE)

    def load_qk1(seq, g, t):
        t0, kst, case = rows1(g, t)
        mask = mask_ref[case]
        return qn_ref[seq, pl.ds(t0, BQ), :], kn_ref[seq, pl.ds(kst, BK), :], mask, mask

    def load_v1(seq, g, t):
        return vn_ref[seq, pl.ds(rows1(g, t)[1], BK), :]

    def finalize1(seq, g, tiles):
        for t, tile in enumerate(tiles):
            t0 = rows1(g, t)[0]
            i0 = lax.div(t0, MAX_DIL)
            state = [jnp.concatenate([ref[pl.ds(pl.multiple_of(c * L16 + i0, rows_per), rows_per), :]
                                      for c in range(MAX_DIL)], axis=0)
                     for ref in (m_ref, l_ref, a_ref)]
            _, l_n, a_n = _merge(*state, *tile)
            out = a_n / l_n
            for c in range(MAX_DIL):
                tmp_ref[t, pl.ds(c, rows_per, stride=MAX_DIL), :] = out[c * rows_per:(c + 1) * rows_per]
            o_ref[seq, pl.ds(t0, BQ), :] = tmp_ref[t].astype(BF16)

    segments = [(n_groups,) + tuple(functools.partial(f, seq) for f in fns)
                for seq in range(o_ref.shape[0])
                for fns in ((load_qk16, load_v16, finalize16), (load_qk4, load_v4, finalize4),
                            (load_qk1, load_v1, finalize1))]
    _attn_pipeline(segments, lo_q, lo, s_scr, p_scr, m_scr, unroll=True)


def _dilated(qkv_nat, qkv_perm):
    B, T, _ = qkv_nat.shape
    nb = max(1, STEP_TOKENS // T)
    assert T % (MAX_DIL * 128) == 0 and T % (128 * GROUP) == 0 and T // MAX_DIL >= 256 and B % nb == 0
    seq = lambda col0: pl.BlockSpec((nb, T, 128), lambda b, j: (b, 0, col0 + j))
    return pl.pallas_call(
        functools.partial(_dilated_kernel, T=T),
        grid=(B // nb, N_PAIRS),
        in_specs=[seq(0), seq(N_PAIRS), seq(2 * N_PAIRS), seq(0), seq(N_PAIRS), seq(2 * N_PAIRS)],
        out_specs=pl.BlockSpec((nb, T, 128), lambda b, j: (b, 0, j)),
        out_shape=jax.ShapeDtypeStruct((B, T, WIDTH_A), BF16),
        scratch_shapes=[pltpu.VMEM((T, 128), F32), pltpu.VMEM((T, 128), F32),
                        pltpu.VMEM((T, 128), F32), pltpu.VMEM((GROUP, 128, 128), F32),
                        pltpu.VMEM((9, 128, 256), F32)] + _pipeline_scratch(128, 256),
        compiler_params=_params(2),
        name="dilated",
    )(qkv_nat, qkv_nat, qkv_nat, qkv_perm, qkv_perm, qkv_perm)


NBR_QROWS = 2
NBR_KROWS = NA_ROWS + 2
NBR_BQ = NBR_QROWS * GRID_W
NBR_BK = NBR_KROWS * GRID_W
NBR_VARIANTS = ((0, (0, 0)), (2, (0, 0)), (4, (0, 1)), (6, (2, 2)), (8, (2, 2)))


def _nbr_kernel(q_ref, k_ref, v_ref, bias_ref, o_ref, s_scr, p_scr, m_scr, *, T):
    rows = T // GRID_W
    tiles_per_seq = rows // NBR_QROWS
    lo = lax.broadcasted_iota(jnp.int32, (NBR_BQ, 128), 1) < HEAD_DIM

    def where(g, t):
        tile = jnp.asarray(g * GROUP + t, jnp.int32)
        seq = lax.div(tile, tiles_per_seq)
        r = NBR_QROWS * lax.rem(tile, tiles_per_seq)
        first_key_row = jnp.clip(r - NA_ROWS // 2, 0, rows - NBR_KROWS)
        variant = lax.div(r - first_key_row, 2)
        return (seq, pl.multiple_of(r * GRID_W, NBR_BQ),
                pl.multiple_of(first_key_row * GRID_W, GRID_W), variant)

    def load_qk(g, t):
        seq, q0, k0, variant = where(g, t)
        return (q_ref[seq, pl.ds(q0, NBR_BQ), :], k_ref[seq, pl.ds(k0, NBR_BK), :],
                bias_ref[0, variant], bias_ref[1, variant])

    def load_v(g, t):
        seq, _, k0, _ = where(g, t)
        return v_ref[seq, pl.ds(k0, NBR_BK), :]

    def finalize(g, tiles):
        for t, (_, l, a) in enumerate(tiles):
            seq, q0, _, _ = where(g, t)
            o_ref[seq, pl.ds(q0, NBR_BQ), :] = (a / l).astype(BF16)

    n_seq = q_ref.shape[0]
    _attn_pipeline([(n_seq * tiles_per_seq // GROUP, load_qk, load_v, finalize)],
                   lo, lo, s_scr, p_scr, m_scr, unroll=False)


def _nbr(qkv_nat, bias):
    B, T, _ = qkv_nat.shape
    nb = max(1, STEP_TOKENS // T)
    assert T % (NBR_BQ * GROUP) == 0 and T // GRID_W >= 2 * NBR_KROWS and B % nb == 0
    seq = lambda col0: pl.BlockSpec((nb, T, 128), lambda b, j: (b, 0, col0 + j))
    return pl.pallas_call(
        functools.partial(_nbr_kernel, T=T),
        grid=(B // nb, N_PAIRS),
        in_specs=[seq(3 * N_PAIRS), seq(4 * N_PAIRS), seq(5 * N_PAIRS),
                  pl.BlockSpec((2, len(NBR_VARIANTS), NBR_BQ, NBR_BK), lambda b, j: (j, 0, 0, 0))],
        out_specs=pl.BlockSpec((nb, T, 128), lambda b, j: (b, 0, j)),
        out_shape=jax.ShapeDtypeStruct((B, T, WIDTH_B), BF16),
        scratch_shapes=_pipeline_scratch(NBR_BQ, NBR_BK),
        compiler_params=_params(2),
        name="nbr",
    )(qkv_nat, qkv_nat, qkv_nat, bias)


def _nbr_bias(rpb):
    qc = jnp.arange(GRID_W)[:, None]
    kc = jnp.arange(GRID_W)[None, :]
    cidx = jnp.clip(kc - qc, -(NA_COLS - 1), NA_COLS - 1) + NA_COLS - 1
    cs = jnp.clip(qc - NA_COLS // 2, 0, GRID_W - NA_COLS)
    col_ok = (kc >= cs) & (kc < cs + NA_COLS)
    scaled = (LOG2E * rpb.astype(F32))[:, None, :, None, :]
    base = sum(jnp.where((cidx == d)[None, :, None, :], scaled[..., d], 0.0)
               for d in range(2 * NA_COLS - 1))
    base = jnp.where(col_ok[None, :, None, :], base, NEG_INF)
    n_heads = rpb.shape[0]
    blocks = []
    for off, first in NBR_VARIANTS:
        for q in range(NBR_QROWS):
            start = first[q] - off - q + NA_ROWS - 1
            pads = [jnp.full((n_heads, GRID_W, n, GRID_W), NEG_INF, F32)
                    for n in (first[q], NBR_KROWS - NA_ROWS - first[q])]
            rows = jnp.concatenate([pads[0], base[:, :, start:start + NA_ROWS], pads[1]], axis=2)
            blocks.append(rows.reshape(n_heads, GRID_W, NBR_BK))
    return jnp.stack(blocks, axis=1).reshape(n_heads, len(NBR_VARIANTS), NBR_BQ, NBR_BK)


UP_CHUNKS = 4


def _tail_kernel(oa_ref, ob_ref, res_ref, kv_ref, ga_ref, gb_ref, wo_ref, g1_ref, b1_ref,
                 wq_ref, wxo_ref, g2_ref, b2_ref, wu_ref, wd_ref, g3_ref, b3_ref, o_ref, x2_scr, r_scr):
    s = pl.program_id(0)
    cur, prev = lax.rem(s, 2), lax.rem(s + 1, 2)

    @pl.when(s == 0)
    def _():
        x2_scr[1] = jnp.zeros(x2_scr.shape[1:], F32)
        r_scr[...] = jnp.zeros(r_scr.shape, F32)

    x2_prev = x2_scr[prev]
    xb_prev = x2_prev.astype(BF16)
    cw = D_FF // UP_CHUNKS
    hidden = []

    def up_chunk():
        c = len(hidden)
        h = jnp.dot(xb_prev, wu_ref[:, c * cw:(c + 1) * cw], preferred_element_type=F32)
        h = jnp.maximum(h.astype(BF16), 0.0)
        hidden.append(h * h)

    up_chunk()
    y = jnp.concatenate([_rms(oa_ref[0].astype(F32), ga_ref[...]),
                         _rms(ob_ref[0].astype(F32), gb_ref[...])], axis=1).astype(BF16)
    z = jnp.dot(y, wo_ref[...], preferred_element_type=F32)
    x1 = _ln(res_ref[0] + z, g1_ref[...], b1_ref[...])
    up_chunk()
    q = jnp.dot(x1.astype(BF16), wq_ref[...], preferred_element_type=F32)
    q = q.astype(BF16)
    outs = []
    for h in range(N_HEADS_X):
        k = kv_ref[0, :, h * HEAD_DIM_X:(h + 1) * HEAD_DIM_X]
        v = kv_ref[0, :, D_MODEL + h * HEAD_DIM_X:D_MODEL + (h + 1) * HEAD_DIM_X]
        sc = lax.dot_general(q[:, h * HEAD_DIM_X:(h + 1) * HEAD_DIM_X], k, NT_DIMS,
                             preferred_element_type=F32)
        pr = jnp.exp(sc - jnp.max(sc, axis=1, keepdims=True))
        l = jnp.sum(pr, axis=1, keepdims=True)
        outs.append(jnp.dot(pr.astype(BF16), v, preferred_element_type=F32) / l)
    up_chunk()
    z = jnp.dot(jnp.concatenate(outs, axis=1).astype(BF16), wxo_ref[...], preferred_element_type=F32)
    x2_scr[cur] = _ln(ALPHA * x1 + z, g2_ref[...], b2_ref[...])
    while len(hidden) < UP_CHUNKS:
        up_chunk()
    o_ref[0] = _ln(r_scr[prev], g3_ref[...], b3_ref[...])
    z = jnp.dot(jnp.concatenate(hidden, axis=1), wd_ref[...], preferred_element_type=F32)
    r_scr[cur] = ALPHA * x2_prev + z


def _tail(oa, ob, res, kv, p, tm):
    B, T, _ = res.shape
    M = kv.shape[1]
    assert T % tm == 0
    n = T // tm
    last = B * n - 1
    vec = lambda width: _const_spec((1, width))
    mat = lambda rows, cols: _const_spec((rows, cols))

    def cur(width):
        return pl.BlockSpec((1, tm, width),
                            lambda s: (jnp.minimum(s, last) // n, jnp.minimum(s, last) % n, 0))

    return pl.pallas_call(
        _tail_kernel,
        grid=(B * n + 2,),
        in_specs=[cur(WIDTH_A), cur(WIDTH_B), cur(D_MODEL),
                  pl.BlockSpec((1, M, 2 * D_MODEL), lambda s: (jnp.minimum(s, last) // n, 0, 0)),
                  vec(WIDTH_A), vec(WIDTH_B),
                  mat(D_MODEL, D_MODEL), vec(D_MODEL), vec(D_MODEL),
                  mat(D_MODEL, D_MODEL), mat(D_MODEL, D_MODEL), vec(D_MODEL), vec(D_MODEL),
                  mat(D_MODEL, D_FF), mat(D_FF, D_MODEL), vec(D_MODEL), vec(D_MODEL)],
        out_specs=pl.BlockSpec((1, tm, D_MODEL),
                               lambda s: (jnp.maximum(s - 2, 0) // n, jnp.maximum(s - 2, 0) % n, 0)),
        out_shape=jax.ShapeDtypeStruct((B, T, D_MODEL), F32),
        scratch_shapes=[pltpu.VMEM((2, tm, D_MODEL), F32), pltpu.VMEM((2, tm, D_MODEL), F32)],
        compiler_params=_params(1),
        name="tail",
    )(oa, ob, res, kv, p["g_mix_a"], p["g_mix_b"], p["w_out"],
      p["ln1_g"], p["ln1_b"], p["w_xq"], p["w_xo"], p["ln2_g"], p["ln2_b"],
      p["w_up"], p["w_down"], p["ln3_g"], p["ln3_b"])


def _rope_tables(T):
    half = HEAD_DIM // 2
    inv = ROPE_THETA ** (-jnp.arange(half, dtype=F32) / half)
    ang = jnp.arange(T, dtype=F32)[:, None] * inv[None, :]
    cos, sin = jnp.cos(ang), jnp.sin(ang)
    return (jnp.concatenate([cos, cos, cos, cos], axis=1),
            jnp.concatenate([-sin, -sin, sin, sin], axis=1))


def _pair_rotary_layout(w_in):
    half = HEAD_DIM // 2
    idx = jnp.arange(128).reshape(4, half)[jnp.array([0, 2, 1, 3])].reshape(128)
    cols = (jnp.arange(2 * WIDTH_A // 128)[:, None] * 128 + idx[None, :]).reshape(-1)
    return jnp.concatenate([w_in[:, cols], w_in[:, 2 * WIDTH_A:]], axis=1)


QKV_TILE = 1024
TAIL_TILE = 512


def _trunk(x, mem, p):
    B, T, _ = x.shape
    kv = _mem_kv(mem, p["w_xkv"])
    qkv_nat, qkv_perm, res = _qkv(x, p["ln_in_g"], p["ln_in_b"], p["w_in"], p["cos"], p["sin"], QKV_TILE)
    oa = _dilated(qkv_nat, qkv_perm.reshape(B, T, 3 * WIDTH_A))
    ob = _nbr(qkv_nat, p["bias"])
    return _tail(oa, ob, res, kv, p, TAIL_TILE)


def kernel(x_prompt, x_sample, mem_prompt, mem_sample, ln_in_g, ln_in_b, w_in, rpb, g_mix_a, g_mix_b,
           w_out, ln1_g, ln1_b, w_xq, w_xkv, w_xo, ln2_g, ln2_b, w_up, w_down, ln3_g, ln3_b):
    assert w_in.shape[0] == 1, "single-layer trunk"
    row = lambda v: v.reshape(1, -1).astype(F32)
    cos, sin = _rope_tables(max(x_prompt.shape[1], x_sample.shape[1]))
    p = dict(
        cos=cos, sin=sin, ln_in_g=row(ln_in_g), ln_in_b=row(ln_in_b),
        w_in=_pair_rotary_layout(w_in[0].astype(BF16)), bias=_nbr_bias(rpb[0]),
        g_mix_a=row(g_mix_a[0]), g_mix_b=row(g_mix_b[0]), w_out=w_out[0].astype(BF16),
        ln1_g=row(ln1_g[0]), ln1_b=row(ln1_b[0]),
        w_xq=(w_xq[0] * HEAD_DIM_X ** -0.5).astype(BF16), w_xkv=w_xkv[0].astype(BF16), w_xo=w_xo[0].astype(BF16),
        ln2_g=row(ln2_g[0]), ln2_b=row(ln2_b[0]),
        w_up=w_up[0].astype(BF16), w_down=w_down[0].astype(BF16),
        ln3_g=row(ln3_g[0]), ln3_b=row(ln3_b[0]),
    )
    return _trunk(x_prompt, mem_prompt, p), _trunk(x_sample, mem_sample, p)
```

```python
import functools

import jax
import jax.numpy as jnp
from jax import lax
from jax.experimental import pallas as pl
from jax.experimental.pallas import tpu as pltpu

F32 = jnp.float32
BF16 = jnp.bfloat16

D_MODEL = 1024
HEAD_DIM = 64
WIDTH_A = 512
WIDTH_B = 512
N_PAIRS = WIDTH_A // 128
GRID_W = 64
NA_ROWS = 8
NA_COLS = 16
N_HEADS_X = 4
HEAD_DIM_X = 256
D_FF = 4096
ROPE_THETA = 10000.0
LN_EPS = 1e-5
ALPHA = 2.0 ** 0.25
NEG_INF = -1e30
LOG2E = 1.4426950408889634
N_SIDE = 64
MAX_DIL = 16

VMEM_LIMIT = 56 * 1024 * 1024

NT_DIMS = (((1,), (1,)), ((), ()))


def _ln(x, g, b):
    mu = jnp.mean(x, axis=-1, keepdims=True)
    xc = x - mu
    var = jnp.mean(xc * xc, axis=-1, keepdims=True)
    return xc * lax.rsqrt(var + LN_EPS) * g + b


def _rms(x, g):
    return x * lax.rsqrt(jnp.mean(x * x, axis=-1, keepdims=True) + LN_EPS) * g


def _params(n_axes, vmem_limit=VMEM_LIMIT):
    return pltpu.CompilerParams(dimension_semantics=("arbitrary",) * n_axes,
                                vmem_limit_bytes=vmem_limit)


def _const_spec(shape):
    nd = len(shape)
    return pl.BlockSpec(shape, lambda *_: (0,) * nd, pipeline_mode=pl.Buffered(1))


def _mem_kv_kernel(mem_ref, w_ref, o_ref):
    o_ref[0] = jnp.dot(mem_ref[0].astype(BF16), w_ref[...],
                       preferred_element_type=F32).astype(BF16)


def _mem_kv(mem, w_xkv):
    B, M, _ = mem.shape
    return pl.pallas_call(
        _mem_kv_kernel,
        grid=(B,),
        in_specs=[pl.BlockSpec((1, M, D_MODEL), lambda b: (b, 0, 0)),
                  _const_spec((D_MODEL, 2 * D_MODEL))],
        out_specs=pl.BlockSpec((1, M, 2 * D_MODEL), lambda b: (b, 0, 0)),
        out_shape=jax.ShapeDtypeStruct((B, M, 2 * D_MODEL), BF16),
        compiler_params=_params(1),
        name="mem_kv",
    )(mem, w_xkv)


PERM_SUB = 256


def _qkv_kernel(x_ref, g_ref, b_ref, w_ref, cos_ref, sin_ref, nat_ref, perm_ref, res_ref, *, tm):
    x0 = _ln(x_ref[0], g_ref[...], b_ref[...])
    res_ref[0] = ALPHA * x0
    xb = x0.astype(BF16)
    cos = jnp.concatenate([cos_ref[...]] * 4, axis=1)
    sin = jnp.concatenate([sin_ref[...]] * 4, axis=1)
    row = lax.broadcasted_iota(jnp.int32, (PERM_SUB, PERM_SUB), 0)
    col = lax.broadcasted_iota(jnp.int32, (PERM_SUB, PERM_SUB), 1)
    rows_per = PERM_SUB // MAX_DIL
    perm_mat = jnp.where(col == MAX_DIL * (row % rows_per) + row // rows_per, 1.0, 0.0).astype(BF16)
    for part in (0, 3, 1, 4, 2, 5):
        cols = slice(part * 512, (part + 1) * 512)
        r = jnp.dot(xb, w_ref[:, cols], preferred_element_type=F32)
        if part in (0, 1):
            swapped = jnp.concatenate([pltpu.roll(r[:, c:c + 128], 64, 1)
                                       for c in range(0, WIDTH_A, 128)], axis=1)
            r = r * cos + swapped * sin
        if part in (0, 3):
            r = r * (HEAD_DIM ** -0.5 * LOG2E)
        rb = r.astype(BF16)
        if part > 0:
            nat_ref[0, :, cols] = rb
        if part < 3:
            for g in range(tm // PERM_SUB):
                moved = jnp.dot(perm_mat, rb[g * PERM_SUB:(g + 1) * PERM_SUB],
                                preferred_element_type=F32)
                for c in range(MAX_DIL):
                    perm_ref[0, c, g * rows_per:(g + 1) * rows_per, cols] = (
                        moved[c * rows_per:(c + 1) * rows_per].astype(BF16))
                if part == 0:
                    half = rows_per // 2
                    for h in range(2):
                        tile = jnp.concatenate(
                            [moved[c * rows_per + h * half:c * rows_per + (h + 1) * half]
                             for c in range(MAX_DIL)], axis=0)
                        row0 = g * PERM_SUB + h * (PERM_SUB // 2)
                        nat_ref[0, row0:row0 + PERM_SUB // 2, cols] = tile.astype(BF16)


def _qkv(x, ln_g, ln_b, w_in, cos, sin, tm):
    B, T, _ = x.shape
    L16 = T // MAX_DIL
    assert tm % PERM_SUB == 0 and T % tm == 0
    return pl.pallas_call(
        functools.partial(_qkv_kernel, tm=tm),
        grid=(B, T // tm),
        in_specs=[pl.BlockSpec((1, tm, D_MODEL), lambda b, i: (b, i, 0)),
                  _const_spec((1, D_MODEL)), _const_spec((1, D_MODEL)),
                  _const_spec((D_MODEL, 3 * D_MODEL)),
                  pl.BlockSpec((tm, 128), lambda b, i: (i, 0)),
                  pl.BlockSpec((tm, 128), lambda b, i: (i, 0))],
        out_specs=[pl.BlockSpec((1, tm, 3 * D_MODEL), lambda b, i: (b, i, 0)),
                   pl.BlockSpec((1, MAX_DIL, tm // MAX_DIL, 3 * WIDTH_A), lambda b, i: (b, 0, i, 0)),
                   pl.BlockSpec((1, tm, D_MODEL), lambda b, i: (b, i, 0))],
        out_shape=[jax.ShapeDtypeStruct((B, T, 3 * D_MODEL), BF16),
                   jax.ShapeDtypeStruct((B, MAX_DIL, L16, 3 * WIDTH_A), BF16),
                   jax.ShapeDtypeStruct((B, T, D_MODEL), F32)],
        compiler_params=_params(2),
        name="qkv",
    )(x, ln_g, ln_b, w_in, cos, sin)


GROUP = 8
STEP_TOKENS = 8192


def _attn_pipeline(segments, lo_q, lo, s_scr, p_scr, m_scr, unroll):
    bq, bk = s_scr.shape[-2:]

    def stage_a(seg, g, slot):
        for t in range(GROUP):
            q2, k2, b_lo, b_hi = seg[1](g, t)
            zero = jnp.zeros_like(q2)
            for h, (q1, bias) in enumerate(((jnp.where(lo_q, q2, zero), b_lo),
                                            (jnp.where(lo_q, zero, q2), b_hi))):
                s = lax.dot_general(q1, k2, NT_DIMS, preferred_element_type=F32) + bias
                s_scr[slot, t, h] = s
                m_scr[slot, t, h] = jnp.broadcast_to(jnp.max(s, axis=1, keepdims=True), (bq, 128))

    def stage_b(slot):
        for t in range(GROUP):
            for h in range(2):
                m = jnp.concatenate([m_scr[slot, t, h]] * (bk // 128), axis=1)
                p_scr[slot, t, h] = jnp.exp2((s_scr[slot, t, h] - m).astype(BF16))

    def stage_c(seg, g, slot):
        tiles = []
        for t in range(GROUP):
            v2 = seg[2](g, t)
            v_ones = jnp.concatenate([v2, jnp.ones_like(v2)], axis=1)
            pv0 = jnp.dot(p_scr[slot, t, 0], v_ones, preferred_element_type=F32)
            pv1 = jnp.dot(p_scr[slot, t, 1], v_ones, preferred_element_type=F32)
            tiles.append((jnp.where(lo, m_scr[slot, t, 0], m_scr[slot, t, 1]),
                          jnp.where(lo, pv0[:, 128:], pv1[:, 128:]),
                          jnp.where(lo, pv0[:, :128], pv1[:, :128])))
        seg[3](g, tiles)

    stage_a(segments[0], 0, 0)
    stage_b(0)
    stage_a(segments[0], 1, 1)
    for k, seg in enumerate(segments):
        n = seg[0]
        assert n >= 2 and n % 2 == 0

        def step(i, slot):
            stage_c(seg, i, slot)
            stage_b(1 - slot)
            stage_a(seg, i + 2, slot)

        def body(j, carry):
            step(2 * j, 0)
            step(2 * j + 1, 1)
            return carry

        if unroll:
            for j in range((n - 2) // 2):
                body(j, 0)
        else:
            lax.fori_loop(0, (n - 2) // 2, body, 0)
        nxt = segments[k + 1] if k + 1 < len(segments) else None
        stage_c(seg, n - 2, 0)
        stage_b(1)
        if nxt:
            stage_a(nxt, 0, 0)
        stage_c(seg, n - 1, 1)
        if nxt:
            stage_b(0)
            stage_a(nxt, 1, 1)


def _pipeline_scratch(bq, bk):
    return [pltpu.VMEM((2, GROUP, 2, bq, bk), F32), pltpu.VMEM((2, GROUP, 2, bq, bk), BF16),
            pltpu.VMEM((2, GROUP, 2, bq, 128), F32)]


def _merge(m_r, l_r, a_r, m_t, l_t, a_t):
    m_n = jnp.maximum(m_r, m_t)
    e_r = jnp.exp2(m_r - m_n)
    e_t = jnp.exp2(m_t - m_n)
    return m_n, l_r * e_r + l_t * e_t, a_r * e_r + a_t * e_t


def _dilated_kernel(qn_ref, kn_ref, vn_ref, qp_ref, kp_ref, vp_ref, o_ref,
                    m_ref, l_ref, a_ref, tmp_ref, mask_ref, s_scr, p_scr, m_scr, *, T):
    BQ, BK = 128, 256
    L16 = T // MAX_DIL
    row = lax.broadcasted_iota(jnp.int32, (BQ, BK), 0)
    col = lax.broadcasted_iota(jnp.int32, (BQ, BK), 1)
    band = row - col
    band4 = 4 * ((row % 32) - (col % 64)) + (row // 32 - col // 64)
    rows_per = BQ // MAX_DIL
    band1 = MAX_DIL * (row % rows_per) + row // rows_per - col
    for n in range(3):
        mask_ref[n] = jnp.where(jnp.abs(band + N_SIDE * n) <= N_SIDE, 0.0, NEG_INF)
        mask_ref[3 + n] = jnp.where(jnp.abs(band4 + N_SIDE * n) <= N_SIDE, 0.0, NEG_INF)
        mask_ref[6 + n] = jnp.where(jnp.abs(band1 + N_SIDE * n) <= N_SIDE, 0.0, NEG_INF)
    lane = lax.broadcasted_iota(jnp.int32, (BQ, 128), 1)
    lo = lane < HEAD_DIM
    lo_q = lane % HEAD_DIM < HEAD_DIM // 2
    n_groups = T // (BQ * GROUP)

    nib = L16 // BQ

    def rows16(g, t):
        tile = jnp.asarray(g * GROUP + t, jnp.int32)
        c = lax.div(tile, nib)
        ib = lax.rem(tile, nib)
        kst = jnp.clip(ib * BQ - N_SIDE, 0, L16 - BK)
        return (pl.multiple_of(c * L16 + ib * BQ, BQ), pl.multiple_of(c * L16 + kst, N_SIDE),
                lax.div(ib * BQ - kst, N_SIDE))

    def load_qk16(seq, g, t):
        q0, k0, case = rows16(g, t)
        mask = mask_ref[case]
        return qp_ref[seq, pl.ds(q0, BQ), :], kp_ref[seq, pl.ds(k0, BK), :], mask, mask

    def load_v16(seq, g, t):
        return vp_ref[seq, pl.ds(rows16(g, t)[1], BK), :]

    def finalize16(seq, g, tiles):
        for t, (m, l, a) in enumerate(tiles):
            q0 = rows16(g, t)[0]
            m_ref[pl.ds(q0, BQ), :] = m
            l_ref[pl.ds(q0, BQ), :] = l
            a_ref[pl.ds(q0, BQ), :] = a

    nb4 = L16 // 32

    def rows4(g, t):
        tile = jnp.asarray(g * GROUP + t, jnp.int32)
        r4 = lax.div(tile, nb4)
        i0 = lax.rem(tile, nb4) * 32
        kst = jnp.clip(i0 - 16, 0, L16 - 64)
        return ([pl.multiple_of((r4 + 4 * w) * L16 + i0, 32) for w in range(4)],
                [pl.multiple_of((r4 + 4 * w) * L16 + kst, 16) for w in range(4)],
                3 + lax.div(i0 - kst, 16))

    def load_qk4(seq, g, t):
        qrows, krows, case = rows4(g, t)
        mask = mask_ref[case]
        return (jnp.concatenate([qp_ref[seq, pl.ds(r, 32), :] for r in qrows], axis=0),
                jnp.concatenate([kp_ref[seq, pl.ds(r, 64), :] for r in krows], axis=0), mask, mask)

    def load_v4(seq, g, t):
        return jnp.concatenate([vp_ref[seq, pl.ds(r, 64), :] for r in rows4(g, t)[1]], axis=0)

    def finalize4(seq, g, tiles):
        merged = []
        for t, tile in enumerate(tiles):
            qrows = rows4(g, t)[0]
            state = [jnp.concatenate([ref[pl.ds(r, 32), :] for r in qrows], axis=0)
                     for ref in (m_ref, l_ref, a_ref)]
            merged.append((qrows, _merge(*state, *tile)))
        for qrows, new in merged:
            for ref, val in zip((m_ref, l_ref, a_ref), new):
                for w, r in enumerate(qrows):
                    ref[pl.ds(r, 32), :] = val[w * 32:(w + 1) * 32]

    def rows1(g, t):
        t0 = pl.multiple_of(jnp.asarray(g * GROUP + t, jnp.int32) * BQ, BQ)
        kst = pl.multiple_of(jnp.clip(t0 - N_SIDE, 0, T - BK), N_SIDE)
        return t0, kst, 6 + lax.div(t0 - kst, N_SIDE)

    def load_qk1(seq, g, t):
        t0, kst, case = rows1(g, t)
        mask = mask_ref[case]
        return qn_ref[seq, pl.ds(t0, BQ), :], kn_ref[seq, pl.ds(kst, BK), :], mask, mask

    def load_v1(seq, g, t):
        return vn_ref[seq, pl.ds(rows1(g, t)[1], BK), :]

    def finalize1(seq, g, tiles):
        for t, tile in enumerate(tiles):
            t0 = rows1(g, t)[0]
            i0 = lax.div(t0, MAX_DIL)
            state = [jnp.concatenate([ref[pl.ds(pl.multiple_of(c * L16 + i0, rows_per), rows_per), :]
                                      for c in range(MAX_DIL)], axis=0)
                     for ref in (m_ref, l_ref, a_ref)]
            _, l_n, a_n = _merge(*state, *tile)
            out = a_n / l_n
            for c in range(MAX_DIL):
                tmp_ref[t, pl.ds(c, rows_per, stride=MAX_DIL), :] = out[c * rows_per:(c + 1) * rows_per]
            o_ref[seq, pl.ds(t0, BQ), :] = tmp_ref[t].astype(BF16)

    segments = [(n_groups,) + tuple(functools.partial(f, seq) for f in fns)
                for seq in range(o_ref.shape[0])
                for fns in ((load_qk16, load_v16, finalize16), (load_qk4, load_v4, finalize4),
                            (load_qk1, load_v1, finalize1))]
    _attn_pipeline(segments, lo_q, lo, s_scr, p_scr, m_scr, unroll=True)


def _dilated(qkv_nat, qkv_perm):
    B, T, _ = qkv_nat.shape
    nb = max(1, STEP_TOKENS // T)
    assert T % (MAX_DIL * 128) == 0 and T % (128 * GROUP) == 0 and T // MAX_DIL >= 256 and B % nb == 0
    seq = lambda col0: pl.BlockSpec((nb, T, 128), lambda b, j: (b, 0, col0 + j))
    return pl.pallas_call(
        functools.partial(_dilated_kernel, T=T),
        grid=(B // nb, N_PAIRS),
        in_specs=[seq(0), seq(N_PAIRS), seq(2 * N_PAIRS), seq(0), seq(N_PAIRS), seq(2 * N_PAIRS)],
        out_specs=pl.BlockSpec((nb, T, 128), lambda b, j: (b, 0, j)),
        out_shape=jax.ShapeDtypeStruct((B, T, WIDTH_A), BF16),
        scratch_shapes=[pltpu.VMEM((T, 128), F32), pltpu.VMEM((T, 128), F32),
                        pltpu.VMEM((T, 128), F32), pltpu.VMEM((GROUP, 128, 128), F32),
                        pltpu.VMEM((9, 128, 256), F32)] + _pipeline_scratch(128, 256),
        compiler_params=_params(2),
        name="dilated",
    )(qkv_nat, qkv_nat, qkv_nat, qkv_perm, qkv_perm, qkv_perm)


NBR_QROWS = 2
NBR_KROWS = NA_ROWS + 2
NBR_BQ = NBR_QROWS * GRID_W
NBR_BK = NBR_KROWS * GRID_W
NBR_VARIANTS = ((0, (0, 0)), (2, (0, 0)), (4, (0, 1)), (6, (2, 2)), (8, (2, 2)))


def _nbr_kernel(q_ref, k_ref, v_ref, bias_ref, o_ref, s_scr, p_scr, m_scr, *, T):
    rows = T // GRID_W
    tiles_per_seq = rows // NBR_QROWS
    lo = lax.broadcasted_iota(jnp.int32, (NBR_BQ, 128), 1) < HEAD_DIM

    def where(g, t):
        tile = jnp.asarray(g * GROUP + t, jnp.int32)
        seq = lax.div(tile, tiles_per_seq)
        r = NBR_QROWS * lax.rem(tile, tiles_per_seq)
        first_key_row = jnp.clip(r - NA_ROWS // 2, 0, rows - NBR_KROWS)
        variant = lax.div(r - first_key_row, 2)
        return (seq, pl.multiple_of(r * GRID_W, NBR_BQ),
                pl.multiple_of(first_key_row * GRID_W, GRID_W), variant)

    def load_qk(g, t):
        seq, q0, k0, variant = where(g, t)
        return (q_ref[seq, pl.ds(q0, NBR_BQ), :], k_ref[seq, pl.ds(k0, NBR_BK), :],
                bias_ref[0, variant], bias_ref[1, variant])

    def load_v(g, t):
        seq, _, k0, _ = where(g, t)
        return v_ref[seq, pl.ds(k0, NBR_BK), :]

    def finalize(g, tiles):
        for t, (_, l, a) in enumerate(tiles):
            seq, q0, _, _ = where(g, t)
            o_ref[seq, pl.ds(q0, NBR_BQ), :] = (a / l).astype(BF16)

    n_seq = q_ref.shape[0]
    _attn_pipeline([(n_seq * tiles_per_seq // GROUP, load_qk, load_v, finalize)],
                   lo, lo, s_scr, p_scr, m_scr, unroll=False)


def _nbr(qkv_nat, bias):
    B, T, _ = qkv_nat.shape
    nb = max(1, STEP_TOKENS // T)
    assert T % (NBR_BQ * GROUP) == 0 and T // GRID_W >= 2 * NBR_KROWS and B % nb == 0
    seq = lambda col0: pl.BlockSpec((nb, T, 128), lambda b, j: (b, 0, col0 + j))
    return pl.pallas_call(
        functools.partial(_nbr_kernel, T=T),
        grid=(B // nb, N_PAIRS),
        in_specs=[seq(3 * N_PAIRS), seq(4 * N_PAIRS), seq(5 * N_PAIRS),
                  pl.BlockSpec((2, len(NBR_VARIANTS), NBR_BQ, NBR_BK), lambda b, j: (j, 0, 0, 0))],
        out_specs=pl.BlockSpec((nb, T, 128), lambda b, j: (b, 0, j)),
        out_shape=jax.ShapeDtypeStruct((B, T, WIDTH_B), BF16),
        scratch_shapes=_pipeline_scratch(NBR_BQ, NBR_BK),
        compiler_params=_params(2),
        name="nbr",
    )(qkv_nat, qkv_nat, qkv_nat, bias)


def _nbr_bias(rpb):
    qc = jnp.arange(GRID_W)[:, None]
    kc = jnp.arange(GRID_W)[None, :]
    cidx = jnp.clip(kc - qc, -(NA_COLS - 1), NA_COLS - 1) + NA_COLS - 1
    cs = jnp.clip(qc - NA_COLS // 2, 0, GRID_W - NA_COLS)
    col_ok = (kc >= cs) & (kc < cs + NA_COLS)
    scaled = (LOG2E * rpb.astype(F32))[:, None, :, None, :]
    base = sum(jnp.where((cidx == d)[None, :, None, :], scaled[..., d], 0.0)
               for d in range(2 * NA_COLS - 1))
    base = jnp.where(col_ok[None, :, None, :], base, NEG_INF)
    n_heads = rpb.shape[0]
    blocks = []
    for off, first in NBR_VARIANTS:
        for q in range(NBR_QROWS):
            start = first[q] - off - q + NA_ROWS - 1
            pads = [jnp.full((n_heads, GRID_W, n, GRID_W), NEG_INF, F32)
                    for n in (first[q], NBR_KROWS - NA_ROWS - first[q])]
            rows = jnp.concatenate([pads[0], base[:, :, start:start + NA_ROWS], pads[1]], axis=2)
            blocks.append(rows.reshape(n_heads, GRID_W, NBR_BK))
    return jnp.stack(blocks, axis=1).reshape(n_heads, len(NBR_VARIANTS), NBR_BQ, NBR_BK)


UP_CHUNKS = 4


def _tail_kernel(oa_ref, ob_ref, res_ref, kv_ref, ga_ref, gb_ref, wo_ref, g1_ref, b1_ref,
                 wq_ref, wxo_ref, g2_ref, b2_ref, wu_ref, wd_ref, g3_ref, b3_ref, o_ref,
                 x1_scr, x2_scr, r_scr):
    s = pl.program_id(0)
    cur, prev = lax.rem(s, 2), lax.rem(s + 1, 2)

    @pl.when(s == 0)
    def _():
        for scr in (x1_scr, x2_scr, r_scr):
            scr[1] = jnp.zeros(scr.shape[1:], F32)

    x2_prev = x2_scr[prev]
    xb_prev = x2_prev.astype(BF16)
    cw = D_FF // UP_CHUNKS
    hidden = []

    def up_chunk():
        c = len(hidden)
        h = jnp.maximum(jnp.dot(xb_prev, wu_ref[:, c * cw:(c + 1) * cw],
                                preferred_element_type=F32), 0.0)
        hidden.append((h * h).astype(BF16))

    up_chunk()
    y = jnp.concatenate([_rms(oa_ref[0].astype(F32), ga_ref[...]),
                         _rms(ob_ref[0].astype(F32), gb_ref[...])], axis=1).astype(BF16)
    x1_prev = x1_scr[prev]
    q = jnp.dot(x1_prev.astype(BF16), wq_ref[...], preferred_element_type=F32)
    q = (q * (HEAD_DIM_X ** -0.5)).astype(BF16)
    z = jnp.dot(y, wo_ref[...], preferred_element_type=F32)
    x1_scr[cur] = _ln(res_ref[0] + z, g1_ref[...], b1_ref[...])
    up_chunk()
    outs = []
    for h in range(N_HEADS_X):
        k = kv_ref[0, :, h * HEAD_DIM_X:(h + 1) * HEAD_DIM_X]
        v = kv_ref[0, :, D_MODEL + h * HEAD_DIM_X:D_MODEL + (h + 1) * HEAD_DIM_X]
        sc = lax.dot_general(q[:, h * HEAD_DIM_X:(h + 1) * HEAD_DIM_X], k, NT_DIMS,
                             preferred_element_type=F32)
        pr = jnp.exp(sc - jnp.max(sc, axis=1, keepdims=True))
        l = jnp.sum(pr, axis=1, keepdims=True)
        outs.append(jnp.dot(pr.astype(BF16), v, preferred_element_type=F32) / l)
    up_chunk()
    z = jnp.dot(jnp.concatenate(outs, axis=1).astype(BF16), wxo_ref[...], preferred_element_type=F32)
    x2_scr[cur] = _ln(ALPHA * x1_prev + z, g2_ref[...], b2_ref[...])
    while len(hidden) < UP_CHUNKS:
        up_chunk()
    o_ref[0] = _ln(r_scr[prev], g3_ref[...], b3_ref[...])
    z = jnp.dot(jnp.concatenate(hidden, axis=1), wd_ref[...], preferred_element_type=F32)
    r_scr[cur] = ALPHA * x2_prev + z


TAIL_LAG = 3
TAIL_VMEM_LIMIT = 60 * 1024 * 1024


def _tail(oa, ob, res, kv, p, tm):
    B, T, _ = res.shape
    M = kv.shape[1]
    assert T % tm == 0
    n = T // tm
    last = B * n - 1
    vec = lambda width: _const_spec((1, width))
    mat = lambda rows, cols: _const_spec((rows, cols))

    tile = lambda s, lag: jnp.clip(s - lag, 0, last)

    def cur(width):
        return pl.BlockSpec((1, tm, width), lambda s: (tile(s, 0) // n, tile(s, 0) % n, 0))

    return pl.pallas_call(
        _tail_kernel,
        grid=(B * n + TAIL_LAG,),
        in_specs=[cur(WIDTH_A), cur(WIDTH_B), cur(D_MODEL),
                  pl.BlockSpec((1, M, 2 * D_MODEL), lambda s: (tile(s, 1) // n, 0, 0)),
                  vec(WIDTH_A), vec(WIDTH_B),
                  mat(D_MODEL, D_MODEL), vec(D_MODEL), vec(D_MODEL),
                  mat(D_MODEL, D_MODEL), mat(D_MODEL, D_MODEL), vec(D_MODEL), vec(D_MODEL),
                  mat(D_MODEL, D_FF), mat(D_FF, D_MODEL), vec(D_MODEL), vec(D_MODEL)],
        out_specs=pl.BlockSpec((1, tm, D_MODEL),
                               lambda s: (tile(s, TAIL_LAG) // n, tile(s, TAIL_LAG) % n, 0)),
        out_shape=jax.ShapeDtypeStruct((B, T, D_MODEL), F32),
        scratch_shapes=[pltpu.VMEM((2, tm, D_MODEL), F32)] * 3,
        compiler_params=_params(1, TAIL_VMEM_LIMIT),
        name="tail",
    )(oa, ob, res, kv, p["g_mix_a"], p["g_mix_b"], p["w_out"],
      p["ln1_g"], p["ln1_b"], p["w_xq"], p["w_xo"], p["ln2_g"], p["ln2_b"],
      p["w_up"], p["w_down"], p["ln3_g"], p["ln3_b"])


def _rope_tables(T):
    half = HEAD_DIM // 2
    inv = ROPE_THETA ** (-jnp.arange(half, dtype=F32) / half)
    ang = jnp.arange(T, dtype=F32)[:, None] * inv[None, :]
    cos, sin = jnp.cos(ang), jnp.sin(ang)
    return (jnp.concatenate([cos, cos, cos, cos], axis=1),
            jnp.concatenate([-sin, -sin, sin, sin], axis=1))


def _pair_rotary_layout(w_in):
    half = HEAD_DIM // 2
    idx = jnp.arange(128).reshape(4, half)[jnp.array([0, 2, 1, 3])].reshape(128)
    cols = (jnp.arange(2 * WIDTH_A // 128)[:, None] * 128 + idx[None, :]).reshape(-1)
    return jnp.concatenate([w_in[:, cols], w_in[:, 2 * WIDTH_A:]], axis=1)


QKV_TILE = 1024
TAIL_TILE = 512


def _trunk(x, mem, p):
    B, T, _ = x.shape
    kv = _mem_kv(mem, p["w_xkv"])
    qkv_nat, qkv_perm, res = _qkv(x, p["ln_in_g"], p["ln_in_b"], p["w_in"], p["cos"], p["sin"], QKV_TILE)
    oa = _dilated(qkv_nat, qkv_perm.reshape(B, T, 3 * WIDTH_A))
    ob = _nbr(qkv_nat, p["bias"])
    return _tail(oa, ob, res, kv, p, TAIL_TILE)


def kernel(x_prompt, x_sample, mem_prompt, mem_sample, ln_in_g, ln_in_b, w_in, rpb, g_mix_a, g_mix_b,
           w_out, ln1_g, ln1_b, w_xq, w_xkv, w_xo, ln2_g, ln2_b, w_up, w_down, ln3_g, ln3_b):
    assert w_in.shape[0] == 1, "single-layer trunk"
    row = lambda v: v.reshape(1, -1).astype(F32)
    cos, sin = _rope_tables(max(x_prompt.shape[1], x_sample.shape[1]))
    p = dict(
        cos=cos, sin=sin, ln_in_g=row(ln_in_g), ln_in_b=row(ln_in_b),
        w_in=_pair_rotary_layout(w_in[0].astype(BF16)), bias=_nbr_bias(rpb[0]),
        g_mix_a=row(g_mix_a[0]), g_mix_b=row(g_mix_b[0]), w_out=w_out[0].astype(BF16),
        ln1_g=row(ln1_g[0]), ln1_b=row(ln1_b[0]),
        w_xq=w_xq[0].astype(BF16), w_xkv=w_xkv[0].astype(BF16), w_xo=w_xo[0].astype(BF16),
        ln2_g=row(ln2_g[0]), ln2_b=row(ln2_b[0]),
        w_up=w_up[0].astype(BF16), w_down=w_down[0].astype(BF16),
        ln3_g=row(ln3_g[0]), ln3_b=row(ln3_b[0]),
    )
    return _trunk(x_prompt, mem_prompt, p), _trunk(x_sample, mem_sample, p)
```
